```python
import math
import jax
import jax.numpy as jnp
from jax import lax
import numpy as np

D_MODEL = 2048
BATCH = 8
SEQ = 2048
DEPTH = 1
DEC_BATCH = 32
DEC_SEQ = 4
PAST_LEN = 16384
PAGE_SIZE = 128

HEAD_DIM = 128
H_FOX = D_MODEL // 2 // HEAD_DIM
KV_FOX = H_FOX // 2
G_FOX = H_FOX // KV_FOX
H_DSA = D_MODEL // 2 // HEAD_DIM
KV_DSA = H_DSA // 2
G_DSA = H_DSA // KV_DSA
W_FOX = H_FOX * HEAD_DIM
W_DSA = H_DSA * HEAD_DIM
MIX_WIDTH = W_FOX + W_DSA
H_IDX = 16
D_IDX = 64
TOPK_MAX = 256
N_BUCKETS = 32
MAX_DISTANCE = 128
PLE_DIM = 256
Q_BLOCK = 128
EPS = 1e-6
FORGET_BIAS = 2.0
SPLIT_SIZES = (W_FOX, KV_FOX * HEAD_DIM, KV_FOX * HEAD_DIM, H_FOX, W_FOX,
               W_DSA, KV_DSA * HEAD_DIM, KV_DSA * HEAD_DIM, W_DSA,
               H_IDX * D_IDX, D_IDX, H_IDX)
SPLIT_POINTS = tuple(sum(SPLIT_SIZES[:i + 1]) for i in range(len(SPLIT_SIZES) - 1))
N_IN = sum(SPLIT_SIZES)

kernel_name = 'hymba_fox_dsa_step'


def rmsnorm(x, g):
    xf = x.astype(jnp.float32)
    y = xf * lax.rsqrt(jnp.mean(xf * xf, axis=-1, keepdims=True) + EPS)
    return (y * g.astype(jnp.float32)).astype(x.dtype)


def t5_bucket(dist):
    max_exact = N_BUCKETS // 2
    d = jnp.maximum(dist, 0)
    log_ratio = jnp.log(jnp.maximum(d, 1).astype(jnp.float32) / max_exact) / math.log(MAX_DISTANCE / max_exact)
    large = jnp.minimum(max_exact + (log_ratio * (N_BUCKETS - max_exact)).astype(jnp.int32), N_BUCKETS - 1)
    return jnp.where(d < max_exact, d, large)


def project(xn, w_in, b_f, gq_f, gk_f, gq_d, gk_d):
    B, T, _ = xn.shape
    fq, fk, fv, ff, fz, dq, dk, dv, dz, iq, ik, iw = jnp.split(xn @ w_in, list(SPLIT_POINTS), axis=-1)
    heads = lambda a, n: a.reshape(B, T, n, HEAD_DIM)
    fq = rmsnorm(heads(fq, H_FOX), gq_f)
    fk = rmsnorm(heads(fk, KV_FOX), gk_f)
    fv = heads(fv, KV_FOX)
    logf = jax.nn.log_sigmoid((ff + b_f).astype(jnp.float32))
    dq = rmsnorm(heads(dq, H_DSA), gq_d)
    dk = rmsnorm(heads(dk, KV_DSA), gk_d)
    dv = heads(dv, KV_DSA)
    iq = iq.reshape(B, T, H_IDX, D_IDX)
    return fq, fk, fv, logf, fz, dq, dk, dv, dz, iq, ik, iw


def fox_scores(q, cq, qpos, k, ck, kpos):
    B, Tq = q.shape[:2]
    Tk = k.shape[1]
    qg = q.reshape(B, Tq, KV_FOX, G_FOX, HEAD_DIM)
    s = jnp.einsum('bqhgd,bshd->bhgqs', qg, k).astype(jnp.float32) * HEAD_DIM ** -0.5
    cq_ = cq.reshape(B, Tq, KV_FOX, G_FOX).transpose(0, 2, 3, 1)[..., :, None]
    ck_ = ck.astype(jnp.float32).reshape(B, Tk, KV_FOX, G_FOX).transpose(0, 2, 3, 1)[..., None, :]
    s = s + (cq_ - ck_)
    return jnp.where(kpos[None, :] <= qpos[:, None], s, -jnp.inf)


def fox_values(p, v):
    o = jnp.einsum('bhgqs,bshd->bqhgd', p.astype(v.dtype), v)
    return o.reshape(o.shape[0], o.shape[1], W_FOX)


def index_scores(iq, iw, ik, qpos, kpos):
    r = jax.nn.relu(jnp.einsum('bqhd,bsd->bqhs', iq, ik).astype(jnp.float32) * D_IDX ** -0.5)
    sc = jnp.einsum('bqh,bqhs->bqs', iw.astype(jnp.float32) * H_IDX ** -0.5, r)
    return jnp.where(kpos[None, None, :] <= qpos[None, :, None], sc, -jnp.inf)


take_rows = jax.vmap(lambda a, i: a[i])


def dsa_attend(q, qpos, k_sel, v_sel, sel_pos, rel_bias):
    B, Tq, K = sel_pos.shape
    qg = q.reshape(B, Tq, KV_DSA, G_DSA, HEAD_DIM)
    s = jnp.einsum('bqhgd,bqnhd->bqhgn', qg, k_sel).astype(jnp.float32) * HEAD_DIM ** -0.5
    dist = qpos[None, :, None] - sel_pos
    bias = rel_bias[t5_bucket(dist)].astype(jnp.float32)
    bias = bias.reshape(B, Tq, K, KV_DSA, G_DSA).transpose(0, 1, 3, 4, 2)
    valid = (dist >= 0)[:, :, None, None, :]
    p = jax.nn.softmax(jnp.where(valid, s + bias, -jnp.inf), axis=-1)
    o = jnp.einsum('bqhgn,bqnhd->bqhgd', p.astype(v_sel.dtype), v_sel)
    return o.reshape(B, Tq, W_DSA)


def merge(fo, fz, do, dz, w_out):
    g = jnp.concatenate([fo * jax.nn.silu(fz), do * jax.nn.silu(dz)], axis=-1)
    return g @ w_out


def add_per_layer_embedding(h, p, w_ple, g_ple, w_pg):
    return h + jax.nn.sigmoid(h @ w_pg) * rmsnorm(p @ w_ple, g_ple)


def mix_prompt(xn, proj, rel_bias, w_out):
    B, T, _ = xn.shape
    fq, fk, fv, logf, fz, dq, dk, dv, dz, iq, ik, iw = project(xn, *proj)
    pos = jnp.arange(T)
    cf = jnp.cumsum(logf, axis=1)
    nb = T // Q_BLOCK
    topk = min(TOPK_MAX, T // 4)

    def to_blocks(a):
        return a.reshape((B, nb, Q_BLOCK) + a.shape[2:]).swapaxes(0, 1)

    def block(args):
        fq_b, cf_b, dq_b, iq_b, iw_b, qpos = args
        p = jax.nn.softmax(fox_scores(fq_b, cf_b, qpos, fk, cf, pos), axis=-1)
        fo = fox_values(p, fv)
        idx = lax.top_k(index_scores(iq_b, iw_b, ik, qpos, pos), topk)[1]
        do = dsa_attend(dq_b, qpos, take_rows(dk, idx), take_rows(dv, idx), idx, rel_bias)
        return fo, do

    fo, do = lax.map(block, (to_blocks(fq), to_blocks(cf), to_blocks(dq), to_blocks(iq),
                             to_blocks(iw), pos.reshape(nb, Q_BLOCK)))
    fo = fo.swapaxes(0, 1).reshape(B, T, W_FOX)
    do = do.swapaxes(0, 1).reshape(B, T, W_DSA)
    return merge(fo, fz, do, dz, w_out), (fk, fv, logf, dk, dv, ik)


def mix_sample(xn, layer, cache_fox_k, cache_fox_v, cache_fox_logf, cache_dsa_k, cache_dsa_v,
               cache_idx_k, page_table, proj, rel_bias, w_out):
    B, T, _ = xn.shape
    n_past = page_table.shape[1] * PAGE_SIZE
    fq, fk, fv, logf, fz, dq, dk, dv, dz, iq, ik, iw = project(xn, *proj)
    qpos = n_past + jnp.arange(T)
    past_pos = jnp.arange(n_past)

    def paged(pool):
        g = pool[layer, page_table]
        return g.reshape((B, n_past) + pool.shape[3:])

    def gather_rows(pool, new, idx):
        pi = jnp.clip(idx, 0, n_past - 1)
        phys = jnp.take_along_axis(page_table, (pi // PAGE_SIZE).reshape(B, -1), axis=1).reshape(idx.shape)
        rows_past = pool[layer, phys, pi % PAGE_SIZE]
        rows_new = take_rows(new, jnp.clip(idx - n_past, 0, T - 1))
        is_past = (idx < n_past).reshape(idx.shape + (1,) * (rows_past.ndim - idx.ndim))
        return jnp.where(is_past, rows_past, rows_new)

    c_past = jnp.cumsum(paged(cache_fox_logf).astype(jnp.float32), axis=1)
    c_new = c_past[:, -1:] + jnp.cumsum(logf, axis=1)
    fv_past = paged(cache_fox_v)
    s = jnp.concatenate([fox_scores(fq, c_new, qpos, paged(cache_fox_k), c_past, past_pos),
                         fox_scores(fq, c_new, qpos, fk, c_new, qpos)], axis=-1)
    p = jax.nn.softmax(s, axis=-1)
    fo = fox_values(p[..., :n_past], fv_past) + fox_values(p[..., n_past:], fv)
    sc = jnp.concatenate([index_scores(iq, iw, paged(cache_idx_k), qpos, past_pos),
                          index_scores(iq, iw, ik, qpos, qpos)], axis=-1)
    idx = lax.top_k(sc, min(TOPK_MAX, (n_past + T) // 4))[1]
    do = dsa_attend(dq, qpos, gather_rows(cache_dsa_k, dk, idx), gather_rows(cache_dsa_v, dv, idx), idx, rel_bias)
    return merge(fo, fz, do, dz, w_out), (fk, fv, logf, dk, dv, ik)


def setup_inputs(seed: int = 0) -> dict:
    key = jax.random.key(seed)
    ks = jax.random.split(key, 24)
    f32 = jnp.float32
    n_pages = PAST_LEN // PAGE_SIZE
    n_used = DEC_BATCH * n_pages
    n_pool = n_used + n_used // 4
    nrm = lambda k, shape: jax.random.normal(k, shape, f32)
    gain = lambda k, shape: 1.0 + 0.02 * nrm(k, shape)
    page_table = jax.random.permutation(ks[0], n_pool)[:n_used].reshape(DEC_BATCH, n_pages).astype(jnp.int32)
    return {
        'x_prompt': nrm(ks[1], (BATCH, SEQ, D_MODEL)),
        'x_sample': nrm(ks[2], (DEC_BATCH, DEC_SEQ, D_MODEL)),
        'cache_fox_k': nrm(ks[3], (DEPTH, n_pool, PAGE_SIZE, KV_FOX, HEAD_DIM)),
        'cache_fox_v': nrm(ks[4], (DEPTH, n_pool, PAGE_SIZE, KV_FOX, HEAD_DIM)),
        'cache_fox_logf': jax.nn.log_sigmoid(FORGET_BIAS + nrm(ks[5], (DEPTH, n_pool, PAGE_SIZE, H_FOX))),
        'cache_dsa_k': nrm(ks[6], (DEPTH, n_pool, PAGE_SIZE, KV_DSA, HEAD_DIM)),
        'cache_dsa_v': nrm(ks[7], (DEPTH, n_pool, PAGE_SIZE, KV_DSA, HEAD_DIM)),
        'cache_idx_k': nrm(ks[8], (DEPTH, n_pool, PAGE_SIZE, D_IDX)),
        'page_table': page_table,
        'p_prompt': nrm(ks[9], (DEPTH, BATCH, SEQ, PLE_DIM)),
        'p_sample': nrm(ks[10], (DEPTH, DEC_BATCH, DEC_SEQ, PLE_DIM)),
        'rel_bias': 0.2 * nrm(ks[11], (N_BUCKETS, H_DSA)),
        'norm_in': gain(ks[12], (DEPTH, D_MODEL)),
        'w_in': nrm(ks[13], (DEPTH, D_MODEL, N_IN)) * D_MODEL ** -0.5,
        'b_f': FORGET_BIAS + 0.1 * nrm(ks[14], (DEPTH, H_FOX)),
        'q_norm_fox': gain(ks[15], (DEPTH, HEAD_DIM)),
        'k_norm_fox': gain(ks[16], (DEPTH, HEAD_DIM)),
        'q_norm_dsa': gain(ks[17], (DEPTH, HEAD_DIM)),
        'k_norm_dsa': gain(ks[18], (DEPTH, HEAD_DIM)),
        'w_out': nrm(ks[19], (DEPTH, MIX_WIDTH, D_MODEL)) * MIX_WIDTH ** -0.5,
        'w_ple': nrm(ks[20], (DEPTH, PLE_DIM, D_MODEL)) * PLE_DIM ** -0.5,
        'ple_norm': gain(ks[21], (DEPTH, D_MODEL)),
        'w_pg': nrm(ks[22], (DEPTH, D_MODEL, D_MODEL)) * D_MODEL ** -0.5,
    }


def reference(x_prompt, x_sample, cache_fox_k, cache_fox_v, cache_fox_logf, cache_dsa_k, cache_dsa_v,
              cache_idx_k, page_table, p_prompt, p_sample, rel_bias, norm_in, w_in, b_f,
              q_norm_fox, k_norm_fox, q_norm_dsa, k_norm_dsa, w_out, w_ple, ple_norm, w_pg):
    hp, hs = x_prompt, x_sample
    rows_p, rows_s = [], []
    for i in range(DEPTH):
        proj = (w_in[i], b_f[i], q_norm_fox[i], k_norm_fox[i], q_norm_dsa[i], k_norm_dsa[i])
        mp, new_p = mix_prompt(rmsnorm(hp, norm_in[i]), proj, rel_bias, w_out[i])
        hp = add_per_layer_embedding(hp + mp, p_prompt[i], w_ple[i], ple_norm[i], w_pg[i])
        ms, new_s = mix_sample(rmsnorm(hs, norm_in[i]), i, cache_fox_k, cache_fox_v, cache_fox_logf,
                               cache_dsa_k, cache_dsa_v, cache_idx_k, page_table, proj, rel_bias, w_out[i])
        hs = add_per_layer_embedding(hs + ms, p_sample[i], w_ple[i], ple_norm[i], w_pg[i])
        rows_p.append(new_p)
        rows_s.append(new_s)
    stack = lambda rows, j: jnp.stack([r[j] for r in rows])
    return (hp, hs,
            stack(rows_p, 0), stack(rows_p, 1), stack(rows_p, 2), stack(rows_p, 3), stack(rows_p, 4), stack(rows_p, 5),
            stack(rows_s, 0), stack(rows_s, 1), stack(rows_s, 2), stack(rows_s, 3), stack(rows_s, 4), stack(rows_s, 5))
```

```python
import functools
import math

import jax
import jax.numpy as jnp
from jax import lax
from jax.experimental import pallas as pl
from jax.experimental.pallas import tpu as pltpu

F32 = jnp.float32
BF16 = jnp.bfloat16
I32 = jnp.int32

D_MODEL = 2048
HEAD_DIM = 128
H_FOX = 8
KV_FOX = 4
H_DSA = 8
KV_DSA = 4
H_IDX = 16
D_IDX = 64
TOPK_MAX = 256
N_BUCKETS = 32
MAX_DISTANCE = 128
PLE_DIM = 256
PAGE_SIZE = 128
EPS = 1e-6
W_HALF = H_FOX * HEAD_DIM
W_KV = KV_FOX * HEAD_DIM
SPLIT_SIZES = (W_HALF, W_KV, W_KV, H_FOX, W_HALF, W_HALF, W_KV, W_KV, W_HALF, H_IDX * D_IDX, D_IDX, H_IDX)

LANES = 128
INT_MIN = -(2 ** 31)
NEG_INF = float("-inf")
VMEM_LIMIT = 56 * 1024 * 1024

PROJ_TN = 512
SEG = dict(fq=(0, 2), fk=(2, 1), fv=(3, 1), fz=(4, 2), dq=(6, 2), dk=(8, 1), dv=(9, 1), dz=(10, 2), iq=(12, 2))
N_MAIN_BLOCKS = 14
SM_IK, SM_IW, SM_FF = 0, D_IDX, D_IDX + H_IDX


def _cparams(sem):
    return pltpu.CompilerParams(dimension_semantics=sem, vmem_limit_bytes=VMEM_LIMIT)


def _dot_nt(a, b):
    return lax.dot_general(a, b, (((1,), (1,)), ((), ())), preferred_element_type=F32)


def _dot(a, b):
    return jnp.dot(a, b, preferred_element_type=F32)


def _log_sigmoid(x):
    return -(jnp.maximum(-x, 0.0) + jnp.log(1.0 + jnp.exp(-jnp.abs(x))))


def _sigmoid(x):
    return 1.0 / (1.0 + jnp.exp(-x))


def _sort_key(x):
    b = pltpu.bitcast(x, I32)
    return b ^ ((b >> 31) & jnp.int32(0x7FFFFFFF))


def _cumsum_lanes(x):
    n = x.shape[-1]
    lane = lax.broadcasted_iota(I32, x.shape, x.ndim - 1)
    k = 1
    while k < n:
        x = x + jnp.where(lane >= k, pltpu.roll(x, k, axis=x.ndim - 1), 0.0)
        k *= 2
    return x


def _proj_kernel(x_ref, nin_ref, w_ref, ws_ref, gqf_ref, gkf_ref, gqd_ref, gkd_ref, bf_ref,
                 fq_o, fk_o, fk16_o, fv_o, fv16_o, fg_o, dq_o, dk_o, dk16_o, dv_o, dv16_o, dg_o,
                 iq_o, small_o, logft_o, xn_s):
    j = pl.program_id(1)

    @pl.when(j == 0)
    def _():
        x = x_ref[...]
        ms = jnp.mean(x * x, axis=-1, keepdims=True)
        xn_s[...] = (x * lax.rsqrt(ms + EPS) * nin_ref[...]).astype(BF16)

    def main():
        return _dot(xn_s[...], w_ref[...])

    def head_norm(y, g_ref, scale):
        outs = []
        for c in range(PROJ_TN // HEAD_DIM):
            yh = y[:, c * HEAD_DIM:(c + 1) * HEAD_DIM]
            n = yh * lax.rsqrt(jnp.mean(yh * yh, axis=-1, keepdims=True) + EPS) * g_ref[...]
            outs.append(n * scale if scale != 1.0 else n)
        return jnp.concatenate(outs, axis=1)

    def in_seg(name):
        a, n = SEG[name]
        return jnp.logical_and(j >= a, j < a + n)

    @pl.when(in_seg("fq"))
    def _():
        fq_o[...] = head_norm(main(), gqf_ref, HEAD_DIM ** -0.5).astype(BF16)

    @pl.when(in_seg("fk"))
    def _():
        n = head_norm(main(), gkf_ref, 1.0)
        fk_o[...] = n
        fk16_o[...] = n.astype(BF16)

    @pl.when(in_seg("fv"))
    def _():
        y = main()
        fv_o[...] = y
        fv16_o[...] = y.astype(BF16)

    @pl.when(in_seg("fz"))
    def _():
        y = main()
        fg_o[...] = y * _sigmoid(y)

    @pl.when(in_seg("dq"))
    def _():
        dq_o[...] = head_norm(main(), gqd_ref, HEAD_DIM ** -0.5).astype(BF16)

    @pl.when(in_seg("dk"))
    def _():
        n = head_norm(main(), gkd_ref, 1.0)
        dk_o[...] = n
        dk16_o[...] = n.astype(BF16)

    @pl.when(in_seg("dv"))
    def _():
        y = main()
        dv_o[...] = y
        dv16_o[...] = y.astype(BF16)

    @pl.when(in_seg("dz"))
    def _():
        y = main()
        dg_o[...] = y * _sigmoid(y)

    @pl.when(in_seg("iq"))
    def _():
        iq_o[...] = (main() * (D_IDX ** -0.5)).astype(BF16)

    @pl.when(j == N_MAIN_BLOCKS)
    def _():
        ys = _dot(xn_s[...], ws_ref[...])
        small_o[...] = ys
        ff_t = ys.T[SM_FF:SM_FF + H_FOX, :]
        logft_o[...] = _log_sigmoid(ff_t + bf_ref[...])


def _proj(x2, nin, w_main, w_small, gqf, gkf, gqd, gkd, bf, tm):
    rows = x2.shape[0]
    assert rows % tm == 0
    grid = (rows // tm, N_MAIN_BLOCKS + 1)

    def seg_spec(name, width):
        a, n = SEG[name]
        return pl.BlockSpec((tm, width), lambda i, j: (i, jnp.clip(j - a, 0, n - 1)))

    row_vec = lambda n: pl.BlockSpec((1, n), lambda i, j: (0, 0))
    in_specs = [
        pl.BlockSpec((tm, D_MODEL), lambda i, j: (i, 0)),
        row_vec(D_MODEL),
        pl.BlockSpec((D_MODEL, PROJ_TN), lambda i, j: (0, jnp.minimum(j, N_MAIN_BLOCKS - 1))),
        pl.BlockSpec((D_MODEL, LANES), lambda i, j: (0, 0)),
        row_vec(HEAD_DIM), row_vec(HEAD_DIM), row_vec(HEAD_DIM), row_vec(HEAD_DIM),
        pl.BlockSpec((H_FOX, 1), lambda i, j: (0, 0)),
    ]
    sds = jax.ShapeDtypeStruct
    out_shape = [
        sds((rows, W_HALF), BF16),
        sds((rows, W_KV), F32), sds((rows, W_KV), BF16),
        sds((rows, W_KV), F32), sds((rows, W_KV), BF16),
        sds((rows, W_HALF), F32),
        sds((rows, W_HALF), BF16),
        sds((rows, W_KV), F32), sds((rows, W_KV), BF16),
        sds((rows, W_KV), F32), sds((rows, W_KV), BF16),
        sds((rows, W_HALF), F32),
        sds((rows, W_HALF), BF16),
        sds((rows, LANES), F32),
        sds((H_FOX, rows), F32),
    ]
    out_specs = [
        seg_spec("fq", PROJ_TN),
        seg_spec("fk", PROJ_TN), seg_spec("fk", PROJ_TN),
        seg_spec("fv", PROJ_TN), seg_spec("fv", PROJ_TN),
        seg_spec("fz", PROJ_TN),
        seg_spec("dq", PROJ_TN),
        seg_spec("dk", PROJ_TN), seg_spec("dk", PROJ_TN),
        seg_spec("dv", PROJ_TN), seg_spec("dv", PROJ_TN),
        seg_spec("dz", PROJ_TN),
        seg_spec("iq", PROJ_TN),
        pl.BlockSpec((tm, LANES), lambda i, j: (i, 0)),
        pl.BlockSpec((H_FOX, tm), lambda i, j: (0, i)),
    ]
    return pl.pallas_call(
        _proj_kernel, out_shape=out_shape, grid=grid, in_specs=in_specs, out_specs=out_specs,
        scratch_shapes=[pltpu.VMEM((tm, D_MODEL), BF16)],
        compiler_params=_cparams(("arbitrary", "arbitrary")), name="proj",
    )(x2, nin, w_main, w_small, gqf, gkf, gqd, gkd, bf)


def _cumsum_kernel(x_ref, o_ref):
    o_ref[...] = _cumsum_lanes(x_ref[...])


def _cumsum_prompt(logft, t):
    rows = logft.shape[1]
    return pl.pallas_call(
        _cumsum_kernel, out_shape=jax.ShapeDtypeStruct(logft.shape, F32), grid=(rows // t,),
        in_specs=[pl.BlockSpec((H_FOX, t), lambda b: (0, b))],
        out_specs=pl.BlockSpec((H_FOX, t), lambda b: (0, b)),
        compiler_params=_cparams(("arbitrary",)), name="cumsum",
    )(logft)


def _fox_kernel(q_ref, k_ref, v_ref, c_ref, g_ref, o_ref, *, tq):
    qi = pl.program_id(2)
    q2 = jnp.concatenate([q_ref[:, :HEAD_DIM], q_ref[:, HEAD_DIM:]], axis=0)
    row = lax.broadcasted_iota(I32, (tq, tq), 0)
    col = lax.broadcasted_iota(I32, (tq, tq), 1)

    def chunk(j, carry, diag):
        ms, ls, accs = carry
        off = pl.multiple_of(j * tq, tq)
        s = _dot_nt(q2, k_ref[pl.ds(off, tq), :])
        ps, alphas, ms_n, ls_n = [], [], [], []
        for g in range(2):
            sg = s[g * tq:(g + 1) * tq] - c_ref[0, g, pl.ds(j, 1), :]
            if diag:
                sg = jnp.where(col <= row, sg, NEG_INF)
            m_new = jnp.maximum(ms[g], jnp.max(sg, axis=-1, keepdims=True))
            alpha = jnp.exp(ms[g] - m_new)
            p = jnp.exp(sg - m_new)
            ls_n.append(alpha * ls[g] + jnp.sum(p, axis=-1, keepdims=True))
            ms_n.append(m_new)
            alphas.append(alpha)
            ps.append(p.astype(BF16))
        pv = _dot(jnp.concatenate(ps, axis=0), v_ref[pl.ds(off, tq), :])
        accs_n = [alphas[g] * accs[g] + pv[g * tq:(g + 1) * tq] for g in range(2)]
        return tuple(ms_n), tuple(ls_n), tuple(accs_n)

    init = (tuple(jnp.full((tq, 1), NEG_INF, F32) for _ in range(2)),
            tuple(jnp.zeros((tq, 1), F32) for _ in range(2)),
            tuple(jnp.zeros((tq, HEAD_DIM), F32) for _ in range(2)))
    carry = lax.fori_loop(0, qi, lambda j, c: chunk(j, c, False), init)
    ms, ls, accs = chunk(qi, carry, True)
    for g in range(2):
        o = accs[g] / ls[g]
        o_ref[:, g * HEAD_DIM:(g + 1) * HEAD_DIM] = (o * g_ref[:, g * HEAD_DIM:(g + 1) * HEAD_DIM]).astype(BF16)


def _fox_prompt(fq16, fk16, fv16, c4, fg, b, t, tq):
    nq = t // tq
    gw = 2 * HEAD_DIM
    return pl.pallas_call(
        functools.partial(_fox_kernel, tq=tq),
        out_shape=jax.ShapeDtypeStruct((b * t, W_HALF), BF16), grid=(b, KV_FOX, nq),
        in_specs=[
            pl.BlockSpec((tq, gw), lambda bi, kv, qi: (bi * nq + qi, kv)),
            pl.BlockSpec((t, HEAD_DIM), lambda bi, kv, qi: (bi, kv)),
            pl.BlockSpec((t, HEAD_DIM), lambda bi, kv, qi: (bi, kv)),
            pl.BlockSpec((1, 2, nq, tq), lambda bi, kv, qi: (kv, 0, bi, 0)),
            pl.BlockSpec((tq, gw), lambda bi, kv, qi: (bi * nq + qi, kv)),
        ],
        out_specs=pl.BlockSpec((tq, gw), lambda bi, kv, qi: (bi * nq + qi, kv)),
        compiler_params=_cparams(("arbitrary", "arbitrary", "arbitrary")), name="fox_prompt",
    )(fq16, fk16, fv16, c4, fg)


def _t5_bucket(d):
    max_exact = N_BUCKETS // 2
    d = jnp.maximum(d, 0)
    lr = jnp.log(jnp.maximum(d, 1).astype(F32) / max_exact) / math.log(MAX_DISTANCE / max_exact)
    large = jnp.minimum(max_exact + (lr * (N_BUCKETS - max_exact)).astype(I32), N_BUCKETS - 1)
    return jnp.where(d < max_exact, d, large)


def _bias_from_dist(dist, rb_ref, h):
    bucket = _t5_bucket(dist)
    out = jnp.zeros(dist.shape, F32)
    for bkt in range(N_BUCKETS):
        out = jnp.where(bucket == bkt, rb_ref[bkt, h], out)
    return out


def _bias_tab_kernel(rb_ref, ptab_ref, stab_ref, *, n_past):
    r = lax.broadcasted_iota(I32, (LANES, 2 * LANES), 0)
    c = lax.broadcasted_iota(I32, (LANES, 2 * LANES), 1)
    for h in range(H_DSA):
        ptab_ref[h] = _bias_from_dist(r - c + LANES, rb_ref, h)
    rows = 4 * H_DSA
    rr = lax.broadcasted_iota(I32, (rows, LANES), 0)
    pos = lax.broadcasted_iota(I32, (rows, LANES), 1)
    t = rr // H_DSA
    hh = rr % H_DSA
    dists = (jnp.full((rows, LANES), 2 * MAX_DISTANCE, I32), PAGE_SIZE + t - pos, t - pos)
    for k, dist in enumerate(dists):
        acc = jnp.zeros((rows, LANES), F32)
        for h in range(H_DSA):
            acc = jnp.where(hh == h, _bias_from_dist(dist, rb_ref, h), acc)
        stab_ref[k] = acc


def _bias_tables(rel_bias, n_past):
    return pl.pallas_call(
        functools.partial(_bias_tab_kernel, n_past=n_past),
        out_shape=[jax.ShapeDtypeStruct((H_DSA, LANES, 2 * LANES), F32),
                   jax.ShapeDtypeStruct((3, 4 * H_DSA, LANES), F32)],
        in_specs=[pl.BlockSpec(memory_space=pltpu.SMEM)],
        name="bias_tables",
    )(rel_bias)


def _kth_largest_key(count_ge, shape, k):
    def step(i, cur):
        cand = cur + lax.shift_left(jnp.int32(1), jnp.int32(31) - i)
        return jnp.where(count_ge(cand) >= k, cand, cur)
    return lax.fori_loop(0, 32, step, jnp.full(shape, INT_MIN, I32))


def _dsa_kernel(rb_ref, iq_ref, ik2_ref, sm_ref, q_ref, k_ref, v_ref, g_ref, tab_ref, o_ref,
                key_s, m_s, l_s, acc_s, *, tq, topk):
    qi = pl.program_id(1)
    row = lax.broadcasted_iota(I32, (tq, LANES), 0)
    col = lax.broadcasted_iota(I32, (tq, LANES), 1)
    lane2 = lax.broadcasted_iota(I32, (LANES, LANES), 1)

    w = sm_ref[:, SM_IW:SM_IW + H_IDX] * (H_IDX ** -0.5)
    wb = [jnp.broadcast_to(w[:, h:h + 1], (tq, LANES)) for h in range(H_IDX)]

    def score_chunk(j, _):
        off = pl.multiple_of(j * LANES, LANES)
        ik2 = ik2_ref[pl.ds(off, LANES), :]
        rhs = jnp.concatenate([jnp.where(lane2 < D_IDX, ik2, 0), jnp.where(lane2 >= D_IDX, ik2, 0)], axis=0)
        sc = jnp.zeros((tq, LANES), F32)
        for p in range(H_IDX // 2):
            s2 = _dot_nt(iq_ref[:, p * LANES:(p + 1) * LANES], rhs)
            sc = sc + wb[2 * p] * jnp.maximum(s2[:, :LANES], 0.0) + wb[2 * p + 1] * jnp.maximum(s2[:, LANES:], 0.0)
        sc = jnp.where(off + col <= qi * tq + row, sc, NEG_INF)
        key_s[j] = _sort_key(sc)
        return 0

    lax.fori_loop(0, qi + 1, score_chunk, 0)

    def count_ge(cand):
        def body(j, cnt):
            return cnt + jnp.where(key_s[j] >= cand, 1, 0)
        cnt = lax.fori_loop(0, qi + 1, body, jnp.zeros((tq, LANES), I32))
        return jnp.sum(cnt, axis=-1, keepdims=True)

    thr = _kth_largest_key(count_ge, (tq, 1), topk)

    m_s[...] = jnp.full(m_s.shape, NEG_INF, F32)
    l_s[...] = jnp.zeros(l_s.shape, F32)
    acc_s[...] = jnp.zeros(acc_s.shape, F32)

    def attend(j, mode):
        off = pl.multiple_of(j * LANES, LANES)
        sel = key_s[j] >= thr
        if mode == 2:
            sel = jnp.logical_and(sel, col <= row)
        for kv in range(KV_DSA):
            q2 = jnp.concatenate([q_ref[:, (2 * kv + g) * HEAD_DIM:(2 * kv + g + 1) * HEAD_DIM] for g in range(2)], axis=0)
            s = _dot_nt(q2, k_ref[pl.ds(off, LANES), kv * HEAD_DIM:(kv + 1) * HEAD_DIM])
            ps = []
            for g in range(2):
                h = 2 * kv + g
                sg = s[g * tq:(g + 1) * tq]
                if mode == 0:
                    sg = sg + rb_ref[N_BUCKETS - 1, h]
                elif mode == 1:
                    sg = sg + tab_ref[h, :, :LANES]
                else:
                    sg = sg + tab_ref[h, :, LANES:]
                sg = jnp.where(sel, sg, NEG_INF)
                m_old = m_s[h]
                m_new = jnp.maximum(m_old, jnp.max(sg, axis=-1, keepdims=True))
                m_safe = jnp.where(m_new == NEG_INF, 0.0, m_new)
                alpha = jnp.exp(m_old - m_safe)
                p = jnp.exp(sg - m_safe)
                l_s[h] = alpha * l_s[h] + jnp.sum(p, axis=-1, keepdims=True)
                m_s[h] = m_new
                acc_s[h] = alpha * acc_s[h]
                ps.append(p.astype(BF16))
            pv = _dot(jnp.concatenate(ps, axis=0), v_ref[pl.ds(off, LANES), kv * HEAD_DIM:(kv + 1) * HEAD_DIM])
            for g in range(2):
                acc_s[2 * kv + g] += pv[g * tq:(g + 1) * tq]

    def far(j, _):
        attend(j, 0)
        return 0

    lax.fori_loop(0, jnp.maximum(qi - 1, 0), far, 0)

    @pl.when(qi >= 1)
    def _():
        attend(qi - 1, 1)

    attend(qi, 2)

    for h in range(H_DSA):
        o = acc_s[h] / l_s[h]
        o_ref[:, h * HEAD_DIM:(h + 1) * HEAD_DIM] = (o * g_ref[:, h * HEAD_DIM:(h + 1) * HEAD_DIM]).astype(BF16)


def _dsa_prompt(rel_bias, iq16, ik2, small, dq16, dk16, dv16, dg, ptab, b, t, tq, topk):
    nq = t // tq
    rows = b * t
    qmap = lambda bi, qi: (bi * nq + qi, 0)
    bmap = lambda bi, qi: (bi, 0)
    return pl.pallas_call(
        functools.partial(_dsa_kernel, tq=tq, topk=topk),
        out_shape=jax.ShapeDtypeStruct((rows, W_HALF), BF16), grid=(b, nq),
        in_specs=[
            pl.BlockSpec(memory_space=pltpu.SMEM),
            pl.BlockSpec((tq, H_IDX * D_IDX), qmap),
            pl.BlockSpec((t, LANES), bmap),
            pl.BlockSpec((tq, LANES), qmap),
            pl.BlockSpec((tq, W_HALF), qmap),
            pl.BlockSpec((t, W_KV), bmap),
            pl.BlockSpec((t, W_KV), bmap),
            pl.BlockSpec((tq, W_HALF), qmap),
            pl.BlockSpec((H_DSA, LANES, 2 * LANES), lambda bi, qi: (0, 0, 0)),
        ],
        out_specs=pl.BlockSpec((tq, W_HALF), qmap),
        scratch_shapes=[pltpu.VMEM((nq, tq, LANES), I32),
                        pltpu.VMEM((H_DSA, tq, 1), F32), pltpu.VMEM((H_DSA, tq, 1), F32),
                        pltpu.VMEM((H_DSA, tq, HEAD_DIM), F32)],
        compiler_params=_cparams(("arbitrary", "arbitrary")), name="dsa_prompt",
    )(rel_bias, iq16, ik2, small, dq16, dk16, dv16, dg, ptab)


def _page_specs(shape_tail, pps, new_step_tail):
    nd = len(shape_tail)
    return [pl.BlockSpec((1,) + shape_tail, functools.partial(
        lambda s, p, pt, i: (pt[s, p * pps + i],) + (0,) * nd, i=i)) for i in range(pps)]


def _smp_score_kernel(pt_ref, iq_ref, w_ref, *refs, pps, n_tok):
    ik_refs, iknew_ref, o_ref = refs[:pps], refs[pps], refs[pps + 1]
    p = pl.program_id(1)
    iq = iq_ref[0]
    wcol = w_ref[0]

    def page_scores(ik):
        r = jnp.maximum(_dot_nt(iq, ik.astype(BF16)), 0.0) * wcol
        return jnp.sum(r.reshape(n_tok, H_IDX, LANES), axis=1)

    for i in range(pps):
        o_ref[0, :, i * LANES:(i + 1) * LANES] = _sort_key(page_scores(ik_refs[i][0]))

    @pl.when(p < pl.num_programs(1) - 1)
    def _():
        o_ref[0, :, pps * LANES:] = jnp.full((n_tok, LANES), INT_MIN, I32)

    @pl.when(p == pl.num_programs(1) - 1)
    def _():
        sc = page_scores(iknew_ref[0])
        t = lax.broadcasted_iota(I32, (n_tok, LANES), 0)
        pos = lax.broadcasted_iota(I32, (n_tok, LANES), 1)
        o_ref[0, :, pps * LANES:] = _sort_key(jnp.where(pos <= t, sc, NEG_INF))


def _smp_scores(page_table, iq_s, w_s, ik_pool, ik_new, pps):
    s, n_pages = page_table.shape
    n_tok = iq_s.shape[1] // H_IDX
    steps = n_pages // pps
    width = (pps + 1) * LANES
    grid_spec = pltpu.PrefetchScalarGridSpec(
        num_scalar_prefetch=1, grid=(s, steps),
        in_specs=[pl.BlockSpec((1, n_tok * H_IDX, D_IDX), lambda si, p, pt: (si, 0, 0)),
                  pl.BlockSpec((1, n_tok * H_IDX, 1), lambda si, p, pt: (si, 0, 0))]
                 + _page_specs((PAGE_SIZE, D_IDX), pps, None)
                 + [pl.BlockSpec((1, PAGE_SIZE, D_IDX), lambda si, p, pt: (si, 0, 0))],
        out_specs=pl.BlockSpec((1, n_tok, width), lambda si, p, pt: (si, 0, p)),
    )
    return pl.pallas_call(
        functools.partial(_smp_score_kernel, pps=pps, n_tok=n_tok),
        out_shape=jax.ShapeDtypeStruct((s, n_tok, steps * width), I32), grid_spec=grid_spec,
        compiler_params=_cparams(("arbitrary", "arbitrary")), name="smp_scores",
    )(page_table, iq_s, w_s, *([ik_pool] * pps), ik_new)


def _smp_thresh_kernel(key_ref, o_ref, *, topk):
    keys = key_ref[0]

    def count_ge(cand):
        return jnp.sum(jnp.where(keys >= cand, 1, 0), axis=-1, keepdims=True)

    thr = _kth_largest_key(count_ge, (keys.shape[0], 1), topk)
    o_ref[0] = jnp.broadcast_to(thr, (keys.shape[0], LANES))


def _smp_thresh(keys, topk):
    s, n_tok, n = keys.shape
    return pl.pallas_call(
        functools.partial(_smp_thresh_kernel, topk=topk),
        out_shape=jax.ShapeDtypeStruct((s, n_tok, LANES), I32), grid=(s,),
        in_specs=[pl.BlockSpec((1, n_tok, n), lambda si: (si, 0, 0))],
        out_specs=pl.BlockSpec((1, n_tok, LANES), lambda si: (si, 0, 0)),
        compiler_params=_cparams(("arbitrary",)), name="smp_thresh",
    )(keys)


def _smp_attn_kernel(pt_ref, q_ref, g_ref, *refs, pps, n_tok, n_kv, mode):
    k_refs, v_refs = refs[:pps], refs[pps:2 * pps]
    rest = refs[2 * pps:]
    if mode == "fox":
        lf_refs, rest = rest[:pps], rest[pps:]
        knew_ref, vnew_ref, lfnew_ref, o_ref, m_s, l_s, acc_s, carry_s = rest
    else:
        knew_ref, vnew_ref, key_ref, thr_ref, tab_ref, o_ref, m_s, l_s, acc_s = rest
    p = pl.program_id(1)
    last = pl.num_programs(1) - 1
    rows = q_ref.shape[1]
    heads = rows // n_tok
    grp = heads // n_kv
    q = q_ref[0]
    rr = lax.broadcasted_iota(I32, (rows, LANES), 0)
    pos = lax.broadcasted_iota(I32, (rows, LANES), 1)
    row_kv = (rr % heads) // grp
    qm = [jnp.where(row_kv == kv, q, jnp.zeros_like(q)) for kv in range(n_kv)]

    @pl.when(p == 0)
    def _():
        m_s[...] = jnp.full(m_s.shape, NEG_INF, F32)
        l_s[...] = jnp.zeros(l_s.shape, F32)
        acc_s[...] = jnp.zeros(acc_s.shape, F32)
        if mode == "fox":
            carry_s[...] = jnp.zeros(carry_s.shape, F32)

    def attend(k_ref, v_ref, bias, sel):
        s = bias
        for kv in range(n_kv):
            k = k_ref[0, pl.ds(kv, PAGE_SIZE, stride=n_kv), :].astype(BF16)
            s = s + _dot_nt(qm[kv], k)
        if sel is not None:
            s = jnp.where(sel, s, NEG_INF)
        m_old = m_s[...]
        m_new = jnp.maximum(m_old, jnp.max(s, axis=-1, keepdims=True))
        m_safe = jnp.where(m_new == NEG_INF, 0.0, m_new)
        alpha = jnp.exp(m_old - m_safe)
        pr = jnp.exp(s - m_safe)
        l_s[...] = alpha * l_s[...] + jnp.sum(pr, axis=-1, keepdims=True)
        m_s[...] = m_new
        pv = jnp.zeros((rows, HEAD_DIM), F32)
        for kv in range(n_kv):
            v = v_ref[0, pl.ds(kv, PAGE_SIZE, stride=n_kv), :].astype(BF16)
            pv = pv + _dot(jnp.where(row_kv == kv, pr, 0.0).astype(BF16), v)
        acc_s[...] = alpha * acc_s[...] + pv

    def fox_bias(lf_ref):
        c = carry_s[...] + _cumsum_lanes(lf_ref[0])
        carry_s[...] = c[:, LANES - 1:]
        return -jnp.concatenate([c] * n_tok, axis=0)

    def expand_rows(x):
        return jnp.concatenate([jnp.broadcast_to(x[t:t + 1], (heads, x.shape[1])) for t in range(n_tok)], axis=0)

    for i in range(pps):
        if mode == "fox":
            attend(k_refs[i], v_refs[i], fox_bias(lf_refs[i]), None)
        else:
            is_last_page = jnp.logical_and(p == last, i == pps - 1) if i == pps - 1 else None
            if is_last_page is None:
                bias = tab_ref[0]
            else:
                bias = tab_ref[jnp.where(is_last_page, 1, 0)]
            sel = expand_rows(key_ref[0, :, i * LANES:(i + 1) * LANES]) >= expand_rows(thr_ref[0])
            attend(k_refs[i], v_refs[i], bias, sel)

    @pl.when(p == last)
    def _():
        causal = pos <= rr // heads
        if mode == "fox":
            attend(knew_ref, vnew_ref, fox_bias(lfnew_ref), causal)
        else:
            sel = expand_rows(key_ref[0, :, pps * LANES:]) >= expand_rows(thr_ref[0])
            attend(knew_ref, vnew_ref, tab_ref[2], jnp.logical_and(sel, causal))
        o_ref[0] = ((acc_s[...] / l_s[...]) * g_ref[0]).astype(BF16)


def _smp_attn(mode, page_table, q_s, g_s, k_pool, v_pool, k_new, v_new, extra, pps):
    s, n_pages = page_table.shape
    steps = n_pages // pps
    rows = q_s.shape[1]
    n_tok = rows // H_FOX
    seq3 = lambda a, b: pl.BlockSpec((1, a, b), lambda si, p, pt: (si, 0, 0))
    in_specs = [seq3(rows, HEAD_DIM), seq3(rows, HEAD_DIM)]
    in_specs += _page_specs((PAGE_SIZE * KV_FOX, HEAD_DIM), pps, None) * 2
    args = [q_s, g_s] + [k_pool] * pps + [v_pool] * pps
    scratch = [pltpu.VMEM((rows, 1), F32), pltpu.VMEM((rows, 1), F32), pltpu.VMEM((rows, HEAD_DIM), F32)]
    if mode == "fox":
        lf_pool, lf_new = extra
        in_specs += _page_specs((H_FOX, PAGE_SIZE), pps, None)
        in_specs += [seq3(PAGE_SIZE * KV_FOX, HEAD_DIM)] * 2 + [seq3(H_FOX, PAGE_SIZE)]
        args += [lf_pool] * pps + [k_new, v_new, lf_new]
        scratch.append(pltpu.VMEM((H_FOX, 1), F32))
    else:
        keys, thr, stab = extra
        width = (pps + 1) * LANES
        in_specs += [seq3(PAGE_SIZE * KV_FOX, HEAD_DIM)] * 2
        in_specs += [pl.BlockSpec((1, n_tok, width), lambda si, p, pt: (si, 0, p)),
                     seq3(n_tok, LANES),
                     pl.BlockSpec((3, rows, LANES), lambda si, p, pt: (0, 0, 0))]
        args += [k_new, v_new, keys, thr, stab]
    grid_spec = pltpu.PrefetchScalarGridSpec(
        num_scalar_prefetch=1, grid=(s, steps), in_specs=in_specs,
        out_specs=seq3(rows, HEAD_DIM), scratch_shapes=scratch)
    return pl.pallas_call(
        functools.partial(_smp_attn_kernel, pps=pps, n_tok=n_tok, n_kv=KV_FOX, mode=mode),
        out_shape=jax.ShapeDtypeStruct((s, rows, HEAD_DIM), BF16), grid_spec=grid_spec,
        compiler_params=_cparams(("arbitrary", "arbitrary")), name="smp_attn_" + mode,
    )(page_table, *args)


def _out_kernel(x_ref, gf_ref, gd_ref, p_ref, wo_ref, wpg_ref, wple_ref, gple_ref, o_ref):
    h = x_ref[...] + _dot(gf_ref[...], wo_ref[:W_HALF, :]) + _dot(gd_ref[...], wo_ref[W_HALF:, :])
    gate = _sigmoid(_dot(h.astype(BF16), wpg_ref[...]))
    e = _dot(p_ref[...].astype(BF16), wple_ref[...])
    e = e * lax.rsqrt(jnp.mean(e * e, axis=-1, keepdims=True) + EPS) * gple_ref[...]
    o_ref[...] = h + gate * e


def _out(x2, gf, gd, p2, wo, wpg, wple, gple, tm):
    rows = x2.shape[0]
    const = lambda shape: pl.BlockSpec(shape, lambda i: (0, 0), pipeline_mode=pl.Buffered(1))
    rmap = lambda i: (i, 0)
    return pl.pallas_call(
        _out_kernel, out_shape=jax.ShapeDtypeStruct((rows, D_MODEL), F32), grid=(rows // tm,),
        in_specs=[pl.BlockSpec((tm, D_MODEL), rmap), pl.BlockSpec((tm, W_HALF), rmap), pl.BlockSpec((tm, W_HALF), rmap),
                  pl.BlockSpec((tm, PLE_DIM), rmap),
                  const((D_MODEL, D_MODEL)), const((D_MODEL, D_MODEL)), const((PLE_DIM, D_MODEL)), const((1, D_MODEL))],
        out_specs=pl.BlockSpec((tm, D_MODEL), rmap),
        compiler_params=_cparams(("arbitrary",)), name="out",
    )(x2, gf, gd, p2, wo, wpg, wple, gple)


def _prep_w_in(w):
    points = [sum(SPLIT_SIZES[:i + 1]) for i in range(len(SPLIT_SIZES) - 1)]
    fq, fk, fv, ff, fz, dq, dk, dv, dz, iq, ik, iw = jnp.split(w, points, axis=1)
    main = jnp.concatenate([fq, fk, fv, fz, dq, dk, dv, dz, iq], axis=1).astype(BF16)
    pad = jnp.zeros((w.shape[0], LANES - D_IDX - H_IDX - H_FOX), w.dtype)
    small = jnp.concatenate([ik, iw, ff, pad], axis=1).astype(BF16)
    return main, small


def _pick_tile(n, pref):
    t = pref
    while n % t:
        t //= 2
    return t


def kernel(x_prompt, x_sample, cache_fox_k, cache_fox_v, cache_fox_logf, cache_dsa_k, cache_dsa_v, cache_idx_k,
           page_table, p_prompt, p_sample, rel_bias, norm_in, w_in, b_f, q_norm_fox, k_norm_fox, q_norm_dsa,
           k_norm_dsa, w_out, w_ple, ple_norm, w_pg):
    b, t, _ = x_prompt.shape
    s, n_tok, _ = x_sample.shape
    n_pages = page_table.shape[1]
    n_past = n_pages * PAGE_SIZE
    n_pool = cache_fox_k.shape[1]
    assert cache_fox_k.shape[0] == 1 and n_tok * H_FOX == 32

    w_main, w_small = _prep_w_in(w_in[0])
    nin = norm_in[0].reshape(1, D_MODEL)
    vec = lambda a: a[0].reshape(1, HEAD_DIM)
    gains = (vec(q_norm_fox), vec(k_norm_fox), vec(q_norm_dsa), vec(k_norm_dsa))
    bf = b_f[0].reshape(H_FOX, 1)
    wo = w_out[0].astype(BF16)
    wpg = w_pg[0].astype(BF16)
    wple = w_ple[0].astype(BF16)
    gple = ple_norm[0].reshape(1, D_MODEL)

    ptab, stab = _bias_tables(rel_bias, n_past)

    rows_p = b * t
    xp = x_prompt.reshape(rows_p, D_MODEL)
    (fq16, fk, fk16, fv, fv16, fg, dq16, dk, dk16, dv, dv16, dg, iq16, small, logft) = _proj(
        xp, nin, w_main, w_small, *gains, bf, _pick_tile(rows_p, 512))
    tq_f = _pick_tile(t, 256)
    ct = _cumsum_prompt(logft, t)
    c4 = ct.reshape(KV_FOX, 2, rows_p // tq_f, tq_f)
    gf = _fox_prompt(fq16, fk16, fv16, c4, fg, b, t, tq_f)
    ik16 = small[:, SM_IK:SM_IK + D_IDX].astype(BF16)
    ik2 = jnp.concatenate([ik16, ik16], axis=1)
    topk_p = min(TOPK_MAX, t // 4)
    gd = _dsa_prompt(rel_bias, iq16, ik2, small, dq16, dk16, dv16, dg, ptab, b, t, LANES, topk_p)
    y_p = _out(xp, gf, gd, p_prompt[0].reshape(rows_p, PLE_DIM), wo, wpg, wple, gple, _pick_tile(rows_p, 256))

    rows_s = s * n_tok
    xs = x_sample.reshape(rows_s, D_MODEL)
    (sfq16, sfk, _, sfv, _, sfg, sdq16, sdk, _, sdv, _, sdg, siq16, ssmall, slogft) = _proj(
        xs, nin, w_main, w_small, *gains, bf, _pick_tile(rows_s, 512))
    pps = _pick_tile(n_pages, 8)
    rows_q = n_tok * H_FOX

    def new_page(a):
        a = a.reshape(s, n_tok * KV_FOX, HEAD_DIM)
        return jnp.pad(a, ((0, 0), (0, (PAGE_SIZE - n_tok) * KV_FOX), (0, 0)))

    q_rows = lambda a: a.reshape(s, rows_q, HEAD_DIM)
    pool = lambda c: c[0].reshape(n_pool, PAGE_SIZE * KV_FOX, HEAD_DIM)
    lf_pool = jnp.transpose(cache_fox_logf[0], (0, 2, 1))
    lf_new = jnp.pad(jnp.transpose(slogft.reshape(H_FOX, s, n_tok), (1, 0, 2)), ((0, 0), (0, 0), (0, PAGE_SIZE - n_tok)))
    sgf = _smp_attn("fox", page_table, q_rows(sfq16), q_rows(sfg), pool(cache_fox_k), pool(cache_fox_v),
                    new_page(sfk), new_page(sfv), (lf_pool, lf_new), pps)

    iq_s = siq16.reshape(s, n_tok * H_IDX, D_IDX)
    w_s = (ssmall[:, SM_IW:SM_IW + H_IDX] * (H_IDX ** -0.5)).reshape(s, n_tok * H_IDX, 1)
    ik_new = jnp.pad(ssmall[:, SM_IK:SM_IK + D_IDX].reshape(s, n_tok, D_IDX), ((0, 0), (0, PAGE_SIZE - n_tok), (0, 0)))
    keys = _smp_scores(page_table, iq_s, w_s, cache_idx_k[0], ik_new, pps)
    topk_s = min(TOPK_MAX, (n_past + n_tok) // 4)
    thr = _smp_thresh(keys, topk_s)
    sgd = _smp_attn("dsa", page_table, q_rows(sdq16), q_rows(sdg), pool(cache_dsa_k), pool(cache_dsa_v),
                    new_page(sdk), new_page(sdv), (keys, thr, stab), pps)
    y_s = _out(xs, sgf.reshape(rows_s, W_HALF), sgd.reshape(rows_s, W_HALF), p_sample[0].reshape(rows_s, PLE_DIM),
               wo, wpg, wple, gple, _pick_tile(rows_s, 256))

    def kv5(a, bb, tt):
        return a.reshape(1, bb, tt, KV_FOX, HEAD_DIM)

    def outs(bb, tt, fk_, fv_, logft_, dk_, dv_, small_):
        return (kv5(fk_, bb, tt), kv5(fv_, bb, tt), logft_.T.reshape(1, bb, tt, H_FOX),
                kv5(dk_, bb, tt), kv5(dv_, bb, tt), small_[:, SM_IK:SM_IK + D_IDX].reshape(1, bb, tt, D_IDX))

    return ((y_p.reshape(b, t, D_MODEL), y_s.reshape(s, n_tok, D_MODEL))
            + outs(b, t, fk, fv, logft, dk, dv, small)
            + outs(s, n_tok, sfk, sfv, slogft, sdk, sdv, ssmall))
```

```python
import functools
import math

import jax
import jax.numpy as jnp
from jax import lax
from jax.experimental import pallas as pl
from jax.experimental.pallas import tpu as pltpu

F32 = jnp.float32
BF16 = jnp.bfloat16
I32 = jnp.int32

D_MODEL = 2048
HEAD_DIM = 128
H_FOX = 8
KV_FOX = 4
H_DSA = 8
KV_DSA = 4
H_IDX = 16
D_IDX = 64
TOPK_MAX = 256
N_BUCKETS = 32
MAX_DISTANCE = 128
PLE_DIM = 256
PAGE_SIZE = 128
EPS = 1e-6
W_HALF = H_FOX * HEAD_DIM
W_KV = KV_FOX * HEAD_DIM
SPLIT_SIZES = (W_HALF, W_KV, W_KV, H_FOX, W_HALF, W_HALF, W_KV, W_KV, W_HALF, H_IDX * D_IDX, D_IDX, H_IDX)

LANES = 128
INT_MIN = -(2 ** 31)
NEG_INF = float("-inf")
VMEM_LIMIT = 56 * 1024 * 1024

PROJ_TN = 512
SEG = dict(fq=(0, 2), fk=(2, 1), fv=(3, 1), fz=(4, 2), dq=(6, 2), dk=(8, 1), dv=(9, 1), dz=(10, 2), iq=(12, 2))
N_MAIN_BLOCKS = 14
SM_IK, SM_IW, SM_FF = 0, D_IDX, D_IDX + H_IDX


def _cparams(sem):
    return pltpu.CompilerParams(dimension_semantics=sem, vmem_limit_bytes=VMEM_LIMIT)


def _dot_nt(a, b):
    return lax.dot_general(a, b, (((1,), (1,)), ((), ())), preferred_element_type=F32)


def _dot(a, b):
    return jnp.dot(a, b, preferred_element_type=F32)


def _log_sigmoid(x):
    return -(jnp.maximum(-x, 0.0) + jnp.log(1.0 + jnp.exp(-jnp.abs(x))))


def _sigmoid(x):
    return 1.0 / (1.0 + jnp.exp(-x))


def _sort_key(x):
    b = pltpu.bitcast(x, I32)
    return b ^ ((b >> 31) & jnp.int32(0x7FFFFFFF))


def _tile_lanes(x, width):
    return x if width == LANES else jnp.concatenate([x] * (width // LANES), axis=1)


def _with_ones(v):
    return jnp.concatenate([v, jnp.ones_like(v)], axis=1)


def _cumsum_lanes(x):
    n = x.shape[-1]
    lane = lax.broadcasted_iota(I32, x.shape, x.ndim - 1)
    k = 1
    while k < n:
        x = x + jnp.where(lane >= k, pltpu.roll(x, k, axis=x.ndim - 1), 0.0)
        k *= 2
    return x


def _proj_kernel(x_ref, nin_ref, w_ref, ws_ref, gqf_ref, gkf_ref, gqd_ref, gkd_ref, bf_ref,
                 fq_o, fk_o, fk16_o, fv_o, fv16_o, fg_o, dq_o, dk_o, dk16_o, dv_o, dv16_o, dg_o,
                 iq_o, small_o, logft_o, xn_s):
    j = pl.program_id(1)

    @pl.when(j == 0)
    def _():
        x = x_ref[...]
        ms = jnp.mean(x * x, axis=-1, keepdims=True)
        xn_s[...] = (x * lax.rsqrt(ms + EPS) * nin_ref[...]).astype(BF16)

    def main():
        return _dot(xn_s[...], w_ref[...])

    def head_norm(y, g_ref, scale):
        outs = []
        for c in range(PROJ_TN // HEAD_DIM):
            yh = y[:, c * HEAD_DIM:(c + 1) * HEAD_DIM]
            n = yh * lax.rsqrt(jnp.mean(yh * yh, axis=-1, keepdims=True) + EPS) * g_ref[...]
            outs.append(n * scale if scale != 1.0 else n)
        return jnp.concatenate(outs, axis=1)

    def in_seg(name):
        a, n = SEG[name]
        return jnp.logical_and(j >= a, j < a + n)

    def store_kv(o_ref, y):
        for kv in range(KV_FOX):
            o_ref[pl.ds(kv, y.shape[0], stride=KV_FOX), :] = y[:, kv * HEAD_DIM:(kv + 1) * HEAD_DIM]

    @pl.when(in_seg("fq"))
    def _():
        fq_o[...] = head_norm(main(), gqf_ref, HEAD_DIM ** -0.5).astype(BF16)

    @pl.when(in_seg("fk"))
    def _():
        n = head_norm(main(), gkf_ref, 1.0)
        store_kv(fk_o, n)
        fk16_o[...] = n.astype(BF16)

    @pl.when(in_seg("fv"))
    def _():
        y = main()
        store_kv(fv_o, y)
        fv16_o[...] = y.astype(BF16)

    @pl.when(in_seg("fz"))
    def _():
        y = main()
        fg_o[...] = y * _sigmoid(y)

    @pl.when(in_seg("dq"))
    def _():
        dq_o[...] = head_norm(main(), gqd_ref, HEAD_DIM ** -0.5).astype(BF16)

    @pl.when(in_seg("dk"))
    def _():
        n = head_norm(main(), gkd_ref, 1.0)
        store_kv(dk_o, n)
        dk16_o[...] = n.astype(BF16)

    @pl.when(in_seg("dv"))
    def _():
        y = main()
        store_kv(dv_o, y)
        dv16_o[...] = y.astype(BF16)

    @pl.when(in_seg("dz"))
    def _():
        y = main()
        dg_o[...] = y * _sigmoid(y)

    @pl.when(in_seg("iq"))
    def _():
        iq_o[...] = (main() * (D_IDX ** -0.5)).astype(BF16)

    @pl.when(j == N_MAIN_BLOCKS)
    def _():
        ys = _dot(xn_s[...], ws_ref[...])
        small_o[...] = ys
        ff_t = ys.T[SM_FF:SM_FF + H_FOX, :]
        logft_o[...] = _log_sigmoid(ff_t + bf_ref[...])


def _proj(x2, nin, w_main, w_small, gqf, gkf, gqd, gkd, bf, tm):
    rows = x2.shape[0]
    assert rows % tm == 0
    grid = (rows // tm, N_MAIN_BLOCKS + 1)

    def seg_spec(name, width):
        a, n = SEG[name]
        return pl.BlockSpec((tm, width), lambda i, j: (i, jnp.clip(j - a, 0, n - 1)))

    row_vec = lambda n: pl.BlockSpec((1, n), lambda i, j: (0, 0))
    kv_rows = pl.BlockSpec((tm * KV_FOX, HEAD_DIM), lambda i, j: (i, 0))
    in_specs = [
        pl.BlockSpec((tm, D_MODEL), lambda i, j: (i, 0)),
        row_vec(D_MODEL),
        pl.BlockSpec((D_MODEL, PROJ_TN), lambda i, j: (0, jnp.minimum(j, N_MAIN_BLOCKS - 1))),
        pl.BlockSpec((D_MODEL, LANES), lambda i, j: (0, 0)),
        row_vec(HEAD_DIM), row_vec(HEAD_DIM), row_vec(HEAD_DIM), row_vec(HEAD_DIM),
        pl.BlockSpec((H_FOX, 1), lambda i, j: (0, 0)),
    ]
    sds = jax.ShapeDtypeStruct
    out_shape = [
        sds((rows, W_HALF), BF16),
        sds((rows * KV_FOX, HEAD_DIM), F32), sds((rows, W_KV), BF16),
        sds((rows * KV_FOX, HEAD_DIM), F32), sds((rows, W_KV), BF16),
        sds((rows, W_HALF), F32),
        sds((rows, W_HALF), BF16),
        sds((rows * KV_FOX, HEAD_DIM), F32), sds((rows, W_KV), BF16),
        sds((rows * KV_FOX, HEAD_DIM), F32), sds((rows, W_KV), BF16),
        sds((rows, W_HALF), F32),
        sds((rows, W_HALF), BF16),
        sds((rows, LANES), F32),
        sds((H_FOX, rows), F32),
    ]
    out_specs = [
        seg_spec("fq", PROJ_TN),
        kv_rows, seg_spec("fk", PROJ_TN),
        kv_rows, seg_spec("fv", PROJ_TN),
        seg_spec("fz", PROJ_TN),
        seg_spec("dq", PROJ_TN),
        kv_rows, seg_spec("dk", PROJ_TN),
        kv_rows, seg_spec("dv", PROJ_TN),
        seg_spec("dz", PROJ_TN),
        seg_spec("iq", PROJ_TN),
        pl.BlockSpec((tm, LANES), lambda i, j: (i, 0)),
        pl.BlockSpec((H_FOX, tm), lambda i, j: (0, i)),
    ]
    return pl.pallas_call(
        _proj_kernel, out_shape=out_shape, grid=grid, in_specs=in_specs, out_specs=out_specs,
        scratch_shapes=[pltpu.VMEM((tm, D_MODEL), BF16)],
        compiler_params=_cparams(("arbitrary", "arbitrary")), name="proj",
    )(x2, nin, w_main, w_small, gqf, gkf, gqd, gkd, bf)


def _cumsum_kernel(x_ref, o_ref):
    o_ref[...] = _cumsum_lanes(x_ref[...])


def _cumsum_prompt(logft, t):
    rows = logft.shape[1]
    return pl.pallas_call(
        _cumsum_kernel, out_shape=jax.ShapeDtypeStruct(logft.shape, F32), grid=(rows // t,),
        in_specs=[pl.BlockSpec((H_FOX, t), lambda b: (0, b))],
        out_specs=pl.BlockSpec((H_FOX, t), lambda b: (0, b)),
        compiler_params=_cparams(("arbitrary",)), name="cumsum",
    )(logft)


def _fox_kernel(q_ref, k_ref, v_ref, c_ref, g_ref, o_ref, m_s, l_s, acc_s, *, tq):
    qi = pl.program_id(1)
    row = lax.broadcasted_iota(I32, (tq, tq), 0)
    col = lax.broadcasted_iota(I32, (tq, tq), 1)
    m_s[...] = jnp.full(m_s.shape, NEG_INF, F32)
    l_s[...] = jnp.zeros(l_s.shape, F32)
    acc_s[...] = jnp.zeros(acc_s.shape, F32)

    def chunk(j, diag):
        off = pl.multiple_of(j * tq, tq)
        for kv in range(KV_FOX):
            q2 = jnp.concatenate([q_ref[:, (2 * kv + g) * HEAD_DIM:(2 * kv + g + 1) * HEAD_DIM] for g in range(2)], axis=0)
            s = _dot_nt(q2, k_ref[pl.ds(off, tq), kv * HEAD_DIM:(kv + 1) * HEAD_DIM])
            ps, alphas = [], []
            for g in range(2):
                h = 2 * kv + g
                sg = s[g * tq:(g + 1) * tq] - c_ref[kv, g, pl.ds(j, 1), :]
                if diag:
                    sg = jnp.where(col <= row, sg, NEG_INF)
                m_old = m_s[h]
                m_new = jnp.maximum(m_old, jnp.broadcast_to(jnp.max(sg, axis=-1, keepdims=True), (tq, LANES)))
                alphas.append(jnp.exp(m_old - m_new))
                ps.append(jnp.exp(sg - _tile_lanes(m_new, tq)).astype(BF16))
                m_s[h] = m_new
            v = v_ref[pl.ds(off, tq), kv * HEAD_DIM:(kv + 1) * HEAD_DIM]
            pv = _dot(jnp.concatenate(ps, axis=0), _with_ones(v))
            for g in range(2):
                h = 2 * kv + g
                l_s[h] = alphas[g] * l_s[h] + pv[g * tq:(g + 1) * tq, HEAD_DIM:]
                acc_s[h] = alphas[g] * acc_s[h] + pv[g * tq:(g + 1) * tq, :HEAD_DIM]

    def off_diag(j, _):
        chunk(j, False)
        return 0

    lax.fori_loop(0, qi, off_diag, 0)
    chunk(qi, True)
    for h in range(H_FOX):
        o = acc_s[h] / l_s[h]
        o_ref[:, h * HEAD_DIM:(h + 1) * HEAD_DIM] = (o * g_ref[:, h * HEAD_DIM:(h + 1) * HEAD_DIM]).astype(BF16)


def _fox_prompt(fq16, fk16, fv16, c4, fg, b, t, tq):
    nq = t // tq
    qmap = lambda bi, qi: (bi * nq + qi, 0)
    bmap = lambda bi, qi: (bi, 0)
    return pl.pallas_call(
        functools.partial(_fox_kernel, tq=tq),
        out_shape=jax.ShapeDtypeStruct((b * t, W_HALF), BF16), grid=(b, nq),
        in_specs=[
            pl.BlockSpec((tq, W_HALF), qmap),
            pl.BlockSpec((t, W_KV), bmap),
            pl.BlockSpec((t, W_KV), bmap),
            pl.BlockSpec((KV_FOX, 2, nq, tq), lambda bi, qi: (0, 0, bi, 0)),
            pl.BlockSpec((tq, W_HALF), qmap),
        ],
        out_specs=pl.BlockSpec((tq, W_HALF), qmap),
        scratch_shapes=[pltpu.VMEM((H_FOX, tq, LANES), F32), pltpu.VMEM((H_FOX, tq, LANES), F32),
                        pltpu.VMEM((H_FOX, tq, HEAD_DIM), F32)],
        compiler_params=_cparams(("arbitrary", "arbitrary")), name="fox_prompt",
    )(fq16, fk16, fv16, c4, fg)


def _t5_bucket(d):
    max_exact = N_BUCKETS // 2
    d = jnp.maximum(d, 0)
    lr = jnp.log(jnp.maximum(d, 1).astype(F32) / max_exact) / math.log(MAX_DISTANCE / max_exact)
    large = jnp.minimum(max_exact + (lr * (N_BUCKETS - max_exact)).astype(I32), N_BUCKETS - 1)
    return jnp.where(d < max_exact, d, large)


def _bias_from_dist(dist, rb_ref, h):
    bucket = _t5_bucket(dist)
    out = jnp.zeros(dist.shape, F32)
    for bkt in range(N_BUCKETS):
        out = jnp.where(bucket == bkt, rb_ref[bkt, h], out)
    return out


def _bias_tab_kernel(rb_ref, ptab_ref, stab_ref, *, tq):
    r = lax.broadcasted_iota(I32, (tq, 2 * tq), 0)
    c = lax.broadcasted_iota(I32, (tq, 2 * tq), 1)
    for h in range(H_DSA):
        ptab_ref[h] = _bias_from_dist(r - c + tq, rb_ref, h)
    rows = 4 * H_DSA
    rr = lax.broadcasted_iota(I32, (rows, LANES), 0)
    pos = lax.broadcasted_iota(I32, (rows, LANES), 1)
    t = rr // H_DSA
    hh = rr % H_DSA
    dists = (jnp.full((rows, LANES), 2 * MAX_DISTANCE, I32), PAGE_SIZE + t - pos, t - pos)
    for k, dist in enumerate(dists):
        acc = jnp.zeros((rows, LANES), F32)
        for h in range(H_DSA):
            acc = jnp.where(hh == h, _bias_from_dist(dist, rb_ref, h), acc)
        stab_ref[k] = acc


def _bias_tables(rel_bias, tq):
    return pl.pallas_call(
        functools.partial(_bias_tab_kernel, tq=tq),
        out_shape=[jax.ShapeDtypeStruct((H_DSA, tq, 2 * tq), F32),
                   jax.ShapeDtypeStruct((3, 4 * H_DSA, LANES), F32)],
        in_specs=[pl.BlockSpec(memory_space=pltpu.SMEM)],
        name="bias_tables",
    )(rel_bias)


def _kth_largest_key(count_ge, shape, k):
    def step(i, cur):
        cand = cur + lax.shift_left(jnp.int32(1), jnp.int32(31) - i)
        return jnp.where(count_ge(cand) >= k, cand, cur)
    return lax.fori_loop(0, 32, step, jnp.full(shape, INT_MIN, I32))


def _dsa_kernel(rb_ref, iq_ref, ik2_ref, sm_ref, q_ref, k_ref, v_ref, g_ref, tab_ref, o_ref,
                key_s, keyt_s, wb_s, m_s, l_s, acc_s, *, tq, topk):
    qi = pl.program_id(1)
    n_pairs = H_IDX // 2
    row = lax.broadcasted_iota(I32, (tq, tq), 0)
    col = lax.broadcasted_iota(I32, (tq, tq), 1)
    lane = lax.broadcasted_iota(I32, (tq, LANES), 1)

    w = sm_ref[:, SM_IW:SM_IW + H_IDX] * (H_IDX ** -0.5)
    for h in range(H_IDX):
        wb_s[h] = jnp.broadcast_to(w[:, h:h + 1], (tq, LANES))
    iqs = jnp.concatenate([iq_ref[:, p * LANES:(p + 1) * LANES] for p in range(n_pairs)], axis=0)

    def score_chunk(j, _):
        off = pl.multiple_of(j * tq, tq)
        ik2 = ik2_ref[pl.ds(off, tq), :]
        rhs = jnp.concatenate([jnp.where(lane < D_IDX, ik2, 0), jnp.where(lane >= D_IDX, ik2, 0)], axis=0)
        s2 = _dot_nt(iqs, rhs)
        sc = jnp.zeros((tq, tq), F32)
        for p in range(n_pairs):
            for e in range(2):
                r = jnp.maximum(s2[p * tq:(p + 1) * tq, e * tq:(e + 1) * tq], 0.0)
                sc = sc + jnp.concatenate([wb_s[2 * p + e]] * (tq // LANES), axis=1) * r
        key = _sort_key(jnp.where(off + col <= qi * tq + row, sc, NEG_INF))
        key_s[j] = key
        keyt_s[j] = key.T
        return 0

    lax.fori_loop(0, qi + 1, score_chunk, 0)

    def count_ge(cand):
        def body(j, cnt):
            hit = jnp.where(keyt_s[j] >= cand, 1, 0)
            return cnt + jnp.sum(hit.reshape(tq // 8, 8, tq), axis=0)
        cnt = lax.fori_loop(0, qi + 1, body, jnp.zeros((8, tq), I32))
        return jnp.sum(cnt, axis=0, keepdims=True)

    thr_row = _kth_largest_key(count_ge, (1, tq), topk)
    thr_col = jnp.broadcast_to(thr_row, (LANES, tq)).T
    thr = jnp.concatenate([thr_col] * (tq // LANES), axis=1)

    m_s[...] = jnp.full(m_s.shape, NEG_INF, F32)
    l_s[...] = jnp.zeros(l_s.shape, F32)
    acc_s[...] = jnp.zeros(acc_s.shape, F32)

    def attend(j, mode):
        off = pl.multiple_of(j * tq, tq)
        sel = key_s[j] >= thr
        if mode == 2:
            sel = jnp.logical_and(sel, col <= row)
        for kv in range(KV_DSA):
            q2 = jnp.concatenate([q_ref[:, (2 * kv + g) * HEAD_DIM:(2 * kv + g + 1) * HEAD_DIM] for g in range(2)], axis=0)
            s = _dot_nt(q2, k_ref[pl.ds(off, tq), kv * HEAD_DIM:(kv + 1) * HEAD_DIM])
            ps, alphas = [], []
            for g in range(2):
                h = 2 * kv + g
                sg = s[g * tq:(g + 1) * tq]
                if mode == 0:
                    sg = sg + rb_ref[N_BUCKETS - 1, h]
                elif mode == 1:
                    sg = sg + tab_ref[h, :, :tq]
                else:
                    sg = sg + tab_ref[h, :, tq:]
                sg = jnp.where(sel, sg, NEG_INF)
                m_old = m_s[h]
                m_new = jnp.maximum(m_old, jnp.broadcast_to(jnp.max(sg, axis=-1, keepdims=True), (tq, LANES)))
                m_safe = jnp.where(m_new == NEG_INF, 0.0, m_new)
                alphas.append(jnp.exp(m_old - m_safe))
                ps.append(jnp.exp(sg - _tile_lanes(m_safe, tq)).astype(BF16))
                m_s[h] = m_new
            v = v_ref[pl.ds(off, tq), kv * HEAD_DIM:(kv + 1) * HEAD_DIM]
            pv = _dot(jnp.concatenate(ps, axis=0), _with_ones(v))
            for g in range(2):
                h = 2 * kv + g
                l_s[h] = alphas[g] * l_s[h] + pv[g * tq:(g + 1) * tq, HEAD_DIM:]
                acc_s[h] = alphas[g] * acc_s[h] + pv[g * tq:(g + 1) * tq, :HEAD_DIM]

    def far(j, _):
        attend(j, 0)
        return 0

    lax.fori_loop(0, jnp.maximum(qi - 1, 0), far, 0)

    @pl.when(qi >= 1)
    def _():
        attend(qi - 1, 1)

    attend(qi, 2)

    for h in range(H_DSA):
        o = acc_s[h] / l_s[h]
        o_ref[:, h * HEAD_DIM:(h + 1) * HEAD_DIM] = (o * g_ref[:, h * HEAD_DIM:(h + 1) * HEAD_DIM]).astype(BF16)


def _dsa_prompt(rel_bias, iq16, ik2, small, dq16, dk16, dv16, dg, ptab, b, t, tq, topk):
    nq = t // tq
    rows = b * t
    qmap = lambda bi, qi: (bi * nq + qi, 0)
    bmap = lambda bi, qi: (bi, 0)
    return pl.pallas_call(
        functools.partial(_dsa_kernel, tq=tq, topk=topk),
        out_shape=jax.ShapeDtypeStruct((rows, W_HALF), BF16), grid=(b, nq),
        in_specs=[
            pl.BlockSpec(memory_space=pltpu.SMEM),
            pl.BlockSpec((tq, H_IDX * D_IDX), qmap),
            pl.BlockSpec((t, LANES), bmap),
            pl.BlockSpec((tq, LANES), qmap),
            pl.BlockSpec((tq, W_HALF), qmap),
            pl.BlockSpec((t, W_KV), bmap),
            pl.BlockSpec((t, W_KV), bmap),
            pl.BlockSpec((tq, W_HALF), qmap),
            pl.BlockSpec((H_DSA, tq, 2 * tq), lambda bi, qi: (0, 0, 0), pipeline_mode=pl.Buffered(1)),
        ],
        out_specs=pl.BlockSpec((tq, W_HALF), qmap),
        scratch_shapes=[pltpu.VMEM((nq, tq, tq), I32), pltpu.VMEM((nq, tq, tq), I32),
                        pltpu.VMEM((H_IDX, tq, LANES), F32),
                        pltpu.VMEM((H_DSA, tq, LANES), F32), pltpu.VMEM((H_DSA, tq, LANES), F32),
                        pltpu.VMEM((H_DSA, tq, HEAD_DIM), F32)],
        compiler_params=_cparams(("arbitrary", "arbitrary")), name="dsa_prompt",
    )(rel_bias, iq16, ik2, small, dq16, dk16, dv16, dg, ptab)


def _page_specs(shape_tail, pps, new_step_tail):
    nd = len(shape_tail)
    return [pl.BlockSpec((1,) + shape_tail, functools.partial(
        lambda s, p, pt, i: (pt[s, p * pps + i],) + (0,) * nd, i=i)) for i in range(pps)]


def _smp_score_kernel(pt_ref, iq_ref, w_ref, *refs, pps, n_tok):
    ik_refs, iknew_ref, o_ref = refs[:pps], refs[pps], refs[pps + 1]
    p = pl.program_id(1)
    iq = iq_ref[0]
    wcol = w_ref[0]

    def page_scores(ik_t):
        r = jnp.maximum(_dot(iq, ik_t.astype(BF16)), 0.0) * wcol
        return jnp.sum(r.reshape(n_tok, H_IDX, ik_t.shape[1]), axis=1)

    ik_all = jnp.concatenate([ik_refs[i][0] for i in range(pps)], axis=1)
    o_ref[0, :, :pps * LANES] = _sort_key(page_scores(ik_all))

    @pl.when(p < pl.num_programs(1) - 1)
    def _():
        o_ref[0, :, pps * LANES:] = jnp.full((n_tok, LANES), INT_MIN, I32)

    @pl.when(p == pl.num_programs(1) - 1)
    def _():
        sc = page_scores(iknew_ref[0])
        t = lax.broadcasted_iota(I32, (n_tok, LANES), 0)
        pos = lax.broadcasted_iota(I32, (n_tok, LANES), 1)
        o_ref[0, :, pps * LANES:] = _sort_key(jnp.where(pos <= t, sc, NEG_INF))


def _smp_scores(page_table, iq_s, w_s, ik_pool, ik_new, pps):
    s, n_pages = page_table.shape
    n_tok = iq_s.shape[1] // H_IDX
    steps = n_pages // pps
    width = (pps + 1) * LANES
    grid_spec = pltpu.PrefetchScalarGridSpec(
        num_scalar_prefetch=1, grid=(s, steps),
        in_specs=[pl.BlockSpec((1, n_tok * H_IDX, D_IDX), lambda si, p, pt: (si, 0, 0)),
                  pl.BlockSpec((1, n_tok * H_IDX, 1), lambda si, p, pt: (si, 0, 0))]
                 + _page_specs((D_IDX, PAGE_SIZE), pps, None)
                 + [pl.BlockSpec((1, D_IDX, PAGE_SIZE), lambda si, p, pt: (si, 0, 0))],
        out_specs=pl.BlockSpec((1, n_tok, width), lambda si, p, pt: (si, 0, p)),
    )
    return pl.pallas_call(
        functools.partial(_smp_score_kernel, pps=pps, n_tok=n_tok),
        out_shape=jax.ShapeDtypeStruct((s, n_tok, steps * width), I32), grid_spec=grid_spec,
        compiler_params=_cparams(("arbitrary", "arbitrary")), name="smp_scores",
    )(page_table, iq_s, w_s, *([ik_pool] * pps), ik_new)


def _smp_thresh_kernel(key_ref, o_ref, *, topk):
    keys = key_ref[0]

    def count_ge(cand):
        return jnp.sum(jnp.where(keys >= cand, 1, 0), axis=-1, keepdims=True)

    thr = _kth_largest_key(count_ge, (keys.shape[0], 1), topk)
    o_ref[0] = jnp.broadcast_to(thr, (keys.shape[0], LANES))


def _smp_thresh(keys, topk):
    s, n_tok, n = keys.shape
    return pl.pallas_call(
        functools.partial(_smp_thresh_kernel, topk=topk),
        out_shape=jax.ShapeDtypeStruct((s, n_tok, LANES), I32), grid=(s,),
        in_specs=[pl.BlockSpec((1, n_tok, n), lambda si: (si, 0, 0))],
        out_specs=pl.BlockSpec((1, n_tok, LANES), lambda si: (si, 0, 0)),
        compiler_params=_cparams(("arbitrary",)), name="smp_thresh",
    )(keys)


def _smp_attn_kernel(pt_ref, q_ref, g_ref, *refs, pps, n_tok, n_kv, mode):
    k_refs, v_refs = refs[:pps], refs[pps:2 * pps]
    rest = refs[2 * pps:]
    if mode == "fox":
        lf_refs, rest = rest[:pps], rest[pps:]
        knew_ref, vnew_ref, lfnew_ref, o_ref, m_s, l_s, acc_s, carry_s = rest
    else:
        knew_ref, vnew_ref, key_ref, thr_ref, tab_ref, o_ref, m_s, l_s, acc_s = rest
    p = pl.program_id(1)
    last = pl.num_programs(1) - 1
    rows = q_ref.shape[1]
    heads = rows // n_tok
    grp = heads // n_kv
    q = q_ref[0]
    rr = lax.broadcasted_iota(I32, (rows, LANES), 0)
    pos = lax.broadcasted_iota(I32, (rows, LANES), 1)
    row_kv = (rr % heads) // grp
    qcat = jnp.concatenate([jnp.where(row_kv == kv, q, jnp.zeros_like(q)) for kv in range(n_kv)], axis=1)

    def heads_of(ref, kv):
        return ref[0, pl.ds(kv, PAGE_SIZE, stride=n_kv), :].astype(BF16)

    @pl.when(p == 0)
    def _():
        m_s[...] = jnp.full(m_s.shape, NEG_INF, F32)
        l_s[...] = jnp.zeros(l_s.shape, F32)
        acc_s[...] = jnp.zeros(acc_s.shape, F32)
        if mode == "fox":
            carry_s[...] = jnp.zeros(carry_s.shape, F32)

    def attend(kv_refs, bias, sel):
        kcat = jnp.concatenate([jnp.concatenate([heads_of(k_ref, kv) for kv in range(n_kv)], axis=1)
                                for k_ref, _ in kv_refs], axis=0)
        s = _dot_nt(qcat, kcat) + bias
        if sel is not None:
            s = jnp.where(sel, s, NEG_INF)
        m_old = m_s[...]
        m_new = jnp.maximum(m_old, jnp.max(s, axis=-1, keepdims=True))
        m_safe = jnp.where(m_new == NEG_INF, 0.0, m_new)
        alpha = jnp.exp(m_old - m_safe)
        pr = jnp.exp(s - m_safe)
        l_s[...] = alpha * l_s[...] + jnp.sum(pr, axis=-1, keepdims=True)
        m_s[...] = m_new
        pcat = jnp.concatenate([jnp.where(row_kv == kv, pr[:, i * LANES:(i + 1) * LANES], 0.0).astype(BF16)
                                for i in range(len(kv_refs)) for kv in range(n_kv)], axis=1)
        vcat = jnp.concatenate([heads_of(v_ref, kv) for _, v_ref in kv_refs for kv in range(n_kv)], axis=0)
        acc_s[...] = alpha * acc_s[...] + _dot(pcat, vcat)

    def fox_bias(lf_list):
        c = carry_s[...] + _cumsum_lanes(jnp.concatenate([r[0] for r in lf_list], axis=1))
        carry_s[...] = c[:, c.shape[1] - 1:]
        return -jnp.concatenate([c] * n_tok, axis=0)

    def expand_rows(x):
        return jnp.concatenate([jnp.broadcast_to(x[t:t + 1], (heads, x.shape[1])) for t in range(n_tok)], axis=0)

    pages = list(zip(k_refs, v_refs))
    if mode == "fox":
        attend(pages, fox_bias(lf_refs), None)
    else:
        last_bias = tab_ref[jnp.where(p == last, 1, 0)]
        bias = jnp.concatenate([tab_ref[0]] * (pps - 1) + [last_bias], axis=1)
        sel = expand_rows(key_ref[0, :, :pps * LANES]) >= expand_rows(thr_ref[0])[:, :1]
        attend(pages, bias, sel)

    @pl.when(p == last)
    def _():
        causal = pos <= rr // heads
        if mode == "fox":
            attend([(knew_ref, vnew_ref)], fox_bias([lfnew_ref]), causal)
        else:
            sel = expand_rows(key_ref[0, :, pps * LANES:]) >= expand_rows(thr_ref[0])
            attend([(knew_ref, vnew_ref)], tab_ref[2], jnp.logical_and(sel, causal))
        o_ref[0] = ((acc_s[...] / l_s[...]) * g_ref[0]).astype(BF16)


def _smp_attn(mode, page_table, q_s, g_s, k_pool, v_pool, k_new, v_new, extra, pps):
    s, n_pages = page_table.shape
    steps = n_pages // pps
    rows = q_s.shape[1]
    n_tok = rows // H_FOX
    seq3 = lambda a, b: pl.BlockSpec((1, a, b), lambda si, p, pt: (si, 0, 0))
    in_specs = [seq3(rows, HEAD_DIM), seq3(rows, HEAD_DIM)]
    in_specs += _page_specs((PAGE_SIZE * KV_FOX, HEAD_DIM), pps, None) * 2
    args = [q_s, g_s] + [k_pool] * pps + [v_pool] * pps
    scratch = [pltpu.VMEM((rows, 1), F32), pltpu.VMEM((rows, 1), F32), pltpu.VMEM((rows, HEAD_DIM), F32)]
    if mode == "fox":
        lf_pool, lf_new = extra
        in_specs += _page_specs((H_FOX, PAGE_SIZE), pps, None)
        in_specs += [seq3(PAGE_SIZE * KV_FOX, HEAD_DIM)] * 2 + [seq3(H_FOX, PAGE_SIZE)]
        args += [lf_pool] * pps + [k_new, v_new, lf_new]
        scratch.append(pltpu.VMEM((H_FOX, 1), F32))
    else:
        keys, thr, stab = extra
        width = (pps + 1) * LANES
        in_specs += [seq3(PAGE_SIZE * KV_FOX, HEAD_DIM)] * 2
        in_specs += [pl.BlockSpec((1, n_tok, width), lambda si, p, pt: (si, 0, p)),
                     seq3(n_tok, LANES),
                     pl.BlockSpec((3, rows, LANES), lambda si, p, pt: (0, 0, 0))]
        args += [k_new, v_new, keys, thr, stab]
    grid_spec = pltpu.PrefetchScalarGridSpec(
        num_scalar_prefetch=1, grid=(s, steps), in_specs=in_specs,
        out_specs=seq3(rows, HEAD_DIM), scratch_shapes=scratch)
    return pl.pallas_call(
        functools.partial(_smp_attn_kernel, pps=pps, n_tok=n_tok, n_kv=KV_FOX, mode=mode),
        out_shape=jax.ShapeDtypeStruct((s, rows, HEAD_DIM), BF16), grid_spec=grid_spec,
        compiler_params=_cparams(("arbitrary", "arbitrary")), name="smp_attn_" + mode,
    )(page_table, *args)


def _out_kernel(x_ref, gf_ref, gd_ref, p_ref, wo_ref, wpg_ref, wple_ref, gple_ref, o_ref):
    h = x_ref[...] + _dot(gf_ref[...], wo_ref[:W_HALF, :]) + _dot(gd_ref[...], wo_ref[W_HALF:, :])
    gate = _sigmoid(_dot(h.astype(BF16), wpg_ref[...]))
    e = _dot(p_ref[...].astype(BF16), wple_ref[...])
    e = e * lax.rsqrt(jnp.mean(e * e, axis=-1, keepdims=True) + EPS) * gple_ref[...]
    o_ref[...] = h + gate * e


def _out(x2, gf, gd, p2, wo, wpg, wple, gple, tm):
    rows = x2.shape[0]
    const = lambda shape: pl.BlockSpec(shape, lambda i: (0, 0), pipeline_mode=pl.Buffered(1))
    rmap = lambda i: (i, 0)
    return pl.pallas_call(
        _out_kernel, out_shape=jax.ShapeDtypeStruct((rows, D_MODEL), F32), grid=(rows // tm,),
        in_specs=[pl.BlockSpec((tm, D_MODEL), rmap), pl.BlockSpec((tm, W_HALF), rmap), pl.BlockSpec((tm, W_HALF), rmap),
                  pl.BlockSpec((tm, PLE_DIM), rmap),
                  const((D_MODEL, D_MODEL)), const((D_MODEL, D_MODEL)), const((PLE_DIM, D_MODEL)), const((1, D_MODEL))],
        out_specs=pl.BlockSpec((tm, D_MODEL), rmap),
        compiler_params=_cparams(("arbitrary",)), name="out",
    )(x2, gf, gd, p2, wo, wpg, wple, gple)


def _prep_w_in(w):
    points = [sum(SPLIT_SIZES[:i + 1]) for i in range(len(SPLIT_SIZES) - 1)]
    fq, fk, fv, ff, fz, dq, dk, dv, dz, iq, ik, iw = jnp.split(w, points, axis=1)
    main = jnp.concatenate([fq, fk, fv, fz, dq, dk, dv, dz, iq], axis=1).astype(BF16)
    pad = jnp.zeros((w.shape[0], LANES - D_IDX - H_IDX - H_FOX), w.dtype)
    small = jnp.concatenate([ik, iw, ff, pad], axis=1).astype(BF16)
    return main, small


def _pick_tile(n, pref):
    t = pref
    while n % t:
        t //= 2
    return t


def kernel(x_prompt, x_sample, cache_fox_k, cache_fox_v, cache_fox_logf, cache_dsa_k, cache_dsa_v, cache_idx_k,
           page_table, p_prompt, p_sample, rel_bias, norm_in, w_in, b_f, q_norm_fox, k_norm_fox, q_norm_dsa,
           k_norm_dsa, w_out, w_ple, ple_norm, w_pg):
    b, t, _ = x_prompt.shape
    s, n_tok, _ = x_sample.shape
    n_pages = page_table.shape[1]
    n_past = n_pages * PAGE_SIZE
    n_pool = cache_fox_k.shape[1]
    assert cache_fox_k.shape[0] == 1 and n_tok * H_FOX == 32

    w_main, w_small = _prep_w_in(w_in[0])
    nin = norm_in[0].reshape(1, D_MODEL)
    vec = lambda a: a[0].reshape(1, HEAD_DIM)
    gains = (vec(q_norm_fox), vec(k_norm_fox), vec(q_norm_dsa), vec(k_norm_dsa))
    bf = b_f[0].reshape(H_FOX, 1)
    wo = w_out[0].astype(BF16)
    wpg = w_pg[0].astype(BF16)
    wple = w_ple[0].astype(BF16)
    gple = ple_norm[0].reshape(1, D_MODEL)

    tq_d = _pick_tile(t, 256)
    ptab, stab = _bias_tables(rel_bias, tq_d)

    rows_p = b * t
    xp = x_prompt.reshape(rows_p, D_MODEL)
    (fq16, fk, fk16, fv, fv16, fg, dq16, dk, dk16, dv, dv16, dg, iq16, small, logft) = _proj(
        xp, nin, w_main, w_small, *gains, bf, _pick_tile(rows_p, 512))
    tq_f = _pick_tile(t, 256)
    ct = _cumsum_prompt(logft, t)
    c4 = ct.reshape(KV_FOX, 2, rows_p // tq_f, tq_f)
    gf = _fox_prompt(fq16, fk16, fv16, c4, fg, b, t, tq_f)
    ik16 = small[:, SM_IK:SM_IK + D_IDX].astype(BF16)
    ik2 = jnp.concatenate([ik16, ik16], axis=1)
    topk_p = min(TOPK_MAX, t // 4)
    gd = _dsa_prompt(rel_bias, iq16, ik2, small, dq16, dk16, dv16, dg, ptab, b, t, tq_d, topk_p)
    y_p = _out(xp, gf, gd, p_prompt[0].reshape(rows_p, PLE_DIM), wo, wpg, wple, gple, _pick_tile(rows_p, 256))

    rows_s = s * n_tok
    xs = x_sample.reshape(rows_s, D_MODEL)
    (sfq16, sfk, _, sfv, _, sfg, sdq16, sdk, _, sdv, _, sdg, siq16, ssmall, slogft) = _proj(
        xs, nin, w_main, w_small, *gains, bf, _pick_tile(rows_s, 512))
    pps = _pick_tile(n_pages, 16)
    rows_q = n_tok * H_FOX

    def new_page(a):
        a = a.reshape(s, n_tok * KV_FOX, HEAD_DIM)
        return jnp.pad(a, ((0, 0), (0, (PAGE_SIZE - n_tok) * KV_FOX), (0, 0)))

    q_rows = lambda a: a.reshape(s, rows_q, HEAD_DIM)
    pool = lambda c: c[0].reshape(n_pool, PAGE_SIZE * KV_FOX, HEAD_DIM)
    lf_pool = jnp.transpose(cache_fox_logf[0], (0, 2, 1))
    lf_new = jnp.pad(jnp.transpose(slogft.reshape(H_FOX, s, n_tok), (1, 0, 2)), ((0, 0), (0, 0), (0, PAGE_SIZE - n_tok)))
    sgf = _smp_attn("fox", page_table, q_rows(sfq16), q_rows(sfg), pool(cache_fox_k), pool(cache_fox_v),
                    new_page(sfk), new_page(sfv), (lf_pool, lf_new), pps)

    iq_s = siq16.reshape(s, n_tok * H_IDX, D_IDX)
    w_s = (ssmall[:, SM_IW:SM_IW + H_IDX] * (H_IDX ** -0.5)).reshape(s, n_tok * H_IDX, 1)
    ik_new = jnp.pad(jnp.transpose(ssmall[:, SM_IK:SM_IK + D_IDX].reshape(s, n_tok, D_IDX), (0, 2, 1)),
                     ((0, 0), (0, 0), (0, PAGE_SIZE - n_tok)))
    keys = _smp_scores(page_table, iq_s, w_s, jnp.transpose(cache_idx_k[0], (0, 2, 1)), ik_new, pps)
    topk_s = min(TOPK_MAX, (n_past + n_tok) // 4)
    thr = _smp_thresh(keys, topk_s)
    sgd = _smp_attn("dsa", page_table, q_rows(sdq16), q_rows(sdg), pool(cache_dsa_k), pool(cache_dsa_v),
                    new_page(sdk), new_page(sdv), (keys, thr, stab), pps)
    y_s = _out(xs, sgf.reshape(rows_s, W_HALF), sgd.reshape(rows_s, W_HALF), p_sample[0].reshape(rows_s, PLE_DIM),
               wo, wpg, wple, gple, _pick_tile(rows_s, 256))

    def kv5(a, bb, tt):
        return a.reshape(1, bb, tt, KV_FOX, HEAD_DIM)

    def outs(bb, tt, fk_, fv_, logft_, dk_, dv_, small_):
        return (kv5(fk_, bb, tt), kv5(fv_, bb, tt), logft_.T.reshape(1, bb, tt, H_FOX),
                kv5(dk_, bb, tt), kv5(dv_, bb, tt), small_[:, SM_IK:SM_IK + D_IDX].reshape(1, bb, tt, D_IDX))

    return ((y_p.reshape(b, t, D_MODEL), y_s.reshape(s, n_tok, D_MODEL))
            + outs(b, t, fk, fv, logft, dk, dv, small)
            + outs(s, n_tok, sfk, sfv, slogft, sdk, sdv, ssmall))
```

```python
import functools
import math

import jax
import jax.numpy as jnp
from jax import lax
from jax.experimental import pallas as pl
from jax.experimental.pallas import tpu as pltpu

F32 = jnp.float32
BF16 = jnp.bfloat16
I32 = jnp.int32

D_MODEL = 2048
HEAD_DIM = 128
H_FOX = 8
KV_FOX = 4
H_DSA = 8
KV_DSA = 4
H_IDX = 16
D_IDX = 64
TOPK_MAX = 256
N_BUCKETS = 32
MAX_DISTANCE = 128
PLE_DIM = 256
PAGE_SIZE = 128
EPS = 1e-6
W_HALF = H_FOX * HEAD_DIM
W_KV = KV_FOX * HEAD_DIM
SPLIT_SIZES = (W_HALF, W_KV, W_KV, H_FOX, W_HALF, W_HALF, W_KV, W_KV, W_HALF, H_IDX * D_IDX, D_IDX, H_IDX)

LANES = 128
INT_MIN = -(2 ** 31)
NEG_INF = float("-inf")
VMEM_LIMIT = 56 * 1024 * 1024

PROJ_TN = W_HALF
PROJ_STEPS = ("fq", "fkv", "fz", "dq", "dkv", "dz", "iq")
N_MAIN_BLOCKS = len(PROJ_STEPS)
SM_IK, SM_IW, SM_FF = 0, D_IDX, D_IDX + H_IDX


def _cparams(sem):
    return pltpu.CompilerParams(dimension_semantics=sem, vmem_limit_bytes=VMEM_LIMIT)


def _dot_nt(a, b):
    return lax.dot_general(a, b, (((1,), (1,)), ((), ())), preferred_element_type=F32)


def _dot(a, b):
    return jnp.dot(a, b, preferred_element_type=F32)


def _log_sigmoid(x):
    return -(jnp.maximum(-x, 0.0) + jnp.log(1.0 + jnp.exp(-jnp.abs(x))))


def _sigmoid(x):
    return 1.0 / (1.0 + jnp.exp(-x))


def _sort_key(x):
    b = pltpu.bitcast(x, I32)
    return b ^ ((b >> 31) & jnp.int32(0x7FFFFFFF))


def _tile_lanes(x, width):
    return x if width == LANES else jnp.concatenate([x] * (width // LANES), axis=1)


def _with_ones(v):
    return jnp.concatenate([v, jnp.ones_like(v)], axis=1)


def _cumsum_lanes(x):
    n = x.shape[-1]
    lane = lax.broadcasted_iota(I32, x.shape, x.ndim - 1)
    k = 1
    while k < n:
        x = x + jnp.where(lane >= k, pltpu.roll(x, k, axis=x.ndim - 1), 0.0)
        k *= 2
    return x


def _proj_kernel(x_ref, nin_ref, w_ref, ws_ref, gqf_ref, gkf_ref, gqd_ref, gkd_ref, bf_ref,
                 fq_o, fk_o, fk16_o, fv_o, fv16_o, fg_o, dq_o, dk_o, dk16_o, dv_o, dv16_o, dg_o,
                 iq_o, small_o, logft_o, xn_s):
    j = pl.program_id(1)

    @pl.when(j == 0)
    def _():
        x = x_ref[...]
        ms = jnp.mean(x * x, axis=-1, keepdims=True)
        xn_s[...] = (x * lax.rsqrt(ms + EPS) * nin_ref[...]).astype(BF16)

    def main():
        return _dot(xn_s[...], w_ref[...])

    def head_norm(y, g_ref, scale):
        outs = []
        for c in range(y.shape[1] // HEAD_DIM):
            yh = y[:, c * HEAD_DIM:(c + 1) * HEAD_DIM]
            n = yh * lax.rsqrt(jnp.mean(yh * yh, axis=-1, keepdims=True) + EPS) * g_ref[...]
            outs.append(n * scale if scale != 1.0 else n)
        return jnp.concatenate(outs, axis=1)

    def store_kv(o_ref, y):
        for kv in range(KV_FOX):
            o_ref[pl.ds(kv, y.shape[0], stride=KV_FOX), :] = y[:, kv * HEAD_DIM:(kv + 1) * HEAD_DIM]

    def kv_step(gk_ref, k_o, k16_o, v_o, v16_o):
        y = main()
        n = head_norm(y[:, :W_KV], gk_ref, 1.0)
        store_kv(k_o, n)
        k16_o[...] = n.astype(BF16)
        store_kv(v_o, y[:, W_KV:])
        v16_o[...] = y[:, W_KV:].astype(BF16)

    def gate_step(o_ref):
        y = main()
        o_ref[...] = (y * _sigmoid(y)).astype(BF16)

    @pl.when(j == PROJ_STEPS.index("fq"))
    def _():
        fq_o[...] = head_norm(main(), gqf_ref, HEAD_DIM ** -0.5).astype(BF16)

    @pl.when(j == PROJ_STEPS.index("fkv"))
    def _():
        kv_step(gkf_ref, fk_o, fk16_o, fv_o, fv16_o)

    @pl.when(j == PROJ_STEPS.index("fz"))
    def _():
        gate_step(fg_o)

    @pl.when(j == PROJ_STEPS.index("dq"))
    def _():
        dq_o[...] = head_norm(main(), gqd_ref, HEAD_DIM ** -0.5).astype(BF16)

    @pl.when(j == PROJ_STEPS.index("dkv"))
    def _():
        kv_step(gkd_ref, dk_o, dk16_o, dv_o, dv16_o)

    @pl.when(j == PROJ_STEPS.index("dz"))
    def _():
        gate_step(dg_o)

    @pl.when(j == PROJ_STEPS.index("iq"))
    def _():
        iq_o[...] = (main() * (D_IDX ** -0.5)).astype(BF16)

    @pl.when(j == N_MAIN_BLOCKS)
    def _():
        ys = _dot(xn_s[...], ws_ref[...])
        small_o[...] = ys
        ff_t = ys.T[SM_FF:SM_FF + H_FOX, :]
        logft_o[...] = _log_sigmoid(ff_t + bf_ref[...])


def _proj(x2, nin, w_main, w_small, gqf, gkf, gqd, gkd, bf, tm):
    rows = x2.shape[0]
    assert rows % tm == 0
    grid = (rows // tm, N_MAIN_BLOCKS + 1)

    rows_of = lambda width: pl.BlockSpec((tm, width), lambda i, j: (i, 0))
    row_vec = lambda n: pl.BlockSpec((1, n), lambda i, j: (0, 0))
    kv_rows = pl.BlockSpec((tm * KV_FOX, HEAD_DIM), lambda i, j: (i, 0))
    in_specs = [
        pl.BlockSpec((tm, D_MODEL), lambda i, j: (i, 0)),
        row_vec(D_MODEL),
        pl.BlockSpec((D_MODEL, PROJ_TN), lambda i, j: (0, jnp.minimum(j, N_MAIN_BLOCKS - 1))),
        pl.BlockSpec((D_MODEL, LANES), lambda i, j: (0, 0)),
        row_vec(HEAD_DIM), row_vec(HEAD_DIM), row_vec(HEAD_DIM), row_vec(HEAD_DIM),
        pl.BlockSpec((H_FOX, 1), lambda i, j: (0, 0)),
    ]
    sds = jax.ShapeDtypeStruct
    out_shape = [
        sds((rows, W_HALF), BF16),
        sds((rows * KV_FOX, HEAD_DIM), F32), sds((rows, W_KV), BF16),
        sds((rows * KV_FOX, HEAD_DIM), F32), sds((rows, W_KV), BF16),
        sds((rows, W_HALF), BF16),
        sds((rows, W_HALF), BF16),
        sds((rows * KV_FOX, HEAD_DIM), F32), sds((rows, W_KV), BF16),
        sds((rows * KV_FOX, HEAD_DIM), F32), sds((rows, W_KV), BF16),
        sds((rows, W_HALF), BF16),
        sds((rows, W_HALF), BF16),
        sds((rows, LANES), F32),
        sds((H_FOX, rows), F32),
    ]
    out_specs = [
        rows_of(W_HALF),
        kv_rows, rows_of(W_KV),
        kv_rows, rows_of(W_KV),
        rows_of(W_HALF),
        rows_of(W_HALF),
        kv_rows, rows_of(W_KV),
        kv_rows, rows_of(W_KV),
        rows_of(W_HALF),
        rows_of(W_HALF),
        rows_of(LANES),
        pl.BlockSpec((H_FOX, tm), lambda i, j: (0, i)),
    ]
    return pl.pallas_call(
        _proj_kernel, out_shape=out_shape, grid=grid, in_specs=in_specs, out_specs=out_specs,
        scratch_shapes=[pltpu.VMEM((tm, D_MODEL), BF16)],
        compiler_params=_cparams(("arbitrary", "arbitrary")), name="proj",
    )(x2, nin, w_main, w_small, gqf, gkf, gqd, gkd, bf)


def _cumsum_kernel(x_ref, o_ref):
    o_ref[...] = _cumsum_lanes(x_ref[...])


def _cumsum_prompt(logft, t):
    rows = logft.shape[1]
    return pl.pallas_call(
        _cumsum_kernel, out_shape=jax.ShapeDtypeStruct(logft.shape, F32), grid=(rows // t,),
        in_specs=[pl.BlockSpec((H_FOX, t), lambda b: (0, b))],
        out_specs=pl.BlockSpec((H_FOX, t), lambda b: (0, b)),
        compiler_params=_cparams(("arbitrary",)), name="cumsum",
    )(logft)


def _fox_kernel(q_ref, k_ref, v_ref, c_ref, g_ref, o_ref, m_s, l_s, acc_s, *, tq):
    qi = pl.program_id(1)
    row = lax.broadcasted_iota(I32, (tq, tq), 0)
    col = lax.broadcasted_iota(I32, (tq, tq), 1)
    m_s[...] = jnp.full(m_s.shape, NEG_INF, F32)
    l_s[...] = jnp.zeros(l_s.shape, F32)
    acc_s[...] = jnp.zeros(acc_s.shape, F32)

    def chunk(j, diag):
        off = pl.multiple_of(j * tq, tq)
        for kv in range(KV_FOX):
            q2 = jnp.concatenate([q_ref[:, (2 * kv + g) * HEAD_DIM:(2 * kv + g + 1) * HEAD_DIM] for g in range(2)], axis=0)
            s = _dot_nt(q2, k_ref[pl.ds(off, tq), kv * HEAD_DIM:(kv + 1) * HEAD_DIM])
            ps, alphas = [], []
            for g in range(2):
                h = 2 * kv + g
                sg = s[g * tq:(g + 1) * tq] - c_ref[kv, g, pl.ds(j, 1), :]
                if diag:
                    sg = jnp.where(col <= row, sg, NEG_INF)
                m_old = m_s[h]
                m_new = jnp.maximum(m_old, jnp.broadcast_to(jnp.max(sg, axis=-1, keepdims=True), (tq, LANES)))
                alphas.append(jnp.exp(m_old - m_new))
                ps.append(jnp.exp(sg - _tile_lanes(m_new, tq)).astype(BF16))
                m_s[h] = m_new
            v = v_ref[pl.ds(off, tq), kv * HEAD_DIM:(kv + 1) * HEAD_DIM]
            pv = _dot(jnp.concatenate(ps, axis=0), _with_ones(v))
            for g in range(2):
                h = 2 * kv + g
                l_s[h] = alphas[g] * l_s[h] + pv[g * tq:(g + 1) * tq, HEAD_DIM:]
                acc_s[h] = alphas[g] * acc_s[h] + pv[g * tq:(g + 1) * tq, :HEAD_DIM]

    def off_diag(j, _):
        chunk(j, False)
        return 0

    lax.fori_loop(0, qi, off_diag, 0)
    chunk(qi, True)
    for h in range(H_FOX):
        o = acc_s[h] / l_s[h]
        o_ref[:, h * HEAD_DIM:(h + 1) * HEAD_DIM] = (o * g_ref[:, h * HEAD_DIM:(h + 1) * HEAD_DIM]).astype(BF16)


def _fox_prompt(fq16, fk16, fv16, c4, fg, b, t, tq):
    nq = t // tq
    qmap = lambda bi, qi: (bi * nq + qi, 0)
    bmap = lambda bi, qi: (bi, 0)
    return pl.pallas_call(
        functools.partial(_fox_kernel, tq=tq),
        out_shape=jax.ShapeDtypeStruct((b * t, W_HALF), BF16), grid=(b, nq),
        in_specs=[
            pl.BlockSpec((tq, W_HALF), qmap),
            pl.BlockSpec((t, W_KV), bmap),
            pl.BlockSpec((t, W_KV), bmap),
            pl.BlockSpec((KV_FOX, 2, nq, tq), lambda bi, qi: (0, 0, bi, 0)),
            pl.BlockSpec((tq, W_HALF), qmap),
        ],
        out_specs=pl.BlockSpec((tq, W_HALF), qmap),
        scratch_shapes=[pltpu.VMEM((H_FOX, tq, LANES), F32), pltpu.VMEM((H_FOX, tq, LANES), F32),
                        pltpu.VMEM((H_FOX, tq, HEAD_DIM), F32)],
        compiler_params=_cparams(("arbitrary", "arbitrary")), name="fox_prompt",
    )(fq16, fk16, fv16, c4, fg)


def _t5_bucket(d):
    max_exact = N_BUCKETS // 2
    d = jnp.maximum(d, 0)
    lr = jnp.log(jnp.maximum(d, 1).astype(F32) / max_exact) / math.log(MAX_DISTANCE / max_exact)
    large = jnp.minimum(max_exact + (lr * (N_BUCKETS - max_exact)).astype(I32), N_BUCKETS - 1)
    return jnp.where(d < max_exact, d, large)


def _bias_from_dist(dist, rb_ref, h):
    bucket = _t5_bucket(dist)
    out = jnp.zeros(dist.shape, F32)
    for bkt in range(N_BUCKETS):
        out = jnp.where(bucket == bkt, rb_ref[bkt, h], out)
    return out


def _bias_tab_kernel(rb_ref, ptab_ref, stab_ref, *, tq):
    r = lax.broadcasted_iota(I32, (tq, 2 * tq), 0)
    c = lax.broadcasted_iota(I32, (tq, 2 * tq), 1)
    for h in range(H_DSA):
        ptab_ref[h] = _bias_from_dist(r - c + tq, rb_ref, h)
    rows = 4 * H_DSA
    rr = lax.broadcasted_iota(I32, (rows, LANES), 0)
    pos = lax.broadcasted_iota(I32, (rows, LANES), 1)
    t = rr // H_DSA
    hh = rr % H_DSA
    dists = (jnp.full((rows, LANES), 2 * MAX_DISTANCE, I32), PAGE_SIZE + t - pos, t - pos)
    for k, dist in enumerate(dists):
        acc = jnp.zeros((rows, LANES), F32)
        for h in range(H_DSA):
            acc = jnp.where(hh == h, _bias_from_dist(dist, rb_ref, h), acc)
        stab_ref[k] = acc


def _bias_tables(rel_bias, tq):
    return pl.pallas_call(
        functools.partial(_bias_tab_kernel, tq=tq),
        out_shape=[jax.ShapeDtypeStruct((H_DSA, tq, 2 * tq), F32),
                   jax.ShapeDtypeStruct((3, 4 * H_DSA, LANES), F32)],
        in_specs=[pl.BlockSpec(memory_space=pltpu.SMEM)],
        name="bias_tables",
    )(rel_bias)


def _kth_largest_key(count_ge, shape, k):
    def step(i, cur):
        cand = cur + lax.shift_left(jnp.int32(1), jnp.int32(31) - i)
        return jnp.where(count_ge(cand) >= k, cand, cur)
    return lax.fori_loop(0, 32, step, jnp.full(shape, INT_MIN, I32))


def _dsa_kernel(rb_ref, iq_ref, ik2_ref, sm_ref, q_ref, k_ref, v_ref, g_ref, tab_ref, o_ref,
                key_s, keyt_s, wb_s, m_s, l_s, acc_s, *, tq, topk):
    qi = pl.program_id(1)
    n_pairs = H_IDX // 2
    row = lax.broadcasted_iota(I32, (tq, tq), 0)
    col = lax.broadcasted_iota(I32, (tq, tq), 1)
    lane = lax.broadcasted_iota(I32, (tq, LANES), 1)

    w = sm_ref[:, SM_IW:SM_IW + H_IDX] * (H_IDX ** -0.5)
    for h in range(H_IDX):
        wb_s[h] = jnp.broadcast_to(w[:, h:h + 1], (tq, LANES))
    iqs = jnp.concatenate([iq_ref[:, p * LANES:(p + 1) * LANES] for p in range(n_pairs)], axis=0)

    def score_chunk(j, _):
        off = pl.multiple_of(j * tq, tq)
        ik2 = ik2_ref[pl.ds(off, tq), :]
        rhs = jnp.concatenate([jnp.where(lane < D_IDX, ik2, 0), jnp.where(lane >= D_IDX, ik2, 0)], axis=0)
        s2 = _dot_nt(iqs, rhs)
        sc = jnp.zeros((tq, tq), F32)
        for p in range(n_pairs):
            for e in range(2):
                r = jnp.maximum(s2[p * tq:(p + 1) * tq, e * tq:(e + 1) * tq], 0.0)
                sc = sc + jnp.concatenate([wb_s[2 * p + e]] * (tq // LANES), axis=1) * r
        key = _sort_key(jnp.where(off + col <= qi * tq + row, sc, NEG_INF))
        key_s[j] = key
        keyt_s[j] = key.T
        return 0

    lax.fori_loop(0, qi + 1, score_chunk, 0)

    def count_ge(cand):
        def body(j, cnt):
            hit = jnp.where(keyt_s[j] >= cand, 1, 0)
            return cnt + jnp.sum(hit.reshape(tq // 8, 8, tq), axis=0)
        cnt = lax.fori_loop(0, qi + 1, body, jnp.zeros((8, tq), I32))
        return jnp.sum(cnt, axis=0, keepdims=True)

    thr_row = _kth_largest_key(count_ge, (1, tq), topk)
    thr_col = jnp.broadcast_to(thr_row, (LANES, tq)).T
    thr = jnp.concatenate([thr_col] * (tq // LANES), axis=1)

    m_s[...] = jnp.full(m_s.shape, NEG_INF, F32)
    l_s[...] = jnp.zeros(l_s.shape, F32)
    acc_s[...] = jnp.zeros(acc_s.shape, F32)

    def attend(j, mode):
        off = pl.multiple_of(j * tq, tq)
        sel = key_s[j] >= thr
        if mode == 2:
            sel = jnp.logical_and(sel, col <= row)
        for kv in range(KV_DSA):
            q2 = jnp.concatenate([q_ref[:, (2 * kv + g) * HEAD_DIM:(2 * kv + g + 1) * HEAD_DIM] for g in range(2)], axis=0)
            s = _dot_nt(q2, k_ref[pl.ds(off, tq), kv * HEAD_DIM:(kv + 1) * HEAD_DIM])
            ps, alphas = [], []
            for g in range(2):
                h = 2 * kv + g
                sg = s[g * tq:(g + 1) * tq]
                if mode == 0:
                    sg = sg + rb_ref[N_BUCKETS - 1, h]
                elif mode == 1:
                    sg = sg + tab_ref[h, :, :tq]
                else:
                    sg = sg + tab_ref[h, :, tq:]
                sg = jnp.where(sel, sg, NEG_INF)
                m_old = m_s[h]
                m_new = jnp.maximum(m_old, jnp.broadcast_to(jnp.max(sg, axis=-1, keepdims=True), (tq, LANES)))
                m_safe = jnp.where(m_new == NEG_INF, 0.0, m_new)
                alphas.append(jnp.exp(m_old - m_safe))
                ps.append(jnp.exp(sg - _tile_lanes(m_safe, tq)).astype(BF16))
                m_s[h] = m_new
            v = v_ref[pl.ds(off, tq), kv * HEAD_DIM:(kv + 1) * HEAD_DIM]
            pv = _dot(jnp.concatenate(ps, axis=0), _with_ones(v))
            for g in range(2):
                h = 2 * kv + g
                l_s[h] = alphas[g] * l_s[h] + pv[g * tq:(g + 1) * tq, HEAD_DIM:]
                acc_s[h] = alphas[g] * acc_s[h] + pv[g * tq:(g + 1) * tq, :HEAD_DIM]

    def far(j, _):
        attend(j, 0)
        return 0

    lax.fori_loop(0, jnp.maximum(qi - 1, 0), far, 0)

    @pl.when(qi >= 1)
    def _():
        attend(qi - 1, 1)

    attend(qi, 2)

    for h in range(H_DSA):
        o = acc_s[h] / l_s[h]
        o_ref[:, h * HEAD_DIM:(h + 1) * HEAD_DIM] = (o * g_ref[:, h * HEAD_DIM:(h + 1) * HEAD_DIM]).astype(BF16)


def _dsa_prompt(rel_bias, iq16, ik2, small, dq16, dk16, dv16, dg, ptab, b, t, tq, topk):
    nq = t // tq
    rows = b * t
    qmap = lambda bi, qi: (bi * nq + qi, 0)
    bmap = lambda bi, qi: (bi, 0)
    return pl.pallas_call(
        functools.partial(_dsa_kernel, tq=tq, topk=topk),
        out_shape=jax.ShapeDtypeStruct((rows, W_HALF), BF16), grid=(b, nq),
        in_specs=[
            pl.BlockSpec(memory_space=pltpu.SMEM),
            pl.BlockSpec((tq, H_IDX * D_IDX), qmap),
            pl.BlockSpec((t, LANES), bmap),
            pl.BlockSpec((tq, LANES), qmap),
            pl.BlockSpec((tq, W_HALF), qmap),
            pl.BlockSpec((t, W_KV), bmap),
            pl.BlockSpec((t, W_KV), bmap),
            pl.BlockSpec((tq, W_HALF), qmap),
            pl.BlockSpec((H_DSA, tq, 2 * tq), lambda bi, qi: (0, 0, 0), pipeline_mode=pl.Buffered(1)),
        ],
        out_specs=pl.BlockSpec((tq, W_HALF), qmap),
        scratch_shapes=[pltpu.VMEM((nq, tq, tq), I32), pltpu.VMEM((nq, tq, tq), I32),
                        pltpu.VMEM((H_IDX, tq, LANES), F32),
                        pltpu.VMEM((H_DSA, tq, LANES), F32), pltpu.VMEM((H_DSA, tq, LANES), F32),
                        pltpu.VMEM((H_DSA, tq, HEAD_DIM), F32)],
        compiler_params=_cparams(("arbitrary", "arbitrary")), name="dsa_prompt",
    )(rel_bias, iq16, ik2, small, dq16, dk16, dv16, dg, ptab)


def _page_specs(shape_tail, pps, new_step_tail):
    nd = len(shape_tail)
    return [pl.BlockSpec((1,) + shape_tail, functools.partial(
        lambda s, p, pt, i: (pt[s, p * pps + i],) + (0,) * nd, i=i)) for i in range(pps)]


def _smp_score_kernel(pt_ref, iq_ref, w_ref, *refs, pps, n_tok):
    ik_refs, iknew_ref, o_ref, onew_ref = refs[:pps], refs[pps], refs[pps + 1], refs[pps + 2]
    p = pl.program_id(1)
    iq = iq_ref[0]
    wcol = w_ref[0]

    def page_scores(ik_t):
        r = jnp.maximum(_dot(iq, ik_t.astype(BF16)), 0.0) * wcol
        return jnp.sum(r.reshape(n_tok, H_IDX, ik_t.shape[1]), axis=1)

    ik_all = jnp.concatenate([ik_refs[i][0] for i in range(pps)], axis=1)
    o_ref[0] = _sort_key(page_scores(ik_all))

    @pl.when(p == pl.num_programs(1) - 1)
    def _():
        sc = page_scores(iknew_ref[0])
        t = lax.broadcasted_iota(I32, (n_tok, LANES), 0)
        pos = lax.broadcasted_iota(I32, (n_tok, LANES), 1)
        onew_ref[0] = _sort_key(jnp.where(pos <= t, sc, NEG_INF))


def _smp_scores(page_table, iq_s, w_s, ik_pool, ik_new, pps):
    s, n_pages = page_table.shape
    n_tok = iq_s.shape[1] // H_IDX
    grid_spec = pltpu.PrefetchScalarGridSpec(
        num_scalar_prefetch=1, grid=(s, n_pages // pps),
        in_specs=[pl.BlockSpec((1, n_tok * H_IDX, D_IDX), lambda si, p, pt: (si, 0, 0)),
                  pl.BlockSpec((1, n_tok * H_IDX, 1), lambda si, p, pt: (si, 0, 0))]
                 + _page_specs((D_IDX, PAGE_SIZE), pps, None)
                 + [pl.BlockSpec((1, D_IDX, PAGE_SIZE), lambda si, p, pt: (si, 0, 0))],
        out_specs=[pl.BlockSpec((1, n_tok, pps * LANES), lambda si, p, pt: (si, 0, p)),
                   pl.BlockSpec((1, n_tok, LANES), lambda si, p, pt: (si, 0, 0))],
    )
    return pl.pallas_call(
        functools.partial(_smp_score_kernel, pps=pps, n_tok=n_tok),
        out_shape=[jax.ShapeDtypeStruct((s, n_tok, n_pages * LANES), I32),
                   jax.ShapeDtypeStruct((s, n_tok, LANES), I32)], grid_spec=grid_spec,
        compiler_params=_cparams(("arbitrary", "arbitrary")), name="smp_scores",
    )(page_table, iq_s, w_s, *([ik_pool] * pps), ik_new)


def _smp_thresh_kernel(key_ref, knew_ref, o_ref, *, topk):
    def count_ge(cand):
        hit = jnp.sum(jnp.where(key_ref[...] >= cand, 1, 0), axis=-1, keepdims=True)
        return hit + jnp.sum(jnp.where(knew_ref[...] >= cand, 1, 0), axis=-1, keepdims=True)

    thr = _kth_largest_key(count_ge, (key_ref.shape[0], 1), topk)
    o_ref[...] = jnp.broadcast_to(thr, o_ref.shape)


def _smp_thresh(keys, keys_new, topk):
    rows, n = keys.shape
    tr = _pick_tile(rows, 32)
    return pl.pallas_call(
        functools.partial(_smp_thresh_kernel, topk=topk),
        out_shape=jax.ShapeDtypeStruct((rows, LANES), I32), grid=(rows // tr,),
        in_specs=[pl.BlockSpec((tr, n), lambda i: (i, 0)), pl.BlockSpec((tr, LANES), lambda i: (i, 0))],
        out_specs=pl.BlockSpec((tr, LANES), lambda i: (i, 0)),
        compiler_params=_cparams(("arbitrary",)), name="smp_thresh",
    )(keys, keys_new)


def _smp_attn_kernel(pt_ref, q_ref, g_ref, *refs, pps, n_tok, n_kv, mode):
    k_refs, v_refs = refs[:pps], refs[pps:2 * pps]
    rest = refs[2 * pps:]
    if mode == "fox":
        lf_refs, rest = rest[:pps], rest[pps:]
        knew_ref, vnew_ref, lfnew_ref, o_ref, m_s, l_s, acc_s, carry_s = rest
    else:
        knew_ref, vnew_ref, key_ref, keynew_ref, thr_ref, tab_ref, o_ref, m_s, l_s, acc_s = rest
    p = pl.program_id(1)
    last = pl.num_programs(1) - 1
    rows = q_ref.shape[1]
    heads = rows // n_tok
    grp = heads // n_kv
    q = q_ref[0]
    rr = lax.broadcasted_iota(I32, (rows, LANES), 0)
    pos = lax.broadcasted_iota(I32, (rows, LANES), 1)
    row_kv = (rr % heads) // grp
    qcat = jnp.concatenate([jnp.where(row_kv == kv, q, jnp.zeros_like(q)) for kv in range(n_kv)], axis=1)

    def heads_of(ref, kv):
        return ref[0, pl.ds(kv, PAGE_SIZE, stride=n_kv), :].astype(BF16)

    @pl.when(p == 0)
    def _():
        m_s[...] = jnp.full(m_s.shape, NEG_INF, F32)
        l_s[...] = jnp.zeros(l_s.shape, F32)
        acc_s[...] = jnp.zeros(acc_s.shape, F32)
        if mode == "fox":
            carry_s[...] = jnp.zeros(carry_s.shape, F32)

    def attend(kv_refs, bias, sel):
        kcat = jnp.concatenate([jnp.concatenate([heads_of(k_ref, kv) for kv in range(n_kv)], axis=1)
                                for k_ref, _ in kv_refs], axis=0)
        s = _dot_nt(qcat, kcat) + bias
        if sel is not None:
            s = jnp.where(sel, s, NEG_INF)
        m_old = m_s[...]
        m_new = jnp.maximum(m_old, jnp.max(s, axis=-1, keepdims=True))
        m_safe = jnp.where(m_new == NEG_INF, 0.0, m_new)
        alpha = jnp.exp(m_old - m_safe)
        pr = jnp.exp(s - m_safe)
        l_s[...] = alpha * l_s[...] + jnp.sum(pr, axis=-1, keepdims=True)
        m_s[...] = m_new
        pcat = jnp.concatenate([jnp.where(row_kv == kv, pr[:, i * LANES:(i + 1) * LANES], 0.0).astype(BF16)
                                for i in range(len(kv_refs)) for kv in range(n_kv)], axis=1)
        vcat = jnp.concatenate([heads_of(v_ref, kv) for _, v_ref in kv_refs for kv in range(n_kv)], axis=0)
        acc_s[...] = alpha * acc_s[...] + _dot(pcat, vcat)

    def fox_bias(lf_list):
        c = carry_s[...] + _cumsum_lanes(jnp.concatenate([r[0] for r in lf_list], axis=1))
        carry_s[...] = c[:, c.shape[1] - 1:]
        return -jnp.concatenate([c] * n_tok, axis=0)

    def expand_rows(x):
        return jnp.concatenate([jnp.broadcast_to(x[t:t + 1], (heads, x.shape[1])) for t in range(n_tok)], axis=0)

    pages = list(zip(k_refs, v_refs))
    if mode == "fox":
        attend(pages, fox_bias(lf_refs), None)
    else:
        last_bias = tab_ref[jnp.where(p == last, 1, 0)]
        bias = jnp.concatenate([tab_ref[0]] * (pps - 1) + [last_bias], axis=1)
        sel = expand_rows(key_ref[0]) >= expand_rows(thr_ref[0])[:, :1]
        attend(pages, bias, sel)

    @pl.when(p == last)
    def _():
        causal = pos <= rr // heads
        if mode == "fox":
            attend([(knew_ref, vnew_ref)], fox_bias([lfnew_ref]), causal)
        else:
            sel = expand_rows(keynew_ref[0]) >= expand_rows(thr_ref[0])
            attend([(knew_ref, vnew_ref)], tab_ref[2], jnp.logical_and(sel, causal))
        o_ref[0] = ((acc_s[...] / l_s[...]) * g_ref[0]).astype(BF16)


def _smp_attn(mode, page_table, q_s, g_s, k_pool, v_pool, k_new, v_new, extra, pps):
    s, n_pages = page_table.shape
    steps = n_pages // pps
    rows = q_s.shape[1]
    n_tok = rows // H_FOX
    seq3 = lambda a, b: pl.BlockSpec((1, a, b), lambda si, p, pt: (si, 0, 0))
    in_specs = [seq3(rows, HEAD_DIM), seq3(rows, HEAD_DIM)]
    in_specs += _page_specs((PAGE_SIZE * KV_FOX, HEAD_DIM), pps, None) * 2
    args = [q_s, g_s] + [k_pool] * pps + [v_pool] * pps
    scratch = [pltpu.VMEM((rows, 1), F32), pltpu.VMEM((rows, 1), F32), pltpu.VMEM((rows, HEAD_DIM), F32)]
    if mode == "fox":
        lf_pool, lf_new = extra
        in_specs += _page_specs((H_FOX, PAGE_SIZE), pps, None)
        in_specs += [seq3(PAGE_SIZE * KV_FOX, HEAD_DIM)] * 2 + [seq3(H_FOX, PAGE_SIZE)]
        args += [lf_pool] * pps + [k_new, v_new, lf_new]
        scratch.append(pltpu.VMEM((H_FOX, 1), F32))
    else:
        keys, keys_new, thr, stab = extra
        in_specs += [seq3(PAGE_SIZE * KV_FOX, HEAD_DIM)] * 2
        in_specs += [pl.BlockSpec((1, n_tok, pps * LANES), lambda si, p, pt: (si, 0, p)),
                     seq3(n_tok, LANES), seq3(n_tok, LANES),
                     pl.BlockSpec((3, rows, LANES), lambda si, p, pt: (0, 0, 0))]
        args += [k_new, v_new, keys, keys_new, thr, stab]
    grid_spec = pltpu.PrefetchScalarGridSpec(
        num_scalar_prefetch=1, grid=(s, steps), in_specs=in_specs,
        out_specs=seq3(rows, HEAD_DIM), scratch_shapes=scratch)
    return pl.pallas_call(
        functools.partial(_smp_attn_kernel, pps=pps, n_tok=n_tok, n_kv=KV_FOX, mode=mode),
        out_shape=jax.ShapeDtypeStruct((s, rows, HEAD_DIM), BF16), grid_spec=grid_spec,
        compiler_params=_cparams(("arbitrary", "arbitrary")), name="smp_attn_" + mode,
    )(page_table, *args)


def _out_kernel(x_ref, gf_ref, gd_ref, p_ref, wo_ref, wpg_ref, wple_ref, gple_ref, o_ref):
    h = x_ref[...] + _dot(gf_ref[...], wo_ref[:W_HALF, :]) + _dot(gd_ref[...], wo_ref[W_HALF:, :])
    gate = _sigmoid(_dot(h.astype(BF16), wpg_ref[...]))
    e = _dot(p_ref[...].astype(BF16), wple_ref[...])
    e = e * lax.rsqrt(jnp.mean(e * e, axis=-1, keepdims=True) + EPS) * gple_ref[...]
    o_ref[...] = h + gate * e


def _out(x2, gf, gd, p2, wo, wpg, wple, gple, tm):
    rows = x2.shape[0]
    const = lambda shape: pl.BlockSpec(shape, lambda i: (0, 0), pipeline_mode=pl.Buffered(1))
    rmap = lambda i: (i, 0)
    return pl.pallas_call(
        _out_kernel, out_shape=jax.ShapeDtypeStruct((rows, D_MODEL), F32), grid=(rows // tm,),
        in_specs=[pl.BlockSpec((tm, D_MODEL), rmap), pl.BlockSpec((tm, W_HALF), rmap), pl.BlockSpec((tm, W_HALF), rmap),
                  pl.BlockSpec((tm, PLE_DIM), rmap),
                  const((D_MODEL, D_MODEL)), const((D_MODEL, D_MODEL)), const((PLE_DIM, D_MODEL)), const((1, D_MODEL))],
        out_specs=pl.BlockSpec((tm, D_MODEL), rmap),
        compiler_params=_cparams(("arbitrary",)), name="out",
    )(x2, gf, gd, p2, wo, wpg, wple, gple)


def _prep_w_in(w):
    points = [sum(SPLIT_SIZES[:i + 1]) for i in range(len(SPLIT_SIZES) - 1)]
    fq, fk, fv, ff, fz, dq, dk, dv, dz, iq, ik, iw = jnp.split(w, points, axis=1)
    main = jnp.concatenate([fq, fk, fv, fz, dq, dk, dv, dz, iq], axis=1).astype(BF16)
    pad = jnp.zeros((w.shape[0], LANES - D_IDX - H_IDX - H_FOX), w.dtype)
    small = jnp.concatenate([ik, iw, ff, pad], axis=1).astype(BF16)
    return main, small


def _pick_tile(n, pref):
    t = pref
    while n % t:
        t //= 2
    return t


def kernel(x_prompt, x_sample, cache_fox_k, cache_fox_v, cache_fox_logf, cache_dsa_k, cache_dsa_v, cache_idx_k,
           page_table, p_prompt, p_sample, rel_bias, norm_in, w_in, b_f, q_norm_fox, k_norm_fox, q_norm_dsa,
           k_norm_dsa, w_out, w_ple, ple_norm, w_pg):
    b, t, _ = x_prompt.shape
    s, n_tok, _ = x_sample.shape
    n_pages = page_table.shape[1]
    n_past = n_pages * PAGE_SIZE
    n_pool = cache_fox_k.shape[1]
    assert cache_fox_k.shape[0] == 1 and n_tok * H_FOX == 32

    w_main, w_small = _prep_w_in(w_in[0])
    nin = norm_in[0].reshape(1, D_MODEL)
    vec = lambda a: a[0].reshape(1, HEAD_DIM)
    gains = (vec(q_norm_fox), vec(k_norm_fox), vec(q_norm_dsa), vec(k_norm_dsa))
    bf = b_f[0].reshape(H_FOX, 1)
    wo = w_out[0].astype(BF16)
    wpg = w_pg[0].astype(BF16)
    wple = w_ple[0].astype(BF16)
    gple = ple_norm[0].reshape(1, D_MODEL)

    tq_d = _pick_tile(t, 256)
    ptab, stab = _bias_tables(rel_bias, tq_d)

    rows_p = b * t
    xp = x_prompt.reshape(rows_p, D_MODEL)
    (fq16, fk, fk16, fv, fv16, fg, dq16, dk, dk16, dv, dv16, dg, iq16, small, logft) = _proj(
        xp, nin, w_main, w_small, *gains, bf, _pick_tile(rows_p, 512))
    tq_f = _pick_tile(t, 256)
    ct = _cumsum_prompt(logft, t)
    c4 = ct.reshape(KV_FOX, 2, rows_p // tq_f, tq_f)
    gf = _fox_prompt(fq16, fk16, fv16, c4, fg, b, t, tq_f)
    ik16 = small[:, SM_IK:SM_IK + D_IDX].astype(BF16)
    ik2 = jnp.concatenate([ik16, ik16], axis=1)
    topk_p = min(TOPK_MAX, t // 4)
    gd = _dsa_prompt(rel_bias, iq16, ik2, small, dq16, dk16, dv16, dg, ptab, b, t, tq_d, topk_p)
    y_p = _out(xp, gf, gd, p_prompt[0].reshape(rows_p, PLE_DIM), wo, wpg, wple, gple, _pick_tile(rows_p, 256))

    rows_s = s * n_tok
    xs = x_sample.reshape(rows_s, D_MODEL)
    (sfq16, sfk, _, sfv, _, sfg, sdq16, sdk, _, sdv, _, sdg, siq16, ssmall, slogft) = _proj(
        xs, nin, w_main, w_small, *gains, bf, _pick_tile(rows_s, 512))
    pps = _pick_tile(n_pages, 16)
    rows_q = n_tok * H_FOX

    def new_page(a):
        a = a.reshape(s, n_tok * KV_FOX, HEAD_DIM)
        return jnp.pad(a, ((0, 0), (0, (PAGE_SIZE - n_tok) * KV_FOX), (0, 0)))

    q_rows = lambda a: a.reshape(s, rows_q, HEAD_DIM)
    pool = lambda c: c[0].reshape(n_pool, PAGE_SIZE * KV_FOX, HEAD_DIM)
    lf_pool = jnp.transpose(cache_fox_logf[0], (0, 2, 1))
    lf_new = jnp.pad(jnp.transpose(slogft.reshape(H_FOX, s, n_tok), (1, 0, 2)), ((0, 0), (0, 0), (0, PAGE_SIZE - n_tok)))
    sgf = _smp_attn("fox", page_table, q_rows(sfq16), q_rows(sfg), pool(cache_fox_k), pool(cache_fox_v),
                    new_page(sfk), new_page(sfv), (lf_pool, lf_new), pps)

    iq_s = siq16.reshape(s, n_tok * H_IDX, D_IDX)
    w_s = (ssmall[:, SM_IW:SM_IW + H_IDX] * (H_IDX ** -0.5)).reshape(s, n_tok * H_IDX, 1)
    ik_new = jnp.pad(jnp.transpose(ssmall[:, SM_IK:SM_IK + D_IDX].reshape(s, n_tok, D_IDX), (0, 2, 1)),
                     ((0, 0), (0, 0), (0, PAGE_SIZE - n_tok)))
    keys, keys_new = _smp_scores(page_table, iq_s, w_s, jnp.transpose(cache_idx_k[0], (0, 2, 1)), ik_new,
                                 _pick_tile(n_pages, 64))
    topk_s = min(TOPK_MAX, (n_past + n_tok) // 4)
    thr = _smp_thresh(keys.reshape(rows_s, n_past), keys_new.reshape(rows_s, LANES), topk_s).reshape(s, n_tok, LANES)
    sgd = _smp_attn("dsa", page_table, q_rows(sdq16), q_rows(sdg), pool(cache_dsa_k), pool(cache_dsa_v),
                    new_page(sdk), new_page(sdv), (keys, keys_new, thr, stab), pps)
    y_s = _out(xs, sgf.reshape(rows_s, W_HALF), sgd.reshape(rows_s, W_HALF), p_sample[0].reshape(rows_s, PLE_DIM),
               wo, wpg, wple, gple, _pick_tile(rows_s, 256))

    def kv5(a, bb, tt):
        return a.reshape(1, bb, tt, KV_FOX, HEAD_DIM)

    def outs(bb, tt, fk_, fv_, logft_, dk_, dv_, small_):
        return (kv5(fk_, bb, tt), kv5(fv_, bb, tt), logft_.T.reshape(1, bb, tt, H_FOX),
                kv5(dk_, bb, tt), kv5(dv_, bb, tt), small_[:, SM_IK:SM_IK + D_IDX].reshape(1, bb, tt, D_IDX))

    return ((y_p.reshape(b, t, D_MODEL), y_s.reshape(s, n_tok, D_MODEL))
            + outs(b, t, fk, fv, logft, dk, dv, small)
            + outs(s, n_tok, sfk, sfv, slogft, sdk, sdv, ssmall))
```

```python
import functools
import math

import jax
import jax.numpy as jnp
from jax import lax
from jax.experimental import pallas as pl
from jax.experimental.pallas import tpu as pltpu
from jax.experimental.pallas import tpu_sc as plsc

F32 = jnp.float32
BF16 = jnp.bfloat16
I32 = jnp.int32

D_MODEL = 2048
HEAD_DIM = 128
H_FOX = 8
KV_FOX = 4
H_DSA = 8
KV_DSA = 4
H_IDX = 16
D_IDX = 64
TOPK_MAX = 256
N_BUCKETS = 32
MAX_DISTANCE = 128
PLE_DIM = 256
PAGE_SIZE = 128
EPS = 1e-6
W_HALF = H_FOX * HEAD_DIM
W_KV = KV_FOX * HEAD_DIM
SPLIT_SIZES = (W_HALF, W_KV, W_KV, H_FOX, W_HALF, W_HALF, W_KV, W_KV, W_HALF, H_IDX * D_IDX, D_IDX, H_IDX)

LANES = 128
INT_MIN = -(2 ** 31)
NEG_INF = float("-inf")
VMEM_LIMIT = 56 * 1024 * 1024
SC_GATHER_WINDOW = 128
PT_SPLIT = 64

PROJ_TN = W_HALF
PROJ_STEPS = ("fq", "fkv", "fz", "dq", "dkv", "dz", "iq")
N_MAIN_BLOCKS = len(PROJ_STEPS)
SM_IK, SM_IW, SM_FF = 0, D_IDX, D_IDX + H_IDX


def _cparams(sem):
    return pltpu.CompilerParams(dimension_semantics=sem, vmem_limit_bytes=VMEM_LIMIT)


def _dot_nt(a, b):
    return lax.dot_general(a, b, (((1,), (1,)), ((), ())), preferred_element_type=F32)


def _dot(a, b):
    return jnp.dot(a, b, preferred_element_type=F32)


def _log_sigmoid(x):
    return -(jnp.maximum(-x, 0.0) + jnp.log(1.0 + jnp.exp(-jnp.abs(x))))


def _sigmoid(x):
    return 1.0 / (1.0 + jnp.exp(-x))


def _sort_key(x):
    b = pltpu.bitcast(x, I32)
    return b ^ ((b >> 31) & jnp.int32(0x7FFFFFFF))


def _tile_lanes(x, width):
    return x if width == LANES else jnp.concatenate([x] * (width // LANES), axis=1)


def _with_ones(v):
    return jnp.concatenate([v, jnp.ones_like(v)], axis=1)


def _cumsum_lanes(x):
    n = x.shape[-1]
    lane = lax.broadcasted_iota(I32, x.shape, x.ndim - 1)
    k = 1
    while k < n:
        x = x + jnp.where(lane >= k, pltpu.roll(x, k, axis=x.ndim - 1), 0.0)
        k *= 2
    return x


def _proj_kernel(x_ref, nin_ref, w_ref, ws_ref, gqf_ref, gkf_ref, gqd_ref, gkd_ref, bf_ref,
                 fq_o, fk_o, fk16_o, fv_o, fv16_o, fg_o, dq_o, dk_o, dk16_o, dv_o, dv16_o, dg_o,
                 iq_o, small_o, logft_o, xn_s):
    j = pl.program_id(1)

    @pl.when(j == 0)
    def _():
        x = x_ref[...]
        ms = jnp.mean(x * x, axis=-1, keepdims=True)
        xn_s[...] = (x * lax.rsqrt(ms + EPS) * nin_ref[...]).astype(BF16)

    def main():
        return _dot(xn_s[...], w_ref[...])

    def head_norm(y, g_ref, scale):
        outs = []
        for c in range(y.shape[1] // HEAD_DIM):
            yh = y[:, c * HEAD_DIM:(c + 1) * HEAD_DIM]
            n = yh * lax.rsqrt(jnp.mean(yh * yh, axis=-1, keepdims=True) + EPS) * g_ref[...]
            outs.append(n * scale if scale != 1.0 else n)
        return jnp.concatenate(outs, axis=1)

    def store_kv(o_ref, y):
        for kv in range(KV_FOX):
            o_ref[pl.ds(kv, y.shape[0], stride=KV_FOX), :] = y[:, kv * HEAD_DIM:(kv + 1) * HEAD_DIM]

    def kv_step(gk_ref, k_o, k16_o, v_o, v16_o):
        y = main()
        n = head_norm(y[:, :W_KV], gk_ref, 1.0)
        store_kv(k_o, n)
        k16_o[...] = n.astype(BF16)
        store_kv(v_o, y[:, W_KV:])
        v16_o[...] = y[:, W_KV:].astype(BF16)

    def gate_step(o_ref):
        y = main()
        o_ref[...] = (y * _sigmoid(y)).astype(BF16)

    @pl.when(j == PROJ_STEPS.index("fq"))
    def _():
        fq_o[...] = head_norm(main(), gqf_ref, HEAD_DIM ** -0.5).astype(BF16)

    @pl.when(j == PROJ_STEPS.index("fkv"))
    def _():
        kv_step(gkf_ref, fk_o, fk16_o, fv_o, fv16_o)

    @pl.when(j == PROJ_STEPS.index("fz"))
    def _():
        gate_step(fg_o)

    @pl.when(j == PROJ_STEPS.index("dq"))
    def _():
        dq_o[...] = head_norm(main(), gqd_ref, HEAD_DIM ** -0.5).astype(BF16)

    @pl.when(j == PROJ_STEPS.index("dkv"))
    def _():
        kv_step(gkd_ref, dk_o, dk16_o, dv_o, dv16_o)

    @pl.when(j == PROJ_STEPS.index("dz"))
    def _():
        gate_step(dg_o)

    @pl.when(j == PROJ_STEPS.index("iq"))
    def _():
        iq_o[...] = (main() * (D_IDX ** -0.5)).astype(BF16)

    @pl.when(j == N_MAIN_BLOCKS)
    def _():
        ys = _dot(xn_s[...], ws_ref[...])
        small_o[...] = ys
        ff_t = ys.T[SM_FF:SM_FF + H_FOX, :]
        logft_o[...] = _log_sigmoid(ff_t + bf_ref[...])


def _proj(x2, nin, w_main, w_small, gqf, gkf, gqd, gkd, bf, tm):
    rows = x2.shape[0]
    assert rows % tm == 0
    grid = (rows // tm, N_MAIN_BLOCKS + 1)

    rows_of = lambda width: pl.BlockSpec((tm, width), lambda i, j: (i, 0))
    row_vec = lambda n: pl.BlockSpec((1, n), lambda i, j: (0, 0))
    kv_rows = pl.BlockSpec((tm * KV_FOX, HEAD_DIM), lambda i, j: (i, 0))
    in_specs = [
        pl.BlockSpec((tm, D_MODEL), lambda i, j: (i, 0)),
        row_vec(D_MODEL),
        pl.BlockSpec((D_MODEL, PROJ_TN), lambda i, j: (0, jnp.minimum(j, N_MAIN_BLOCKS - 1))),
        pl.BlockSpec((D_MODEL, LANES), lambda i, j: (0, 0)),
        row_vec(HEAD_DIM), row_vec(HEAD_DIM), row_vec(HEAD_DIM), row_vec(HEAD_DIM),
        pl.BlockSpec((H_FOX, 1), lambda i, j: (0, 0)),
    ]
    sds = jax.ShapeDtypeStruct
    out_shape = [
        sds((rows, W_HALF), BF16),
        sds((rows * KV_FOX, HEAD_DIM), F32), sds((rows, W_KV), BF16),
        sds((rows * KV_FOX, HEAD_DIM), F32), sds((rows, W_KV), BF16),
        sds((rows, W_HALF), BF16),
        sds((rows, W_HALF), BF16),
        sds((rows * KV_FOX, HEAD_DIM), F32), sds((rows, W_KV), BF16),
        sds((rows * KV_FOX, HEAD_DIM), F32), sds((rows, W_KV), BF16),
        sds((rows, W_HALF), BF16),
        sds((rows, W_HALF), BF16),
        sds((rows, LANES), F32),
        sds((H_FOX, rows), F32),
    ]
    out_specs = [
        rows_of(W_HALF),
        kv_rows, rows_of(W_KV),
        kv_rows, rows_of(W_KV),
        rows_of(W_HALF),
        rows_of(W_HALF),
        kv_rows, rows_of(W_KV),
        kv_rows, rows_of(W_KV),
        rows_of(W_HALF),
        rows_of(W_HALF),
        rows_of(LANES),
        pl.BlockSpec((H_FOX, tm), lambda i, j: (0, i)),
    ]
    return pl.pallas_call(
        _proj_kernel, out_shape=out_shape, grid=grid, in_specs=in_specs, out_specs=out_specs,
        scratch_shapes=[pltpu.VMEM((tm, D_MODEL), BF16)],
        compiler_params=_cparams(("arbitrary", "arbitrary")), name="proj",
    )(x2, nin, w_main, w_small, gqf, gkf, gqd, gkd, bf)


def _cumsum_kernel(x_ref, o_ref):
    o_ref[...] = _cumsum_lanes(x_ref[...])


def _cumsum_prompt(logft, t):
    rows = logft.shape[1]
    return pl.pallas_call(
        _cumsum_kernel, out_shape=jax.ShapeDtypeStruct(logft.shape, F32), grid=(rows // t,),
        in_specs=[pl.BlockSpec((H_FOX, t), lambda b: (0, b))],
        out_specs=pl.BlockSpec((H_FOX, t), lambda b: (0, b)),
        compiler_params=_cparams(("arbitrary",)), name="cumsum",
    )(logft)


def _fox_kernel(q_ref, k_ref, v_ref, c_ref, g_ref, o_ref, m_s, l_s, acc_s, *, tq):
    qi = pl.program_id(1)
    row = lax.broadcasted_iota(I32, (tq, tq), 0)
    col = lax.broadcasted_iota(I32, (tq, tq), 1)
    m_s[...] = jnp.full(m_s.shape, NEG_INF, F32)
    l_s[...] = jnp.zeros(l_s.shape, F32)
    acc_s[...] = jnp.zeros(acc_s.shape, F32)

    def chunk(j, diag):
        off = pl.multiple_of(j * tq, tq)
        for kv in range(KV_FOX):
            q2 = jnp.concatenate([q_ref[:, (2 * kv + g) * HEAD_DIM:(2 * kv + g + 1) * HEAD_DIM] for g in range(2)], axis=0)
            s = _dot_nt(q2, k_ref[pl.ds(off, tq), kv * HEAD_DIM:(kv + 1) * HEAD_DIM])
            ps, alphas = [], []
            for g in range(2):
                h = 2 * kv + g
                sg = s[g * tq:(g + 1) * tq] - c_ref[kv, g, pl.ds(j, 1), :]
                if diag:
                    sg = jnp.where(col <= row, sg, NEG_INF)
                m_old = m_s[h]
                m_new = jnp.maximum(m_old, jnp.broadcast_to(jnp.max(sg, axis=-1, keepdims=True), (tq, LANES)))
                alphas.append(jnp.exp(m_old - m_new))
                ps.append(jnp.exp(sg - _tile_lanes(m_new, tq)).astype(BF16))
                m_s[h] = m_new
            v = v_ref[pl.ds(off, tq), kv * HEAD_DIM:(kv + 1) * HEAD_DIM]
            pv = _dot(jnp.concatenate(ps, axis=0), _with_ones(v))
            for g in range(2):
                h = 2 * kv + g
                l_s[h] = alphas[g] * l_s[h] + pv[g * tq:(g + 1) * tq, HEAD_DIM:]
                acc_s[h] = alphas[g] * acc_s[h] + pv[g * tq:(g + 1) * tq, :HEAD_DIM]

    def off_diag(j, _):
        chunk(j, False)
        return 0

    lax.fori_loop(0, qi, off_diag, 0)
    chunk(qi, True)
    for h in range(H_FOX):
        o = acc_s[h] / l_s[h]
        o_ref[:, h * HEAD_DIM:(h + 1) * HEAD_DIM] = (o * g_ref[:, h * HEAD_DIM:(h + 1) * HEAD_DIM]).astype(BF16)


def _fox_prompt(fq16, fk16, fv16, c4, fg, b, t, tq):
    nq = t // tq
    qmap = lambda bi, qi: (bi * nq + qi, 0)
    bmap = lambda bi, qi: (bi, 0)
    return pl.pallas_call(
        functools.partial(_fox_kernel, tq=tq),
        out_shape=jax.ShapeDtypeStruct((b * t, W_HALF), BF16), grid=(b, nq),
        in_specs=[
            pl.BlockSpec((tq, W_HALF), qmap),
            pl.BlockSpec((t, W_KV), bmap),
            pl.BlockSpec((t, W_KV), bmap),
            pl.BlockSpec((KV_FOX, 2, nq, tq), lambda bi, qi: (0, 0, bi, 0)),
            pl.BlockSpec((tq, W_HALF), qmap),
        ],
        out_specs=pl.BlockSpec((tq, W_HALF), qmap),
        scratch_shapes=[pltpu.VMEM((H_FOX, tq, LANES), F32), pltpu.VMEM((H_FOX, tq, LANES), F32),
                        pltpu.VMEM((H_FOX, tq, HEAD_DIM), F32)],
        compiler_params=_cparams(("arbitrary", "arbitrary")), name="fox_prompt",
    )(fq16, fk16, fv16, c4, fg)


def _t5_bucket(d):
    max_exact = N_BUCKETS // 2
    d = jnp.maximum(d, 0)
    lr = jnp.log(jnp.maximum(d, 1).astype(F32) / max_exact) / math.log(MAX_DISTANCE / max_exact)
    large = jnp.minimum(max_exact + (lr * (N_BUCKETS - max_exact)).astype(I32), N_BUCKETS - 1)
    return jnp.where(d < max_exact, d, large)


def _bias_from_dist(dist, rb_ref, h):
    bucket = _t5_bucket(dist)
    out = jnp.zeros(dist.shape, F32)
    for bkt in range(N_BUCKETS):
        out = jnp.where(bucket == bkt, rb_ref[bkt, h], out)
    return out


def _bias_tab_kernel(rb_ref, ptab_ref, stab_ref, *, tq):
    r = lax.broadcasted_iota(I32, (tq, 2 * tq), 0)
    c = lax.broadcasted_iota(I32, (tq, 2 * tq), 1)
    for h in range(H_DSA):
        ptab_ref[h] = _bias_from_dist(r - c + tq, rb_ref, h)
    rows = 4 * H_DSA
    rr = lax.broadcasted_iota(I32, (rows, LANES), 0)
    pos = lax.broadcasted_iota(I32, (rows, LANES), 1)
    t = rr // H_DSA
    hh = rr % H_DSA
    dists = (jnp.full((rows, LANES), 2 * MAX_DISTANCE, I32), PAGE_SIZE + t - pos, t - pos)
    for k, dist in enumerate(dists):
        acc = jnp.zeros((rows, LANES), F32)
        for h in range(H_DSA):
            acc = jnp.where(hh == h, _bias_from_dist(dist, rb_ref, h), acc)
        stab_ref[k] = acc


def _bias_tables(rel_bias, tq):
    return pl.pallas_call(
        functools.partial(_bias_tab_kernel, tq=tq),
        out_shape=[jax.ShapeDtypeStruct((H_DSA, tq, 2 * tq), F32),
                   jax.ShapeDtypeStruct((3, 4 * H_DSA, LANES), F32)],
        in_specs=[pl.BlockSpec(memory_space=pltpu.SMEM)],
        name="bias_tables",
    )(rel_bias)


def _kth_largest_key(count_ge, shape, k):
    def step(i, cur):
        cand = cur + lax.shift_left(jnp.int32(1), jnp.int32(31) - i)
        return jnp.where(count_ge(cand) >= k, cand, cur)
    return lax.fori_loop(0, 32, step, jnp.full(shape, INT_MIN, I32))


def _dsa_kernel(rb_ref, iq_ref, ik2_ref, sm_ref, q_ref, k_ref, v_ref, g_ref, tab_ref, o_ref,
                key_s, keyt_s, wb_s, m_s, l_s, acc_s, *, tq, topk):
    qi = pl.program_id(1)
    n_pairs = H_IDX // 2
    row = lax.broadcasted_iota(I32, (tq, tq), 0)
    col = lax.broadcasted_iota(I32, (tq, tq), 1)
    lane = lax.broadcasted_iota(I32, (tq, LANES), 1)

    w = sm_ref[:, SM_IW:SM_IW + H_IDX] * (H_IDX ** -0.5)
    for h in range(H_IDX):
        wb_s[h] = jnp.broadcast_to(w[:, h:h + 1], (tq, LANES))
    iqs = jnp.concatenate([iq_ref[:, p * LANES:(p + 1) * LANES] for p in range(n_pairs)], axis=0)

    def score_chunk(j, _):
        off = pl.multiple_of(j * tq, tq)
        ik2 = ik2_ref[pl.ds(off, tq), :]
        rhs = jnp.concatenate([jnp.where(lane < D_IDX, ik2, 0), jnp.where(lane >= D_IDX, ik2, 0)], axis=0)
        s2 = _dot_nt(iqs, rhs)
        sc = jnp.zeros((tq, tq), F32)
        for p in range(n_pairs):
            for e in range(2):
                r = jnp.maximum(s2[p * tq:(p + 1) * tq, e * tq:(e + 1) * tq], 0.0)
                sc = sc + jnp.concatenate([wb_s[2 * p + e]] * (tq // LANES), axis=1) * r
        key = _sort_key(jnp.where(off + col <= qi * tq + row, sc, NEG_INF))
        key_s[j] = key
        keyt_s[j] = key.T
        return 0

    lax.fori_loop(0, qi + 1, score_chunk, 0)

    def count_ge(cand):
        def body(j, cnt):
            hit = jnp.where(keyt_s[j] >= cand, 1, 0)
            return cnt + jnp.sum(hit.reshape(tq // 8, 8, tq), axis=0)
        cnt = lax.fori_loop(0, qi + 1, body, jnp.zeros((8, tq), I32))
        return jnp.sum(cnt, axis=0, keepdims=True)

    thr_row = _kth_largest_key(count_ge, (1, tq), topk)
    thr_col = jnp.broadcast_to(thr_row, (LANES, tq)).T
    thr = jnp.concatenate([thr_col] * (tq // LANES), axis=1)

    m_s[...] = jnp.full(m_s.shape, NEG_INF, F32)
    l_s[...] = jnp.zeros(l_s.shape, F32)
    acc_s[...] = jnp.zeros(acc_s.shape, F32)

    def attend(j, mode):
        off = pl.multiple_of(j * tq, tq)
        sel = key_s[j] >= thr
        if mode == 2:
            sel = jnp.logical_and(sel, col <= row)
        for kv in range(KV_DSA):
            q2 = jnp.concatenate([q_ref[:, (2 * kv + g) * HEAD_DIM:(2 * kv + g + 1) * HEAD_DIM] for g in range(2)], axis=0)
            s = _dot_nt(q2, k_ref[pl.ds(off, tq), kv * HEAD_DIM:(kv + 1) * HEAD_DIM])
            ps, alphas = [], []
            for g in range(2):
                h = 2 * kv + g
                sg = s[g * tq:(g + 1) * tq]
                if mode == 0:
                    sg = sg + rb_ref[N_BUCKETS - 1, h]
                elif mode == 1:
                    sg = sg + tab_ref[h, :, :tq]
                else:
                    sg = sg + tab_ref[h, :, tq:]
                sg = jnp.where(sel, sg, NEG_INF)
                m_old = m_s[h]
                m_new = jnp.maximum(m_old, jnp.broadcast_to(jnp.max(sg, axis=-1, keepdims=True), (tq, LANES)))
                m_safe = jnp.where(m_new == NEG_INF, 0.0, m_new)
                alphas.append(jnp.exp(m_old - m_safe))
                ps.append(jnp.exp(sg - _tile_lanes(m_safe, tq)).astype(BF16))
                m_s[h] = m_new
            v = v_ref[pl.ds(off, tq), kv * HEAD_DIM:(kv + 1) * HEAD_DIM]
            pv = _dot(jnp.concatenate(ps, axis=0), _with_ones(v))
            for g in range(2):
                h = 2 * kv + g
                l_s[h] = alphas[g] * l_s[h] + pv[g * tq:(g + 1) * tq, HEAD_DIM:]
                acc_s[h] = alphas[g] * acc_s[h] + pv[g * tq:(g + 1) * tq, :HEAD_DIM]

    def far(j, _):
        attend(j, 0)
        return 0

    lax.fori_loop(0, jnp.maximum(qi - 1, 0), far, 0)

    @pl.when(qi >= 1)
    def _():
        attend(qi - 1, 1)

    attend(qi, 2)

    for h in range(H_DSA):
        o = acc_s[h] / l_s[h]
        o_ref[:, h * HEAD_DIM:(h + 1) * HEAD_DIM] = (o * g_ref[:, h * HEAD_DIM:(h + 1) * HEAD_DIM]).astype(BF16)


def _dsa_prompt(rel_bias, iq16, ik2, small, dq16, dk16, dv16, dg, ptab, b, t, tq, topk):
    nq = t // tq
    rows = b * t
    qmap = lambda bi, qi: (bi * nq + qi, 0)
    bmap = lambda bi, qi: (bi, 0)
    return pl.pallas_call(
        functools.partial(_dsa_kernel, tq=tq, topk=topk),
        out_shape=jax.ShapeDtypeStruct((rows, W_HALF), BF16), grid=(b, nq),
        in_specs=[
            pl.BlockSpec(memory_space=pltpu.SMEM),
            pl.BlockSpec((tq, H_IDX * D_IDX), qmap),
            pl.BlockSpec((t, LANES), bmap),
            pl.BlockSpec((tq, LANES), qmap),
            pl.BlockSpec((tq, W_HALF), qmap),
            pl.BlockSpec((t, W_KV), bmap),
            pl.BlockSpec((t, W_KV), bmap),
            pl.BlockSpec((tq, W_HALF), qmap),
            pl.BlockSpec((H_DSA, tq, 2 * tq), lambda bi, qi: (0, 0, 0), pipeline_mode=pl.Buffered(1)),
        ],
        out_specs=pl.BlockSpec((tq, W_HALF), qmap),
        scratch_shapes=[pltpu.VMEM((nq, tq, tq), I32), pltpu.VMEM((nq, tq, tq), I32),
                        pltpu.VMEM((H_IDX, tq, LANES), F32),
                        pltpu.VMEM((H_DSA, tq, LANES), F32), pltpu.VMEM((H_DSA, tq, LANES), F32),
                        pltpu.VMEM((H_DSA, tq, HEAD_DIM), F32)],
        compiler_params=_cparams(("arbitrary", "arbitrary")), name="dsa_prompt",
    )(rel_bias, iq16, ik2, small, dq16, dk16, dv16, dg, ptab)


def _page_specs(shape_tail, pps, new_step_tail):
    nd = len(shape_tail)
    return [pl.BlockSpec((1,) + shape_tail, functools.partial(
        lambda s, p, pt, i: (pt[s, p * pps + i],) + (0,) * nd, i=i)) for i in range(pps)]


def _smp_score_kernel(pt_ref, iq_ref, w_ref, *refs, pps, n_tok):
    ik_refs, iknew_ref, o_ref, onew_ref = refs[:pps], refs[pps], refs[pps + 1], refs[pps + 2]
    p = pl.program_id(1)
    iq = iq_ref[0]
    wcol = w_ref[0]

    def page_scores(ik_t):
        r = jnp.maximum(_dot(iq, ik_t.astype(BF16)), 0.0) * wcol
        return jnp.sum(r.reshape(n_tok, H_IDX, ik_t.shape[1]), axis=1)

    ik_all = jnp.concatenate([ik_refs[i][0] for i in range(pps)], axis=1)
    o_ref[0] = _sort_key(page_scores(ik_all))

    @pl.when(p == pl.num_programs(1) - 1)
    def _():
        sc = page_scores(iknew_ref[0])
        t = lax.broadcasted_iota(I32, (n_tok, LANES), 0)
        pos = lax.broadcasted_iota(I32, (n_tok, LANES), 1)
        onew_ref[0] = _sort_key(jnp.where(pos <= t, sc, NEG_INF))


def _smp_scores(page_table, iq_s, w_s, ik_pool, ik_new, pps):
    s, n_pages = page_table.shape
    n_tok = iq_s.shape[1] // H_IDX
    grid_spec = pltpu.PrefetchScalarGridSpec(
        num_scalar_prefetch=1, grid=(s, n_pages // pps),
        in_specs=[pl.BlockSpec((1, n_tok * H_IDX, D_IDX), lambda si, p, pt: (si, 0, 0)),
                  pl.BlockSpec((1, n_tok * H_IDX, 1), lambda si, p, pt: (si, 0, 0))]
                 + _page_specs((D_IDX, PAGE_SIZE), pps, None)
                 + [pl.BlockSpec((1, D_IDX, PAGE_SIZE), lambda si, p, pt: (si, 0, 0))],
        out_specs=[pl.BlockSpec((1, n_tok, pps * LANES), lambda si, p, pt: (si, 0, p)),
                   pl.BlockSpec((1, n_tok, LANES), lambda si, p, pt: (si, 0, 0))],
    )
    return pl.pallas_call(
        functools.partial(_smp_score_kernel, pps=pps, n_tok=n_tok),
        out_shape=[jax.ShapeDtypeStruct((s, n_tok, n_pages * LANES), I32),
                   jax.ShapeDtypeStruct((s, n_tok, LANES), I32)], grid_spec=grid_spec,
        compiler_params=_cparams(("arbitrary", "arbitrary")), name="smp_scores",
    )(page_table, iq_s, w_s, *([ik_pool] * pps), ik_new)


def _smp_thresh_kernel(key_ref, knew_ref, o_ref, *, topk):
    def count_ge(cand):
        hit = jnp.sum(jnp.where(key_ref[...] >= cand, 1, 0), axis=-1, keepdims=True)
        return hit + jnp.sum(jnp.where(knew_ref[...] >= cand, 1, 0), axis=-1, keepdims=True)

    thr = _kth_largest_key(count_ge, (key_ref.shape[0], 1), topk)
    o_ref[...] = jnp.broadcast_to(thr, o_ref.shape)


def _smp_thresh(keys, keys_new, topk):
    rows, n = keys.shape
    tr = _pick_tile(rows, 32)
    return pl.pallas_call(
        functools.partial(_smp_thresh_kernel, topk=topk),
        out_shape=jax.ShapeDtypeStruct((rows, LANES), I32), grid=(rows // tr,),
        in_specs=[pl.BlockSpec((tr, n), lambda i: (i, 0)), pl.BlockSpec((tr, LANES), lambda i: (i, 0))],
        out_specs=pl.BlockSpec((tr, LANES), lambda i: (i, 0)),
        compiler_params=_cparams(("arbitrary",)), name="smp_thresh",
    )(keys, keys_new)


def _smp_fox_kernel(pt_ref, q_ref, g_ref, *refs, pps, n_tok, n_kv):
    k_refs, v_refs, lf_refs = refs[:pps], refs[pps:2 * pps], refs[2 * pps:3 * pps]
    knew_ref, vnew_ref, lfnew_ref, o_ref, m_s, l_s, acc_s, carry_s = refs[3 * pps:]
    p = pl.program_id(1)
    last = pl.num_programs(1) - 1
    rows = q_ref.shape[1]
    heads = rows // n_tok
    grp = heads // n_kv
    q = q_ref[0]
    rr = lax.broadcasted_iota(I32, (rows, LANES), 0)
    pos = lax.broadcasted_iota(I32, (rows, LANES), 1)
    row_kv = (rr % heads) // grp
    qcat = jnp.concatenate([jnp.where(row_kv == kv, q, jnp.zeros_like(q)) for kv in range(n_kv)], axis=1)

    def heads_of(ref, kv):
        return ref[0, pl.ds(kv, PAGE_SIZE, stride=n_kv), :].astype(BF16)

    @pl.when(p == 0)
    def _():
        m_s[...] = jnp.full(m_s.shape, NEG_INF, F32)
        l_s[...] = jnp.zeros(l_s.shape, F32)
        acc_s[...] = jnp.zeros(acc_s.shape, F32)
        carry_s[...] = jnp.zeros(carry_s.shape, F32)

    def attend(kv_refs, bias, sel):
        kcat = jnp.concatenate([jnp.concatenate([heads_of(k_ref, kv) for kv in range(n_kv)], axis=1)
                                for k_ref, _ in kv_refs], axis=0)
        s = _dot_nt(qcat, kcat) + bias
        if sel is not None:
            s = jnp.where(sel, s, NEG_INF)
        m_old = m_s[...]
        m_new = jnp.maximum(m_old, jnp.max(s, axis=-1, keepdims=True))
        m_safe = jnp.where(m_new == NEG_INF, 0.0, m_new)
        alpha = jnp.exp(m_old - m_safe)
        pr = jnp.exp(s - m_safe)
        l_s[...] = alpha * l_s[...] + jnp.sum(pr, axis=-1, keepdims=True)
        m_s[...] = m_new
        pcat = jnp.concatenate([jnp.where(row_kv == kv, pr[:, i * LANES:(i + 1) * LANES], 0.0).astype(BF16)
                                for i in range(len(kv_refs)) for kv in range(n_kv)], axis=1)
        vcat = jnp.concatenate([heads_of(v_ref, kv) for _, v_ref in kv_refs for kv in range(n_kv)], axis=0)
        acc_s[...] = alpha * acc_s[...] + _dot(pcat, vcat)

    def fox_bias(lf_list):
        c = carry_s[...] + _cumsum_lanes(jnp.concatenate([r[0] for r in lf_list], axis=1))
        carry_s[...] = c[:, c.shape[1] - 1:]
        return -jnp.concatenate([c] * n_tok, axis=0)

    attend(list(zip(k_refs, v_refs)), fox_bias(lf_refs), None)

    @pl.when(p == last)
    def _():
        attend([(knew_ref, vnew_ref)], fox_bias([lfnew_ref]), pos <= rr // heads)
        o_ref[0] = ((acc_s[...] / l_s[...]) * g_ref[0]).astype(BF16)


def _smp_fox(page_table, q_s, g_s, k_pool, v_pool, lf_pool, k_new, v_new, lf_new, pps):
    s, n_pages = page_table.shape
    rows = q_s.shape[1]
    seq3 = lambda a, b: pl.BlockSpec((1, a, b), lambda si, p, pt: (si, 0, 0))
    in_specs = ([seq3(rows, HEAD_DIM), seq3(rows, HEAD_DIM)]
                + _page_specs((PAGE_SIZE * KV_FOX, HEAD_DIM), pps, None) * 2
                + _page_specs((H_FOX, PAGE_SIZE), pps, None)
                + [seq3(PAGE_SIZE * KV_FOX, HEAD_DIM)] * 2 + [seq3(H_FOX, PAGE_SIZE)])
    grid_spec = pltpu.PrefetchScalarGridSpec(
        num_scalar_prefetch=1, grid=(s, n_pages // pps), in_specs=in_specs, out_specs=seq3(rows, HEAD_DIM),
        scratch_shapes=[pltpu.VMEM((rows, 1), F32), pltpu.VMEM((rows, 1), F32), pltpu.VMEM((rows, HEAD_DIM), F32),
                        pltpu.VMEM((H_FOX, 1), F32)])
    return pl.pallas_call(
        functools.partial(_smp_fox_kernel, pps=pps, n_tok=rows // H_FOX, n_kv=KV_FOX),
        out_shape=jax.ShapeDtypeStruct((s, rows, HEAD_DIM), BF16), grid_spec=grid_spec,
        compiler_params=_cparams(("arbitrary", "arbitrary")), name="smp_attn_fox",
    )(page_table, q_s, g_s, *([k_pool] * pps), *([v_pool] * pps), *([lf_pool] * pps), k_new, v_new, lf_new)


def _smp_compact_kernel(key_ref, thr_ref, pthi_ref, ptlo_ref, phys_o, pos_o, *, n_tok, topk):
    n_pg = key_ref.shape[2]
    reps = topk // LANES
    pg = lax.broadcasted_iota(I32, (n_pg, LANES), 0).astype(BF16)
    strict_lower = (lax.broadcasted_iota(I32, (n_pg, n_pg), 1) < lax.broadcasted_iota(I32, (n_pg, n_pg), 0))
    ltri = jnp.where(strict_lower, 1.0, 0.0).astype(BF16)
    in_page = lax.broadcasted_iota(I32, (LANES, LANES), 0).astype(BF16)
    slot_p = lax.broadcasted_iota(I32, (topk, n_pg), 0).astype(F32)
    slot_l = lax.broadcasted_iota(I32, (topk, LANES), 0).astype(F32)
    rows_t = lambda x: jnp.concatenate([x.T] * reps, axis=0)

    for t in range(n_tok):
        hit = jnp.where(key_ref[0, t] >= thr_ref[0, t:t + 1, :], 1.0, 0.0)
        lr = _cumsum_lanes(hit)
        cnt = jnp.broadcast_to(lr[:, LANES - 1:], (n_pg, LANES))
        off = _dot(ltri, cnt.astype(BF16))
        total = off[n_pg - 1:, :] + cnt[n_pg - 1:, :]
        off_t, cnt_t = rows_t(off), rows_t(cnt)
        owner = jnp.where(jnp.logical_and(off_t <= slot_p, slot_p < off_t + cnt_t), 1.0, 0.0).astype(BF16)
        to_slot = lambda x: _dot(owner, x.astype(BF16))
        rank_in_page = slot_l - to_slot(off) + 1.0
        pick = jnp.logical_and(to_slot(hit) > 0.5, to_slot(lr) == rank_in_page)
        pos_in_page = _dot(jnp.where(pick, 1.0, 0.0).astype(BF16), in_page)
        valid = slot_l < total
        pos = to_slot(pg) * PAGE_SIZE + pos_in_page
        phys = (_dot(owner, pthi_ref[0]) * PT_SPLIT + _dot(owner, ptlo_ref[0])) * PAGE_SIZE + pos_in_page
        phys_o[0, t] = jnp.where(valid, phys, 0.0).astype(I32)[:, :1]
        pos_o[0, t] = jnp.where(valid, pos, -1.0).astype(I32)[:, :1]


def _smp_compact(keys4, thr, pt_hi, pt_lo, topk):
    s, n_tok, n_pg, _ = keys4.shape
    seq = lambda *tail: pl.BlockSpec((1,) + tail, lambda si: (si,) + (0,) * len(tail))
    out = jax.ShapeDtypeStruct((s, n_tok, topk, 1), I32)
    return pl.pallas_call(
        functools.partial(_smp_compact_kernel, n_tok=n_tok, topk=topk), out_shape=[out, out], grid=(s,),
        in_specs=[seq(n_tok, n_pg, LANES), seq(n_tok, LANES), seq(n_pg, LANES), seq(n_pg, LANES)],
        out_specs=[seq(n_tok, topk, 1), seq(n_tok, topk, 1)],
        compiler_params=_cparams(("arbitrary",)), name="smp_compact",
    )(keys4, thr, pt_hi, pt_lo)


def _sc_gather_rows(table, idx):
    n, d = idx.shape[0], table.shape[1]
    half = n // SC_GATHER_WINDOW // 2
    mesh = plsc.VectorSubcoreMesh(core_axis_name="core", subcore_axis_name="subcore")

    @functools.partial(pl.kernel, out_type=jax.ShapeDtypeStruct((n, d), table.dtype), mesh=mesh)
    def gather(x_hbm, i_hbm, o_hbm):
        def body(i_vmem, o_vmem):
            pltpu.sync_copy(x_hbm.at[i_vmem.at[0]], o_vmem)

        pltpu.emit_pipeline(
            body, grid=(2, half),
            in_specs=[pl.BlockSpec((1, SC_GATHER_WINDOW), index_map=lambda c, i: (0, c * half + i))],
            out_specs=[pl.BlockSpec((SC_GATHER_WINDOW, d), index_map=lambda c, i: (c * half + i, 0))],
            core_axis_name=("core", "subcore"), dimension_semantics=(pltpu.PARALLEL, pltpu.PARALLEL),
        )(i_hbm, o_hbm)

    return gather(table, idx.reshape(1, n))


def _smp_attn_g_kernel(rb_ref, q_ref, g_ref, kg_ref, vg_ref, pos_ref, knew_ref, vnew_ref, keynew_ref, thr_ref,
                       tab_ref, o_ref, *, n_tok, n_kv, n_past):
    heads = q_ref.shape[1] // n_tok
    grp = heads // n_kv
    topk = pos_ref.shape[2]
    row_kv = lax.broadcasted_iota(I32, (heads, LANES), 0) // grp
    lane = lax.broadcasted_iota(I32, (1, LANES), 1)

    def by_kv(x):
        return jnp.concatenate([jnp.where(row_kv[:, :1] == kv, x, jnp.zeros_like(x)) for kv in range(n_kv)], axis=1)

    def new_rows(ref, kv):
        return ref[0, pl.ds(kv, PAGE_SIZE, stride=n_kv), :].astype(BF16)

    knew = jnp.concatenate([new_rows(knew_ref, kv) for kv in range(n_kv)], axis=1)
    vnew = jnp.concatenate([new_rows(vnew_ref, kv) for kv in range(n_kv)], axis=0)

    for t in range(n_tok):
        rows = slice(t * heads, (t + 1) * heads)
        blk = lambda ref, kv: ref[(t * n_kv + kv) * topk:(t * n_kv + kv + 1) * topk, :].astype(BF16)
        qcat = by_kv(q_ref[0, rows, :])
        pos = pos_ref[0, t:t + 1, :]
        dist = n_past + t - pos
        bias = jnp.concatenate([_bias_from_dist(dist, rb_ref, h) for h in range(heads)], axis=0)
        s_g = _dot_nt(qcat, jnp.concatenate([blk(kg_ref, kv) for kv in range(n_kv)], axis=1)) + bias
        s_g = jnp.where(pos >= 0, s_g, NEG_INF)
        sel_n = jnp.logical_and(keynew_ref[0, t:t + 1, :] >= thr_ref[0, t:t + 1, :], lane <= t)
        s_n = jnp.where(sel_n, _dot_nt(qcat, knew) + tab_ref[2, rows, :], NEG_INF)
        m = jnp.maximum(jnp.max(s_g, axis=-1, keepdims=True), jnp.max(s_n, axis=-1, keepdims=True))
        p_g, p_n = jnp.exp(s_g - m), jnp.exp(s_n - m)
        denom = jnp.sum(p_g, axis=-1, keepdims=True) + jnp.sum(p_n, axis=-1, keepdims=True)
        o = _dot(by_kv(p_g).astype(BF16), jnp.concatenate([blk(vg_ref, kv) for kv in range(n_kv)], axis=0))
        o = o + _dot(by_kv(p_n).astype(BF16), vnew)
        o_ref[0, rows, :] = ((o / denom) * g_ref[0, rows, :]).astype(BF16)


def _smp_attn_g(rel_bias, q_s, g_s, kg, vg, posv, k_new, v_new, keys_new, thr, stab, n_past):
    s, rows, _ = q_s.shape
    n_tok, topk = posv.shape[1], posv.shape[2]
    seq = lambda *tail: pl.BlockSpec((1,) + tail, lambda si: (si,) + (0,) * len(tail))
    gathered = pl.BlockSpec((n_tok * KV_DSA * topk, HEAD_DIM), lambda si: (si, 0))
    return pl.pallas_call(
        functools.partial(_smp_attn_g_kernel, n_tok=n_tok, n_kv=KV_DSA, n_past=n_past),
        out_shape=jax.ShapeDtypeStruct((s, rows, HEAD_DIM), BF16), grid=(s,),
        in_specs=[pl.BlockSpec(memory_space=pltpu.SMEM), seq(rows, HEAD_DIM), seq(rows, HEAD_DIM), gathered, gathered,
                  seq(n_tok, topk), seq(PAGE_SIZE * KV_DSA, HEAD_DIM), seq(PAGE_SIZE * KV_DSA, HEAD_DIM),
                  seq(n_tok, LANES), seq(n_tok, LANES), pl.BlockSpec((3, rows, LANES), lambda si: (0, 0, 0))],
        out_specs=seq(rows, HEAD_DIM),
        compiler_params=_cparams(("arbitrary",)), name="smp_attn_gathered",
    )(rel_bias, q_s, g_s, kg, vg, posv, k_new, v_new, keys_new, thr, stab)


def _out_kernel(x_ref, gf_ref, gd_ref, p_ref, wo_ref, wpg_ref, wple_ref, gple_ref, o_ref):
    h = x_ref[...] + _dot(gf_ref[...], wo_ref[:W_HALF, :]) + _dot(gd_ref[...], wo_ref[W_HALF:, :])
    gate = _sigmoid(_dot(h.astype(BF16), wpg_ref[...]))
    e = _dot(p_ref[...].astype(BF16), wple_ref[...])
    e = e * lax.rsqrt(jnp.mean(e * e, axis=-1, keepdims=True) + EPS) * gple_ref[...]
    o_ref[...] = h + gate * e


def _out(x2, gf, gd, p2, wo, wpg, wple, gple, tm):
    rows = x2.shape[0]
    const = lambda shape: pl.BlockSpec(shape, lambda i: (0, 0), pipeline_mode=pl.Buffered(1))
    rmap = lambda i: (i, 0)
    return pl.pallas_call(
        _out_kernel, out_shape=jax.ShapeDtypeStruct((rows, D_MODEL), F32), grid=(rows // tm,),
        in_specs=[pl.BlockSpec((tm, D_MODEL), rmap), pl.BlockSpec((tm, W_HALF), rmap), pl.BlockSpec((tm, W_HALF), rmap),
                  pl.BlockSpec((tm, PLE_DIM), rmap),
                  const((D_MODEL, D_MODEL)), const((D_MODEL, D_MODEL)), const((PLE_DIM, D_MODEL)), const((1, D_MODEL))],
        out_specs=pl.BlockSpec((tm, D_MODEL), rmap),
        compiler_params=_cparams(("arbitrary",)), name="out",
    )(x2, gf, gd, p2, wo, wpg, wple, gple)


def _prep_w_in(w):
    points = [sum(SPLIT_SIZES[:i + 1]) for i in range(len(SPLIT_SIZES) - 1)]
    fq, fk, fv, ff, fz, dq, dk, dv, dz, iq, ik, iw = jnp.split(w, points, axis=1)
    main = jnp.concatenate([fq, fk, fv, fz, dq, dk, dv, dz, iq], axis=1).astype(BF16)
    pad = jnp.zeros((w.shape[0], LANES - D_IDX - H_IDX - H_FOX), w.dtype)
    small = jnp.concatenate([ik, iw, ff, pad], axis=1).astype(BF16)
    return main, small


def _pick_tile(n, pref):
    t = pref
    while n % t:
        t //= 2
    return t


def kernel(x_prompt, x_sample, cache_fox_k, cache_fox_v, cache_fox_logf, cache_dsa_k, cache_dsa_v, cache_idx_k,
           page_table, p_prompt, p_sample, rel_bias, norm_in, w_in, b_f, q_norm_fox, k_norm_fox, q_norm_dsa,
           k_norm_dsa, w_out, w_ple, ple_norm, w_pg):
    b, t, _ = x_prompt.shape
    s, n_tok, _ = x_sample.shape
    n_pages = page_table.shape[1]
    n_past = n_pages * PAGE_SIZE
    n_pool = cache_fox_k.shape[1]
    assert cache_fox_k.shape[0] == 1 and n_tok * H_FOX == 32

    w_main, w_small = _prep_w_in(w_in[0])
    nin = norm_in[0].reshape(1, D_MODEL)
    vec = lambda a: a[0].reshape(1, HEAD_DIM)
    gains = (vec(q_norm_fox), vec(k_norm_fox), vec(q_norm_dsa), vec(k_norm_dsa))
    bf = b_f[0].reshape(H_FOX, 1)
    wo = w_out[0].astype(BF16)
    wpg = w_pg[0].astype(BF16)
    wple = w_ple[0].astype(BF16)
    gple = ple_norm[0].reshape(1, D_MODEL)

    tq_d = _pick_tile(t, 256)
    ptab, stab = _bias_tables(rel_bias, tq_d)

    rows_p = b * t
    xp = x_prompt.reshape(rows_p, D_MODEL)
    (fq16, fk, fk16, fv, fv16, fg, dq16, dk, dk16, dv, dv16, dg, iq16, small, logft) = _proj(
        xp, nin, w_main, w_small, *gains, bf, _pick_tile(rows_p, 512))
    tq_f = _pick_tile(t, 256)
    ct = _cumsum_prompt(logft, t)
    c4 = ct.reshape(KV_FOX, 2, rows_p // tq_f, tq_f)
    gf = _fox_prompt(fq16, fk16, fv16, c4, fg, b, t, tq_f)
    ik16 = small[:, SM_IK:SM_IK + D_IDX].astype(BF16)
    ik2 = jnp.concatenate([ik16, ik16], axis=1)
    topk_p = min(TOPK_MAX, t // 4)
    gd = _dsa_prompt(rel_bias, iq16, ik2, small, dq16, dk16, dv16, dg, ptab, b, t, tq_d, topk_p)
    y_p = _out(xp, gf, gd, p_prompt[0].reshape(rows_p, PLE_DIM), wo, wpg, wple, gple, _pick_tile(rows_p, 256))

    rows_s = s * n_tok
    xs = x_sample.reshape(rows_s, D_MODEL)
    (sfq16, sfk, _, sfv, _, sfg, sdq16, sdk, _, sdv, _, sdg, siq16, ssmall, slogft) = _proj(
        xs, nin, w_main, w_small, *gains, bf, _pick_tile(rows_s, 512))
    pps = _pick_tile(n_pages, 16)
    rows_q = n_tok * H_FOX

    def new_page(a):
        a = a.reshape(s, n_tok * KV_FOX, HEAD_DIM)
        return jnp.pad(a, ((0, 0), (0, (PAGE_SIZE - n_tok) * KV_FOX), (0, 0)))

    q_rows = lambda a: a.reshape(s, rows_q, HEAD_DIM)
    pool = lambda c: c[0].reshape(n_pool, PAGE_SIZE * KV_FOX, HEAD_DIM)
    lf_pool = jnp.transpose(cache_fox_logf[0], (0, 2, 1))
    lf_new = jnp.pad(jnp.transpose(slogft.reshape(H_FOX, s, n_tok), (1, 0, 2)), ((0, 0), (0, 0), (0, PAGE_SIZE - n_tok)))
    sgf = _smp_fox(page_table, q_rows(sfq16), q_rows(sfg), pool(cache_fox_k), pool(cache_fox_v), lf_pool,
                   new_page(sfk), new_page(sfv), lf_new, pps)

    iq_s = siq16.reshape(s, n_tok * H_IDX, D_IDX)
    w_s = (ssmall[:, SM_IW:SM_IW + H_IDX] * (H_IDX ** -0.5)).reshape(s, n_tok * H_IDX, 1)
    ik_new = jnp.pad(jnp.transpose(ssmall[:, SM_IK:SM_IK + D_IDX].reshape(s, n_tok, D_IDX), (0, 2, 1)),
                     ((0, 0), (0, 0), (0, PAGE_SIZE - n_tok)))
    keys, keys_new = _smp_scores(page_table, iq_s, w_s, jnp.transpose(cache_idx_k[0], (0, 2, 1)), ik_new,
                                 _pick_tile(n_pages, 64))
    topk_s = min(TOPK_MAX, (n_past + n_tok) // 4)
    thr = _smp_thresh(keys.reshape(rows_s, n_past), keys_new.reshape(rows_s, LANES), topk_s).reshape(s, n_tok, LANES)
    assert n_pool <= 256 * PT_SPLIT and topk_s % LANES == 0
    pt_col = lambda a: jnp.broadcast_to(a.astype(BF16)[:, :, None], (s, n_pages, LANES))
    phys, posv = _smp_compact(keys.reshape(s, n_tok, n_pages, PAGE_SIZE), thr,
                              pt_col(page_table // PT_SPLIT), pt_col(page_table % PT_SPLIT), topk_s)
    idx = (phys.reshape(s, n_tok, 1, topk_s) * KV_DSA + jnp.arange(KV_DSA, dtype=I32).reshape(1, 1, KV_DSA, 1)).reshape(-1)
    kv_rows = lambda c: c[0].reshape(n_pool * PAGE_SIZE * KV_DSA, HEAD_DIM)
    kg = _sc_gather_rows(kv_rows(cache_dsa_k), idx)
    vg = _sc_gather_rows(kv_rows(cache_dsa_v), idx)
    sgd = _smp_attn_g(rel_bias, q_rows(sdq16), q_rows(sdg), kg, vg, posv.reshape(s, n_tok, topk_s),
                      new_page(sdk), new_page(sdv), keys_new, thr, stab, n_past)
    y_s = _out(xs, sgf.reshape(rows_s, W_HALF), sgd.reshape(rows_s, W_HALF), p_sample[0].reshape(rows_s, PLE_DIM),
               wo, wpg, wple, gple, _pick_tile(rows_s, 256))

    def kv5(a, bb, tt):
        return a.reshape(1, bb, tt, KV_FOX, HEAD_DIM)

    def outs(bb, tt, fk_, fv_, logft_, dk_, dv_, small_):
        return (kv5(fk_, bb, tt), kv5(fv_, bb, tt), logft_.T.reshape(1, bb, tt, H_FOX),
                kv5(dk_, bb, tt), kv5(dv_, bb, tt), small_[:, SM_IK:SM_IK + D_IDX].reshape(1, bb, tt, D_IDX))

    return ((y_p.reshape(b, t, D_MODEL), y_s.reshape(s, n_tok, D_MODEL))
            + outs(b, t, fk, fv, logft, dk, dv, small)
            + outs(s, n_tok, sfk, sfv, slogft, sdk, sdv, ssmall))
```

```python
import functools
import math

import jax
import jax.numpy as jnp
from jax import lax
from jax.experimental import pallas as pl
from jax.experimental.pallas import tpu as pltpu
from jax.experimental.pallas import tpu_sc as plsc

F32 = jnp.float32
BF16 = jnp.bfloat16
I32 = jnp.int32

D_MODEL = 2048
HEAD_DIM = 128
H_FOX = 8
KV_FOX = 4
H_DSA = 8
KV_DSA = 4
H_IDX = 16
D_IDX = 64
TOPK_MAX = 256
N_BUCKETS = 32
MAX_DISTANCE = 128
PLE_DIM = 256
PAGE_SIZE = 128
EPS = 1e-6
W_HALF = H_FOX * HEAD_DIM
W_KV = KV_FOX * HEAD_DIM
SPLIT_SIZES = (W_HALF, W_KV, W_KV, H_FOX, W_HALF, W_HALF, W_KV, W_KV, W_HALF, H_IDX * D_IDX, D_IDX, H_IDX)

LANES = 128
INT_MIN = -(2 ** 31)
NEG_INF = float("-inf")
VMEM_LIMIT = 56 * 1024 * 1024
SC_GATHER_WINDOW = 128
PT_SPLIT = 64

PROJ_TN = W_HALF
PROJ_STEPS = ("fq", "fkv", "fz", "dq", "dkv", "dz", "iq")
N_MAIN_BLOCKS = len(PROJ_STEPS)
SM_IK, SM_IW, SM_FF = 0, D_IDX, D_IDX + H_IDX


def _cparams(sem):
    return pltpu.CompilerParams(dimension_semantics=sem, vmem_limit_bytes=VMEM_LIMIT)


def _dot_nt(a, b):
    return lax.dot_general(a, b, (((1,), (1,)), ((), ())), preferred_element_type=F32)


def _dot(a, b):
    return jnp.dot(a, b, preferred_element_type=F32)


def _log_sigmoid(x):
    return -(jnp.maximum(-x, 0.0) + jnp.log(1.0 + jnp.exp(-jnp.abs(x))))


def _sigmoid(x):
    return 1.0 / (1.0 + jnp.exp(-x))


def _sort_key(x):
    b = pltpu.bitcast(x, I32)
    return b ^ ((b >> 31) & jnp.int32(0x7FFFFFFF))


def _tile_lanes(x, width):
    return x if width == LANES else jnp.concatenate([x] * (width // LANES), axis=1)


def _with_ones(v):
    return jnp.concatenate([v, jnp.ones_like(v)], axis=1)


def _cumsum_lanes(x):
    n = x.shape[-1]
    lane = lax.broadcasted_iota(I32, x.shape, x.ndim - 1)
    k = 1
    while k < n:
        x = x + jnp.where(lane >= k, pltpu.roll(x, k, axis=x.ndim - 1), 0.0)
        k *= 2
    return x


def _proj_kernel(x_ref, nin_ref, w_ref, ws_ref, gqf_ref, gkf_ref, gqd_ref, gkd_ref, bf_ref,
                 fq_o, fk_o, fk16_o, fv_o, fv16_o, fg_o, dq_o, dk_o, dk16_o, dv_o, dv16_o, dg_o,
                 iq_o, small_o, logft_o, xn_s):
    j, r = pl.program_id(1), pl.program_id(2)

    @pl.when(j == 0)
    def _():
        x = x_ref[...]
        ms = jnp.mean(x * x, axis=-1, keepdims=True)
        xn_s[r] = (x * lax.rsqrt(ms + EPS) * nin_ref[...]).astype(BF16)

    def main():
        return _dot_nt(xn_s[r], w_ref[...])

    def head_norm(y, g_ref, scale):
        outs = []
        for c in range(y.shape[1] // HEAD_DIM):
            yh = y[:, c * HEAD_DIM:(c + 1) * HEAD_DIM]
            n = yh * lax.rsqrt(jnp.mean(yh * yh, axis=-1, keepdims=True) + EPS) * g_ref[...]
            outs.append(n * scale if scale != 1.0 else n)
        return jnp.concatenate(outs, axis=1)

    def store_kv(o_ref, y):
        for kv in range(KV_FOX):
            o_ref[pl.ds(kv, y.shape[0], stride=KV_FOX), :] = y[:, kv * HEAD_DIM:(kv + 1) * HEAD_DIM]

    def kv_step(gk_ref, k_o, k16_o, v_o, v16_o):
        y = main()
        n = head_norm(y[:, :W_KV], gk_ref, 1.0)
        store_kv(k_o, n)
        k16_o[...] = n.astype(BF16)
        store_kv(v_o, y[:, W_KV:])
        v16_o[...] = y[:, W_KV:].astype(BF16)

    def gate_step(o_ref):
        y = main()
        o_ref[...] = (y * _sigmoid(y)).astype(BF16)

    @pl.when(j == PROJ_STEPS.index("fq"))
    def _():
        fq_o[...] = head_norm(main(), gqf_ref, HEAD_DIM ** -0.5).astype(BF16)

    @pl.when(j == PROJ_STEPS.index("fkv"))
    def _():
        kv_step(gkf_ref, fk_o, fk16_o, fv_o, fv16_o)

    @pl.when(j == PROJ_STEPS.index("fz"))
    def _():
        gate_step(fg_o)

    @pl.when(j == PROJ_STEPS.index("dq"))
    def _():
        dq_o[...] = head_norm(main(), gqd_ref, HEAD_DIM ** -0.5).astype(BF16)

    @pl.when(j == PROJ_STEPS.index("dkv"))
    def _():
        kv_step(gkd_ref, dk_o, dk16_o, dv_o, dv16_o)

    @pl.when(j == PROJ_STEPS.index("dz"))
    def _():
        gate_step(dg_o)

    @pl.when(j == PROJ_STEPS.index("iq"))
    def _():
        iq_o[...] = (main() * (D_IDX ** -0.5)).astype(BF16)

    @pl.when(j == N_MAIN_BLOCKS)
    def _():
        ys = _dot_nt(xn_s[r], ws_ref[...])
        small_o[...] = ys
        ff_t = ys.T[SM_FF:SM_FF + H_FOX, :]
        logft_o[...] = _log_sigmoid(ff_t + bf_ref[...])


def _proj(x2, nin, w_main, w_small, gqf, gkf, gqd, gkd, bf, tm, group):
    rows = x2.shape[0]
    assert rows % (tm * group) == 0
    grid = (rows // (tm * group), N_MAIN_BLOCKS + 1, group)

    def tile(k):
        return lambda i, j, r: i * group + jnp.where(j < k, 0, jnp.where(j == k, r, group - 1))

    def out_spec(shape, k, row_dim=0):
        t = tile(k)
        return pl.BlockSpec(shape, (lambda i, j, r: (t(i, j, r), 0)) if row_dim == 0 else (lambda i, j, r: (0, t(i, j, r))))

    step = PROJ_STEPS.index
    rows_of = lambda width, k: out_spec((tm, width), k)
    kv_rows = lambda k: out_spec((tm * KV_FOX, HEAD_DIM), k)
    row_vec = lambda n: pl.BlockSpec((1, n), lambda i, j, r: (0, 0))
    in_specs = [
        out_spec((tm, D_MODEL), 0),
        row_vec(D_MODEL),
        pl.BlockSpec((PROJ_TN, D_MODEL), lambda i, j, r: (jnp.minimum(j, N_MAIN_BLOCKS - 1), 0)),
        pl.BlockSpec((LANES, D_MODEL), lambda i, j, r: (0, 0)),
        row_vec(HEAD_DIM), row_vec(HEAD_DIM), row_vec(HEAD_DIM), row_vec(HEAD_DIM),
        pl.BlockSpec((H_FOX, 1), lambda i, j, r: (0, 0)),
    ]
    sds = jax.ShapeDtypeStruct
    out_shape = [
        sds((rows, W_HALF), BF16),
        sds((rows * KV_FOX, HEAD_DIM), F32), sds((rows, W_KV), BF16),
        sds((rows * KV_FOX, HEAD_DIM), F32), sds((rows, W_KV), BF16),
        sds((rows, W_HALF), BF16),
        sds((rows, W_HALF), BF16),
        sds((rows * KV_FOX, HEAD_DIM), F32), sds((rows, W_KV), BF16),
        sds((rows * KV_FOX, HEAD_DIM), F32), sds((rows, W_KV), BF16),
        sds((rows, W_HALF), BF16),
        sds((rows, W_HALF), BF16),
        sds((rows, LANES), F32),
        sds((H_FOX, rows), F32),
    ]
    out_specs = [
        rows_of(W_HALF, step("fq")),
        kv_rows(step("fkv")), rows_of(W_KV, step("fkv")),
        kv_rows(step("fkv")), rows_of(W_KV, step("fkv")),
        rows_of(W_HALF, step("fz")),
        rows_of(W_HALF, step("dq")),
        kv_rows(step("dkv")), rows_of(W_KV, step("dkv")),
        kv_rows(step("dkv")), rows_of(W_KV, step("dkv")),
        rows_of(W_HALF, step("dz")),
        rows_of(W_HALF, step("iq")),
        rows_of(LANES, N_MAIN_BLOCKS),
        out_spec((H_FOX, tm), N_MAIN_BLOCKS, row_dim=1),
    ]
    return pl.pallas_call(
        _proj_kernel, out_shape=out_shape, grid=grid, in_specs=in_specs, out_specs=out_specs,
        scratch_shapes=[pltpu.VMEM((group, tm, D_MODEL), BF16)],
        compiler_params=_cparams(("arbitrary", "arbitrary", "arbitrary")), name="proj",
    )(x2, nin, w_main, w_small, gqf, gkf, gqd, gkd, bf)


def _cumsum_kernel(x_ref, o_ref):
    o_ref[...] = _cumsum_lanes(x_ref[...])


def _cumsum_prompt(logft, t):
    rows = logft.shape[1]
    return pl.pallas_call(
        _cumsum_kernel, out_shape=jax.ShapeDtypeStruct(logft.shape, F32), grid=(rows // t,),
        in_specs=[pl.BlockSpec((H_FOX, t), lambda b: (0, b))],
        out_specs=pl.BlockSpec((H_FOX, t), lambda b: (0, b)),
        compiler_params=_cparams(("arbitrary",)), name="cumsum",
    )(logft)


def _fox_kernel(q_ref, k_ref, v_ref, c_ref, g_ref, o_ref, m_s, l_s, acc_s, *, tq):
    qi = pl.program_id(1)
    row = lax.broadcasted_iota(I32, (tq, tq), 0)
    col = lax.broadcasted_iota(I32, (tq, tq), 1)
    m_s[...] = jnp.full(m_s.shape, NEG_INF, F32)
    l_s[...] = jnp.zeros(l_s.shape, F32)
    acc_s[...] = jnp.zeros(acc_s.shape, F32)

    def chunk(j, diag):
        off = pl.multiple_of(j * tq, tq)
        for kv in range(KV_FOX):
            q2 = jnp.concatenate([q_ref[:, (2 * kv + g) * HEAD_DIM:(2 * kv + g + 1) * HEAD_DIM] for g in range(2)], axis=0)
            s = _dot_nt(q2, k_ref[pl.ds(off, tq), kv * HEAD_DIM:(kv + 1) * HEAD_DIM])
            ps, alphas = [], []
            for g in range(2):
                h = 2 * kv + g
                sg = s[g * tq:(g + 1) * tq] - c_ref[kv, g, pl.ds(j, 1), :]
                if diag:
                    sg = jnp.where(col <= row, sg, NEG_INF)
                m_old = m_s[h]
                m_new = jnp.maximum(m_old, jnp.broadcast_to(jnp.max(sg, axis=-1, keepdims=True), (tq, LANES)))
                alphas.append(jnp.exp(m_old - m_new))
                ps.append(jnp.exp(sg - _tile_lanes(m_new, tq)).astype(BF16))
                m_s[h] = m_new
            v = v_ref[pl.ds(off, tq), kv * HEAD_DIM:(kv + 1) * HEAD_DIM]
            pv = _dot(jnp.concatenate(ps, axis=0), _with_ones(v))
            for g in range(2):
                h = 2 * kv + g
                l_s[h] = alphas[g] * l_s[h] + pv[g * tq:(g + 1) * tq, HEAD_DIM:]
                acc_s[h] = alphas[g] * acc_s[h] + pv[g * tq:(g + 1) * tq, :HEAD_DIM]

    def off_diag(j, _):
        chunk(j, False)
        return 0

    lax.fori_loop(0, qi, off_diag, 0)
    chunk(qi, True)
    for h in range(H_FOX):
        o = acc_s[h] / l_s[h]
        o_ref[:, h * HEAD_DIM:(h + 1) * HEAD_DIM] = (o * g_ref[:, h * HEAD_DIM:(h + 1) * HEAD_DIM]).astype(BF16)


def _fox_prompt(fq16, fk16, fv16, c4, fg, b, t, tq):
    nq = t // tq
    qmap = lambda bi, qi: (bi * nq + qi, 0)
    bmap = lambda bi, qi: (bi, 0)
    return pl.pallas_call(
        functools.partial(_fox_kernel, tq=tq),
        out_shape=jax.ShapeDtypeStruct((b * t, W_HALF), BF16), grid=(b, nq),
        in_specs=[
            pl.BlockSpec((tq, W_HALF), qmap),
            pl.BlockSpec((t, W_KV), bmap),
            pl.BlockSpec((t, W_KV), bmap),
            pl.BlockSpec((KV_FOX, 2, nq, tq), lambda bi, qi: (0, 0, bi, 0)),
            pl.BlockSpec((tq, W_HALF), qmap),
        ],
        out_specs=pl.BlockSpec((tq, W_HALF), qmap),
        scratch_shapes=[pltpu.VMEM((H_FOX, tq, LANES), F32), pltpu.VMEM((H_FOX, tq, LANES), F32),
                        pltpu.VMEM((H_FOX, tq, HEAD_DIM), F32)],
        compiler_params=_cparams(("arbitrary", "arbitrary")), name="fox_prompt",
    )(fq16, fk16, fv16, c4, fg)


def _t5_bucket(d):
    max_exact = N_BUCKETS // 2
    d = jnp.maximum(d, 0)
    lr = jnp.log(jnp.maximum(d, 1).astype(F32) / max_exact) / math.log(MAX_DISTANCE / max_exact)
    large = jnp.minimum(max_exact + (lr * (N_BUCKETS - max_exact)).astype(I32), N_BUCKETS - 1)
    return jnp.where(d < max_exact, d, large)


def _bias_from_dist(dist, rb_ref, h):
    bucket = _t5_bucket(dist)
    out = jnp.zeros(dist.shape, F32)
    for bkt in range(N_BUCKETS):
        out = jnp.where(bucket == bkt, rb_ref[bkt, h], out)
    return out


def _bias_tab_kernel(rb_ref, ptab_ref, stab_ref, *, tq):
    r = lax.broadcasted_iota(I32, (tq, 2 * tq), 0)
    c = lax.broadcasted_iota(I32, (tq, 2 * tq), 1)
    for h in range(H_DSA):
        ptab_ref[h] = _bias_from_dist(r - c + tq, rb_ref, h)
    rows = 4 * H_DSA
    rr = lax.broadcasted_iota(I32, (rows, LANES), 0)
    pos = lax.broadcasted_iota(I32, (rows, LANES), 1)
    t = rr // H_DSA
    hh = rr % H_DSA
    dists = (jnp.full((rows, LANES), 2 * MAX_DISTANCE, I32), PAGE_SIZE + t - pos, t - pos)
    for k, dist in enumerate(dists):
        acc = jnp.zeros((rows, LANES), F32)
        for h in range(H_DSA):
            acc = jnp.where(hh == h, _bias_from_dist(dist, rb_ref, h), acc)
        stab_ref[k] = acc


def _bias_tables(rel_bias, tq):
    return pl.pallas_call(
        functools.partial(_bias_tab_kernel, tq=tq),
        out_shape=[jax.ShapeDtypeStruct((H_DSA, tq, 2 * tq), F32),
                   jax.ShapeDtypeStruct((3, 4 * H_DSA, LANES), F32)],
        in_specs=[pl.BlockSpec(memory_space=pltpu.SMEM)],
        name="bias_tables",
    )(rel_bias)


def _kth_largest_key(count_ge, shape, k):
    def step(i, cur):
        cand = cur + lax.shift_left(jnp.int32(1), jnp.int32(31) - i)
        return jnp.where(count_ge(cand) >= k, cand, cur)
    return lax.fori_loop(0, 32, step, jnp.full(shape, INT_MIN, I32))


def _dsa_kernel(rb_ref, iq_ref, ik2_ref, sm_ref, q_ref, k_ref, v_ref, g_ref, tab_ref, o_ref,
                key_s, keyt_s, wb_s, m_s, l_s, acc_s, *, tq, topk):
    qi = pl.program_id(1)
    n_pairs = H_IDX // 2
    row = lax.broadcasted_iota(I32, (tq, tq), 0)
    col = lax.broadcasted_iota(I32, (tq, tq), 1)
    lane = lax.broadcasted_iota(I32, (tq, LANES), 1)

    w = sm_ref[:, SM_IW:SM_IW + H_IDX] * (H_IDX ** -0.5)
    for h in range(H_IDX):
        wb_s[h] = jnp.broadcast_to(w[:, h:h + 1], (tq, LANES))
    iqs = jnp.concatenate([iq_ref[:, p * LANES:(p + 1) * LANES] for p in range(n_pairs)], axis=0)

    def score_chunk(j, _):
        off = pl.multiple_of(j * tq, tq)
        ik2 = ik2_ref[pl.ds(off, tq), :]
        rhs = jnp.concatenate([jnp.where(lane < D_IDX, ik2, 0), jnp.where(lane >= D_IDX, ik2, 0)], axis=0)
        s2 = _dot_nt(iqs, rhs)
        sc = jnp.zeros((tq, tq), F32)
        for p in range(n_pairs):
            for e in range(2):
                r = jnp.maximum(s2[p * tq:(p + 1) * tq, e * tq:(e + 1) * tq], 0.0)
                sc = sc + jnp.concatenate([wb_s[2 * p + e]] * (tq // LANES), axis=1) * r
        key = _sort_key(jnp.where(off + col <= qi * tq + row, sc, NEG_INF))
        key_s[j] = key
        keyt_s[j] = key.T
        return 0

    lax.fori_loop(0, qi + 1, score_chunk, 0)

    def count_ge(cand):
        def body(j, cnt):
            hit = jnp.where(keyt_s[j] >= cand, 1, 0)
            return cnt + jnp.sum(hit.reshape(tq // 8, 8, tq), axis=0)
        cnt = lax.fori_loop(0, qi + 1, body, jnp.zeros((8, tq), I32))
        return jnp.sum(cnt, axis=0, keepdims=True)

    thr_row = _kth_largest_key(count_ge, (1, tq), topk)
    thr_col = jnp.broadcast_to(thr_row, (LANES, tq)).T
    thr = jnp.concatenate([thr_col] * (tq // LANES), axis=1)

    m_s[...] = jnp.full(m_s.shape, NEG_INF, F32)
    l_s[...] = jnp.zeros(l_s.shape, F32)
    acc_s[...] = jnp.zeros(acc_s.shape, F32)

    def attend(j, mode):
        off = pl.multiple_of(j * tq, tq)
        sel = key_s[j] >= thr
        if mode == 2:
            sel = jnp.logical_and(sel, col <= row)
        for kv in range(KV_DSA):
            q2 = jnp.concatenate([q_ref[:, (2 * kv + g) * HEAD_DIM:(2 * kv + g + 1) * HEAD_DIM] for g in range(2)], axis=0)
            s = _dot_nt(q2, k_ref[pl.ds(off, tq), kv * HEAD_DIM:(kv + 1) * HEAD_DIM])
            ps, alphas = [], []
            for g in range(2):
                h = 2 * kv + g
                sg = s[g * tq:(g + 1) * tq]
                if mode == 0:
                    sg = sg + rb_ref[N_BUCKETS - 1, h]
                elif mode == 1:
                    sg = sg + tab_ref[h, :, :tq]
                else:
                    sg = sg + tab_ref[h, :, tq:]
                sg = jnp.where(sel, sg, NEG_INF)
                m_old = m_s[h]
                m_new = jnp.maximum(m_old, jnp.broadcast_to(jnp.max(sg, axis=-1, keepdims=True), (tq, LANES)))
                m_safe = jnp.where(m_new == NEG_INF, 0.0, m_new)
                alphas.append(jnp.exp(m_old - m_safe))
                ps.append(jnp.exp(sg - _tile_lanes(m_safe, tq)).astype(BF16))
                m_s[h] = m_new
            v = v_ref[pl.ds(off, tq), kv * HEAD_DIM:(kv + 1) * HEAD_DIM]
            pv = _dot(jnp.concatenate(ps, axis=0), _with_ones(v))
            for g in range(2):
                h = 2 * kv + g
                l_s[h] = alphas[g] * l_s[h] + pv[g * tq:(g + 1) * tq, HEAD_DIM:]
                acc_s[h] = alphas[g] * acc_s[h] + pv[g * tq:(g + 1) * tq, :HEAD_DIM]

    def far(j, _):
        attend(j, 0)
        return 0

    lax.fori_loop(0, jnp.maximum(qi - 1, 0), far, 0)

    @pl.when(qi >= 1)
    def _():
        attend(qi - 1, 1)

    attend(qi, 2)

    for h in range(H_DSA):
        o = acc_s[h] / l_s[h]
        o_ref[:, h * HEAD_DIM:(h + 1) * HEAD_DIM] = (o * g_ref[:, h * HEAD_DIM:(h + 1) * HEAD_DIM]).astype(BF16)


def _dsa_prompt(rel_bias, iq16, ik2, small, dq16, dk16, dv16, dg, ptab, b, t, tq, topk):
    nq = t // tq
    rows = b * t
    qmap = lambda bi, qi: (bi * nq + qi, 0)
    bmap = lambda bi, qi: (bi, 0)
    return pl.pallas_call(
        functools.partial(_dsa_kernel, tq=tq, topk=topk),
        out_shape=jax.ShapeDtypeStruct((rows, W_HALF), BF16), grid=(b, nq),
        in_specs=[
            pl.BlockSpec(memory_space=pltpu.SMEM),
            pl.BlockSpec((tq, H_IDX * D_IDX), qmap),
            pl.BlockSpec((t, LANES), bmap),
            pl.BlockSpec((tq, LANES), qmap),
            pl.BlockSpec((tq, W_HALF), qmap),
            pl.BlockSpec((t, W_KV), bmap),
            pl.BlockSpec((t, W_KV), bmap),
            pl.BlockSpec((tq, W_HALF), qmap),
            pl.BlockSpec((H_DSA, tq, 2 * tq), lambda bi, qi: (0, 0, 0), pipeline_mode=pl.Buffered(1)),
        ],
        out_specs=pl.BlockSpec((tq, W_HALF), qmap),
        scratch_shapes=[pltpu.VMEM((nq, tq, tq), I32), pltpu.VMEM((nq, tq, tq), I32),
                        pltpu.VMEM((H_IDX, tq, LANES), F32),
                        pltpu.VMEM((H_DSA, tq, LANES), F32), pltpu.VMEM((H_DSA, tq, LANES), F32),
                        pltpu.VMEM((H_DSA, tq, HEAD_DIM), F32)],
        compiler_params=_cparams(("arbitrary", "arbitrary")), name="dsa_prompt",
    )(rel_bias, iq16, ik2, small, dq16, dk16, dv16, dg, ptab)


def _page_specs(shape_tail, pps, new_step_tail):
    nd = len(shape_tail)
    return [pl.BlockSpec((1,) + shape_tail, functools.partial(
        lambda s, p, pt, i: (pt[s, p * pps + i],) + (0,) * nd, i=i)) for i in range(pps)]


def _smp_score_kernel(pt_ref, iq_ref, w_ref, *refs, pps, n_tok):
    ik_refs, iknew_ref, o_ref, onew_ref = refs[:pps], refs[pps], refs[pps + 1], refs[pps + 2]
    p = pl.program_id(1)
    iq = iq_ref[0]
    wcol = w_ref[0]

    def page_scores(ik_t):
        r = jnp.maximum(_dot(iq, ik_t.astype(BF16)), 0.0) * wcol
        return jnp.sum(r.reshape(n_tok, H_IDX, ik_t.shape[1]), axis=1)

    ik_all = jnp.concatenate([ik_refs[i][0] for i in range(pps)], axis=1)
    o_ref[0] = _sort_key(page_scores(ik_all))

    @pl.when(p == pl.num_programs(1) - 1)
    def _():
        sc = page_scores(iknew_ref[0])
        t = lax.broadcasted_iota(I32, (n_tok, LANES), 0)
        pos = lax.broadcasted_iota(I32, (n_tok, LANES), 1)
        onew_ref[0] = _sort_key(jnp.where(pos <= t, sc, NEG_INF))


def _smp_scores(page_table, iq_s, w_s, ik_pool, ik_new, pps):
    s, n_pages = page_table.shape
    n_tok = iq_s.shape[1] // H_IDX
    grid_spec = pltpu.PrefetchScalarGridSpec(
        num_scalar_prefetch=1, grid=(s, n_pages // pps),
        in_specs=[pl.BlockSpec((1, n_tok * H_IDX, D_IDX), lambda si, p, pt: (si, 0, 0)),
                  pl.BlockSpec((1, n_tok * H_IDX, 1), lambda si, p, pt: (si, 0, 0))]
                 + _page_specs((D_IDX, PAGE_SIZE), pps, None)
                 + [pl.BlockSpec((1, D_IDX, PAGE_SIZE), lambda si, p, pt: (si, 0, 0))],
        out_specs=[pl.BlockSpec((1, n_tok, pps * LANES), lambda si, p, pt: (si, 0, p)),
                   pl.BlockSpec((1, n_tok, LANES), lambda si, p, pt: (si, 0, 0))],
    )
    return pl.pallas_call(
        functools.partial(_smp_score_kernel, pps=pps, n_tok=n_tok),
        out_shape=[jax.ShapeDtypeStruct((s, n_tok, n_pages * LANES), I32),
                   jax.ShapeDtypeStruct((s, n_tok, LANES), I32)], grid_spec=grid_spec,
        compiler_params=_cparams(("arbitrary", "arbitrary")), name="smp_scores",
    )(page_table, iq_s, w_s, *([ik_pool] * pps), ik_new)


def _smp_thresh_kernel(key_ref, knew_ref, o_ref, *, topk):
    def count_ge(cand):
        hit = jnp.sum(jnp.where(key_ref[...] >= cand, 1, 0), axis=-1, keepdims=True)
        return hit + jnp.sum(jnp.where(knew_ref[...] >= cand, 1, 0), axis=-1, keepdims=True)

    thr = _kth_largest_key(count_ge, (key_ref.shape[0], 1), topk)
    o_ref[...] = jnp.broadcast_to(thr, o_ref.shape)


def _smp_thresh(keys, keys_new, topk):
    rows, n = keys.shape
    tr = _pick_tile(rows, 32)
    return pl.pallas_call(
        functools.partial(_smp_thresh_kernel, topk=topk),
        out_shape=jax.ShapeDtypeStruct((rows, LANES), I32), grid=(rows // tr,),
        in_specs=[pl.BlockSpec((tr, n), lambda i: (i, 0)), pl.BlockSpec((tr, LANES), lambda i: (i, 0))],
        out_specs=pl.BlockSpec((tr, LANES), lambda i: (i, 0)),
        compiler_params=_cparams(("arbitrary",)), name="smp_thresh",
    )(keys, keys_new)


def _smp_fox_kernel(pt_ref, q_ref, g_ref, *refs, pps, n_tok, n_kv):
    k_refs, v_refs, lf_refs = refs[:pps], refs[pps:2 * pps], refs[2 * pps:3 * pps]
    knew_ref, vnew_ref, lfnew_ref, o_ref, m_s, l_s, acc_s, carry_s = refs[3 * pps:]
    p = pl.program_id(1)
    last = pl.num_programs(1) - 1
    rows = q_ref.shape[1]
    heads = rows // n_tok
    grp = heads // n_kv
    q = q_ref[0]
    rr = lax.broadcasted_iota(I32, (rows, LANES), 0)
    pos = lax.broadcasted_iota(I32, (rows, LANES), 1)
    row_kv = (rr % heads) // grp
    qcat = jnp.concatenate([jnp.where(row_kv == kv, q, jnp.zeros_like(q)) for kv in range(n_kv)], axis=1)

    def heads_of(ref, kv):
        return ref[0, pl.ds(kv, PAGE_SIZE, stride=n_kv), :].astype(BF16)

    @pl.when(p == 0)
    def _():
        m_s[...] = jnp.full(m_s.shape, NEG_INF, F32)
        l_s[...] = jnp.zeros(l_s.shape, F32)
        acc_s[...] = jnp.zeros(acc_s.shape, F32)
        carry_s[...] = jnp.zeros(carry_s.shape, F32)

    def attend(kv_refs, bias, sel):
        kcat = jnp.concatenate([jnp.concatenate([heads_of(k_ref, kv) for kv in range(n_kv)], axis=1)
                                for k_ref, _ in kv_refs], axis=0)
        s = _dot_nt(qcat, kcat) + bias
        if sel is not None:
            s = jnp.where(sel, s, NEG_INF)
        m_old = m_s[...]
        m_new = jnp.maximum(m_old, jnp.max(s, axis=-1, keepdims=True))
        m_safe = jnp.where(m_new == NEG_INF, 0.0, m_new)
        alpha = jnp.exp(m_old - m_safe)
        pr = jnp.exp(s - m_safe)
        l_s[...] = alpha * l_s[...] + jnp.sum(pr, axis=-1, keepdims=True)
        m_s[...] = m_new
        pcat = jnp.concatenate([jnp.where(row_kv == kv, pr[:, i * LANES:(i + 1) * LANES], 0.0).astype(BF16)
                                for i in range(len(kv_refs)) for kv in range(n_kv)], axis=1)
        vcat = jnp.concatenate([heads_of(v_ref, kv) for _, v_ref in kv_refs for kv in range(n_kv)], axis=0)
        acc_s[...] = alpha * acc_s[...] + _dot(pcat, vcat)

    def fox_bias(lf_list):
        c = carry_s[...] + _cumsum_lanes(jnp.concatenate([r[0] for r in lf_list], axis=1))
        carry_s[...] = c[:, c.shape[1] - 1:]
        return -jnp.concatenate([c] * n_tok, axis=0)

    attend(list(zip(k_refs, v_refs)), fox_bias(lf_refs), None)

    @pl.when(p == last)
    def _():
        attend([(knew_ref, vnew_ref)], fox_bias([lfnew_ref]), pos <= rr // heads)
        o_ref[0] = ((acc_s[...] / l_s[...]) * g_ref[0]).astype(BF16)


def _smp_fox(page_table, q_s, g_s, k_pool, v_pool, lf_pool, k_new, v_new, lf_new, pps):
    s, n_pages = page_table.shape
    rows = q_s.shape[1]
    seq3 = lambda a, b: pl.BlockSpec((1, a, b), lambda si, p, pt: (si, 0, 0))
    in_specs = ([seq3(rows, HEAD_DIM), seq3(rows, HEAD_DIM)]
                + _page_specs((PAGE_SIZE * KV_FOX, HEAD_DIM), pps, None) * 2
                + _page_specs((H_FOX, PAGE_SIZE), pps, None)
                + [seq3(PAGE_SIZE * KV_FOX, HEAD_DIM)] * 2 + [seq3(H_FOX, PAGE_SIZE)])
    grid_spec = pltpu.PrefetchScalarGridSpec(
        num_scalar_prefetch=1, grid=(s, n_pages // pps), in_specs=in_specs, out_specs=seq3(rows, HEAD_DIM),
        scratch_shapes=[pltpu.VMEM((rows, 1), F32), pltpu.VMEM((rows, 1), F32), pltpu.VMEM((rows, HEAD_DIM), F32),
                        pltpu.VMEM((H_FOX, 1), F32)])
    return pl.pallas_call(
        functools.partial(_smp_fox_kernel, pps=pps, n_tok=rows // H_FOX, n_kv=KV_FOX),
        out_shape=jax.ShapeDtypeStruct((s, rows, HEAD_DIM), BF16), grid_spec=grid_spec,
        compiler_params=_cparams(("arbitrary", "arbitrary")), name="smp_attn_fox",
    )(page_table, q_s, g_s, *([k_pool] * pps), *([v_pool] * pps), *([lf_pool] * pps), k_new, v_new, lf_new)


def _smp_compact_kernel(key_ref, thr_ref, pthi_ref, ptlo_ref, phys_o, pos_o, *, n_tok, topk):
    n_pg = key_ref.shape[2]
    reps = topk // LANES
    pg = lax.broadcasted_iota(I32, (n_pg, LANES), 0).astype(BF16)
    strict_lower = (lax.broadcasted_iota(I32, (n_pg, n_pg), 1) < lax.broadcasted_iota(I32, (n_pg, n_pg), 0))
    ltri = jnp.where(strict_lower, 1.0, 0.0).astype(BF16)
    in_page = lax.broadcasted_iota(I32, (LANES, LANES), 0).astype(BF16)
    slot_p = lax.broadcasted_iota(I32, (topk, n_pg), 0).astype(F32)
    slot_l = lax.broadcasted_iota(I32, (topk, LANES), 0).astype(F32)
    rows_t = lambda x: jnp.concatenate([x.T] * reps, axis=0)

    for b, t in [(b, t) for b in range(key_ref.shape[0]) for t in range(n_tok)]:
        hit = jnp.where(key_ref[b, t] >= thr_ref[b, t:t + 1, :], 1.0, 0.0)
        lr = _cumsum_lanes(hit)
        cnt = jnp.broadcast_to(lr[:, LANES - 1:], (n_pg, LANES))
        off = _dot(ltri, cnt.astype(BF16))
        total = off[n_pg - 1:, :] + cnt[n_pg - 1:, :]
        off_t, cnt_t = rows_t(off), rows_t(cnt)
        owner = jnp.where(jnp.logical_and(off_t <= slot_p, slot_p < off_t + cnt_t), 1.0, 0.0).astype(BF16)
        to_slot = lambda x: _dot(owner, x.astype(BF16))
        rank_in_page = slot_l - to_slot(off) + 1.0
        pick = jnp.logical_and(to_slot(hit) > 0.5, to_slot(lr) == rank_in_page)
        pos_in_page = _dot(jnp.where(pick, 1.0, 0.0).astype(BF16), in_page)
        valid = slot_l < total
        pos = to_slot(pg) * PAGE_SIZE + pos_in_page
        phys = (_dot(owner, pthi_ref[b]) * PT_SPLIT + _dot(owner, ptlo_ref[b])) * PAGE_SIZE + pos_in_page
        phys_o[b, t] = jnp.where(valid, phys, 0.0).astype(I32)[:, :1]
        pos_o[b, t] = jnp.where(valid, pos, -1.0).astype(I32)[:, :1]


def _smp_compact(keys4, thr, pt_hi, pt_lo, topk):
    s, n_tok, n_pg, _ = keys4.shape
    bs = _pick_tile(s, 2)
    seq = lambda *tail: pl.BlockSpec((bs,) + tail, lambda si: (si,) + (0,) * len(tail))
    out = jax.ShapeDtypeStruct((s, n_tok, topk, 1), I32)
    return pl.pallas_call(
        functools.partial(_smp_compact_kernel, n_tok=n_tok, topk=topk), out_shape=[out, out], grid=(s // bs,),
        in_specs=[seq(n_tok, n_pg, LANES), seq(n_tok, LANES), seq(n_pg, LANES), seq(n_pg, LANES)],
        out_specs=[seq(n_tok, topk, 1), seq(n_tok, topk, 1)],
        compiler_params=_cparams(("arbitrary",)), name="smp_compact",
    )(keys4, thr, pt_hi, pt_lo)


def _sc_gather_rows(table, idx):
    n, d = idx.shape[0], table.shape[1]
    half = n // SC_GATHER_WINDOW // 2
    mesh = plsc.VectorSubcoreMesh(core_axis_name="core", subcore_axis_name="subcore")

    @functools.partial(pl.kernel, out_type=jax.ShapeDtypeStruct((n, d), table.dtype), mesh=mesh)
    def gather(x_hbm, i_hbm, o_hbm):
        def body(i_vmem, o_vmem):
            pltpu.sync_copy(x_hbm.at[i_vmem.at[0]], o_vmem)

        pltpu.emit_pipeline(
            body, grid=(2, half),
            in_specs=[pl.BlockSpec((1, SC_GATHER_WINDOW), index_map=lambda c, i: (0, c * half + i))],
            out_specs=[pl.BlockSpec((SC_GATHER_WINDOW, d), index_map=lambda c, i: (c * half + i, 0))],
            core_axis_name=("core", "subcore"), dimension_semantics=(pltpu.PARALLEL, pltpu.PARALLEL),
        )(i_hbm, o_hbm)

    return gather(table, idx.reshape(1, n))


def _smp_attn_g_kernel(rb_ref, q_ref, g_ref, kg_ref, vg_ref, pos_ref, knew_ref, vnew_ref, keynew_ref, thr_ref,
                       tab_ref, o_ref, *, n_tok, n_kv, n_past):
    heads = q_ref.shape[1] // n_tok
    grp = heads // n_kv
    topk = pos_ref.shape[2]
    row_kv = lax.broadcasted_iota(I32, (heads, LANES), 0) // grp
    lane = lax.broadcasted_iota(I32, (1, LANES), 1)

    def by_kv(x):
        return jnp.concatenate([jnp.where(row_kv[:, :1] == kv, x, jnp.zeros_like(x)) for kv in range(n_kv)], axis=1)

    def new_rows(ref, kv):
        return ref[0, pl.ds(kv, PAGE_SIZE, stride=n_kv), :].astype(BF16)

    knew = jnp.concatenate([new_rows(knew_ref, kv) for kv in range(n_kv)], axis=1)
    vnew = jnp.concatenate([new_rows(vnew_ref, kv) for kv in range(n_kv)], axis=0)

    for t in range(n_tok):
        rows = slice(t * heads, (t + 1) * heads)
        blk = lambda ref, kv: ref[(t * n_kv + kv) * topk:(t * n_kv + kv + 1) * topk, :].astype(BF16)
        qcat = by_kv(q_ref[0, rows, :])
        pos = pos_ref[0, t:t + 1, :]
        dist = n_past + t - pos
        bias = jnp.concatenate([_bias_from_dist(dist, rb_ref, h) for h in range(heads)], axis=0)
        s_g = _dot_nt(qcat, jnp.concatenate([blk(kg_ref, kv) for kv in range(n_kv)], axis=1)) + bias
        s_g = jnp.where(pos >= 0, s_g, NEG_INF)
        sel_n = jnp.logical_and(keynew_ref[0, t:t + 1, :] >= thr_ref[0, t:t + 1, :], lane <= t)
        s_n = jnp.where(sel_n, _dot_nt(qcat, knew) + tab_ref[2, rows, :], NEG_INF)
        m = jnp.maximum(jnp.max(s_g, axis=-1, keepdims=True), jnp.max(s_n, axis=-1, keepdims=True))
        p_g, p_n = jnp.exp(s_g - m), jnp.exp(s_n - m)
        denom = jnp.sum(p_g, axis=-1, keepdims=True) + jnp.sum(p_n, axis=-1, keepdims=True)
        o = _dot(by_kv(p_g).astype(BF16), jnp.concatenate([blk(vg_ref, kv) for kv in range(n_kv)], axis=0))
        o = o + _dot(by_kv(p_n).astype(BF16), vnew)
        o_ref[0, rows, :] = ((o / denom) * g_ref[0, rows, :]).astype(BF16)


def _smp_attn_g(rel_bias, q_s, g_s, kg, vg, posv, k_new, v_new, keys_new, thr, stab, n_past):
    s, rows, _ = q_s.shape
    n_tok, topk = posv.shape[1], posv.shape[2]
    seq = lambda *tail: pl.BlockSpec((1,) + tail, lambda si: (si,) + (0,) * len(tail))
    gathered = pl.BlockSpec((n_tok * KV_DSA * topk, HEAD_DIM), lambda si: (si, 0))
    return pl.pallas_call(
        functools.partial(_smp_attn_g_kernel, n_tok=n_tok, n_kv=KV_DSA, n_past=n_past),
        out_shape=jax.ShapeDtypeStruct((s, rows, HEAD_DIM), BF16), grid=(s,),
        in_specs=[pl.BlockSpec(memory_space=pltpu.SMEM), seq(rows, HEAD_DIM), seq(rows, HEAD_DIM), gathered, gathered,
                  seq(n_tok, topk), seq(PAGE_SIZE * KV_DSA, HEAD_DIM), seq(PAGE_SIZE * KV_DSA, HEAD_DIM),
                  seq(n_tok, LANES), seq(n_tok, LANES), pl.BlockSpec((3, rows, LANES), lambda si: (0, 0, 0))],
        out_specs=seq(rows, HEAD_DIM),
        compiler_params=_cparams(("arbitrary",)), name="smp_attn_gathered",
    )(rel_bias, q_s, g_s, kg, vg, posv, k_new, v_new, keys_new, thr, stab)


def _out_kernel(x_ref, gf_ref, gd_ref, p_ref, wo_ref, wpg_ref, wple_ref, gple_ref, o_ref):
    h = x_ref[...] + _dot(gf_ref[...], wo_ref[:W_HALF, :]) + _dot(gd_ref[...], wo_ref[W_HALF:, :])
    gate = _sigmoid(_dot(h.astype(BF16), wpg_ref[...]))
    e = _dot(p_ref[...].astype(BF16), wple_ref[...])
    e = e * lax.rsqrt(jnp.mean(e * e, axis=-1, keepdims=True) + EPS) * gple_ref[...]
    o_ref[...] = h + gate * e


def _out(x2, gf, gd, p2, wo, wpg, wple, gple, tm):
    rows = x2.shape[0]
    const = lambda shape: pl.BlockSpec(shape, lambda i: (0, 0), pipeline_mode=pl.Buffered(1))
    rmap = lambda i: (i, 0)
    return pl.pallas_call(
        _out_kernel, out_shape=jax.ShapeDtypeStruct((rows, D_MODEL), F32), grid=(rows // tm,),
        in_specs=[pl.BlockSpec((tm, D_MODEL), rmap), pl.BlockSpec((tm, W_HALF), rmap), pl.BlockSpec((tm, W_HALF), rmap),
                  pl.BlockSpec((tm, PLE_DIM), rmap),
                  const((D_MODEL, D_MODEL)), const((D_MODEL, D_MODEL)), const((PLE_DIM, D_MODEL)), const((1, D_MODEL))],
        out_specs=pl.BlockSpec((tm, D_MODEL), rmap),
        compiler_params=_cparams(("arbitrary",)), name="out",
    )(x2, gf, gd, p2, wo, wpg, wple, gple)


def _prep_w_in(w):
    wt = w.T
    points = [sum(SPLIT_SIZES[:i + 1]) for i in range(len(SPLIT_SIZES) - 1)]
    fq, fk, fv, ff, fz, dq, dk, dv, dz, iq, ik, iw = jnp.split(wt, points, axis=0)
    main = jnp.concatenate([fq, fk, fv, fz, dq, dk, dv, dz, iq], axis=0).astype(BF16)
    pad = jnp.zeros((LANES - D_IDX - H_IDX - H_FOX, w.shape[0]), w.dtype)
    small = jnp.concatenate([ik, iw, ff, pad], axis=0).astype(BF16)
    return main, small


def _pick_tile(n, pref):
    t = pref
    while n % t:
        t //= 2
    return t


def kernel(x_prompt, x_sample, cache_fox_k, cache_fox_v, cache_fox_logf, cache_dsa_k, cache_dsa_v, cache_idx_k,
           page_table, p_prompt, p_sample, rel_bias, norm_in, w_in, b_f, q_norm_fox, k_norm_fox, q_norm_dsa,
           k_norm_dsa, w_out, w_ple, ple_norm, w_pg):
    b, t, _ = x_prompt.shape
    s, n_tok, _ = x_sample.shape
    n_pages = page_table.shape[1]
    n_past = n_pages * PAGE_SIZE
    n_pool = cache_fox_k.shape[1]
    assert cache_fox_k.shape[0] == 1 and n_tok * H_FOX == 32

    w_main, w_small = _prep_w_in(w_in[0])
    nin = norm_in[0].reshape(1, D_MODEL)
    vec = lambda a: a[0].reshape(1, HEAD_DIM)
    gains = (vec(q_norm_fox), vec(k_norm_fox), vec(q_norm_dsa), vec(k_norm_dsa))
    bf = b_f[0].reshape(H_FOX, 1)
    wo = w_out[0].astype(BF16)
    wpg = w_pg[0].astype(BF16)
    wple = w_ple[0].astype(BF16)
    gple = ple_norm[0].reshape(1, D_MODEL)

    tq_d = _pick_tile(t, 256)
    ptab, stab = _bias_tables(rel_bias, tq_d)

    rows_p = b * t
    xp = x_prompt.reshape(rows_p, D_MODEL)
    tm_p = _pick_tile(rows_p, 512)
    (fq16, fk, fk16, fv, fv16, fg, dq16, dk, dk16, dv, dv16, dg, iq16, small, logft) = _proj(
        xp, nin, w_main, w_small, *gains, bf, tm_p, 2 if rows_p % (2 * tm_p) == 0 else 1)
    tq_f = _pick_tile(t, 256)
    ct = _cumsum_prompt(logft, t)
    c4 = ct.reshape(KV_FOX, 2, rows_p // tq_f, tq_f)
    gf = _fox_prompt(fq16, fk16, fv16, c4, fg, b, t, tq_f)
    ik16 = small[:, SM_IK:SM_IK + D_IDX].astype(BF16)
    ik2 = jnp.concatenate([ik16, ik16], axis=1)
    topk_p = min(TOPK_MAX, t // 4)
    gd = _dsa_prompt(rel_bias, iq16, ik2, small, dq16, dk16, dv16, dg, ptab, b, t, tq_d, topk_p)
    y_p = _out(xp, gf, gd, p_prompt[0].reshape(rows_p, PLE_DIM), wo, wpg, wple, gple, _pick_tile(rows_p, 256))

    rows_s = s * n_tok
    xs = x_sample.reshape(rows_s, D_MODEL)
    (sfq16, sfk, _, sfv, _, sfg, sdq16, sdk, _, sdv, _, sdg, siq16, ssmall, slogft) = _proj(
        xs, nin, w_main, w_small, *gains, bf, _pick_tile(rows_s, 512), 1)
    pps = _pick_tile(n_pages, 16)
    rows_q = n_tok * H_FOX

    def new_page(a):
        a = a.reshape(s, n_tok * KV_FOX, HEAD_DIM)
        return jnp.pad(a, ((0, 0), (0, (PAGE_SIZE - n_tok) * KV_FOX), (0, 0)))

    q_rows = lambda a: a.reshape(s, rows_q, HEAD_DIM)
    pool = lambda c: c[0].reshape(n_pool, PAGE_SIZE * KV_FOX, HEAD_DIM)
    lf_pool = jnp.transpose(cache_fox_logf[0], (0, 2, 1))
    lf_new = jnp.pad(jnp.transpose(slogft.reshape(H_FOX, s, n_tok), (1, 0, 2)), ((0, 0), (0, 0), (0, PAGE_SIZE - n_tok)))
    sgf = _smp_fox(page_table, q_rows(sfq16), q_rows(sfg), pool(cache_fox_k), pool(cache_fox_v), lf_pool,
                   new_page(sfk), new_page(sfv), lf_new, pps)

    iq_s = siq16.reshape(s, n_tok * H_IDX, D_IDX)
    w_s = (ssmall[:, SM_IW:SM_IW + H_IDX] * (H_IDX ** -0.5)).reshape(s, n_tok * H_IDX, 1)
    ik_new = jnp.pad(jnp.transpose(ssmall[:, SM_IK:SM_IK + D_IDX].reshape(s, n_tok, D_IDX), (0, 2, 1)),
                     ((0, 0), (0, 0), (0, PAGE_SIZE - n_tok)))
    keys, keys_new = _smp_scores(page_table, iq_s, w_s, jnp.transpose(cache_idx_k[0], (0, 2, 1)), ik_new,
                                 _pick_tile(n_pages, 64))
    topk_s = min(TOPK_MAX, (n_past + n_tok) // 4)
    thr = _smp_thresh(keys.reshape(rows_s, n_past), keys_new.reshape(rows_s, LANES), topk_s).reshape(s, n_tok, LANES)
    assert n_pool <= 256 * PT_SPLIT and topk_s % LANES == 0
    pt_col = lambda a: jnp.broadcast_to(a.astype(BF16)[:, :, None], (s, n_pages, LANES))
    phys, posv = _smp_compact(keys.reshape(s, n_tok, n_pages, PAGE_SIZE), thr,
                              pt_col(page_table // PT_SPLIT), pt_col(page_table % PT_SPLIT), topk_s)
    idx = (phys.reshape(s, n_tok, 1, topk_s) * KV_DSA + jnp.arange(KV_DSA, dtype=I32).reshape(1, 1, KV_DSA, 1)).reshape(-1)
    kv_rows = lambda c: c[0].reshape(n_pool * PAGE_SIZE * KV_DSA, HEAD_DIM)
    kg = _sc_gather_rows(kv_rows(cache_dsa_k), idx)
    vg = _sc_gather_rows(kv_rows(cache_dsa_v), idx)
    sgd = _smp_attn_g(rel_bias, q_rows(sdq16), q_rows(sdg), kg, vg, posv.reshape(s, n_tok, topk_s),
                      new_page(sdk), new_page(sdv), keys_new, thr, stab, n_past)
    y_s = _out(xs, sgf.reshape(rows_s, W_HALF), sgd.reshape(rows_s, W_HALF), p_sample[0].reshape(rows_s, PLE_DIM),
               wo, wpg, wple, gple, _pick_tile(rows_s, 256))

    def kv5(a, bb, tt):
        return a.reshape(1, bb, tt, KV_FOX, HEAD_DIM)

    def outs(bb, tt, fk_, fv_, logft_, dk_, dv_, small_):
        return (kv5(fk_, bb, tt), kv5(fv_, bb, tt), logft_.T.reshape(1, bb, tt, H_FOX),
                kv5(dk_, bb, tt), kv5(dv_, bb, tt), small_[:, SM_IK:SM_IK + D_IDX].reshape(1, bb, tt, D_IDX))

    return ((y_p.reshape(b, t, D_MODEL), y_s.reshape(s, n_tok, D_MODEL))
            + outs(b, t, fk, fv, logft, dk, dv, small)
            + outs(s, n_tok, sfk, sfv, slogft, sdk, sdv, ssmall))
```

```python
import functools
import math

import jax
import jax.numpy as jnp
from jax import lax
from jax.experimental import pallas as pl
from jax.experimental.pallas import tpu as pltpu
from jax.experimental.pallas import tpu_sc as plsc

F32 = jnp.float32
BF16 = jnp.bfloat16
I32 = jnp.int32

D_MODEL = 2048
HEAD_DIM = 128
H_FOX = 8
KV_FOX = 4
H_DSA = 8
KV_DSA = 4
H_IDX = 16
D_IDX = 64
TOPK_MAX = 256
N_BUCKETS = 32
MAX_DISTANCE = 128
PLE_DIM = 256
PAGE_SIZE = 128
EPS = 1e-6
W_HALF = H_FOX * HEAD_DIM
W_KV = KV_FOX * HEAD_DIM
SPLIT_SIZES = (W_HALF, W_KV, W_KV, H_FOX, W_HALF, W_HALF, W_KV, W_KV, W_HALF, H_IDX * D_IDX, D_IDX, H_IDX)

LANES = 128
INT_MIN = -(2 ** 31)
NEG_INF = float("-inf")
VMEM_LIMIT = 56 * 1024 * 1024
SC_GATHER_WINDOW = 128
PT_SPLIT = 64

PROJ_TN = W_HALF
PROJ_OUTS = dict(fq=1, fkv=4, fz=1, dq=1, dkv=4, dz=1, iq=1, small=2)
PROJ_GAIN_ROW = dict(fq=0, fkv=1, dq=2, dkv=3)
PROJ_CALLS = (("fq", "fkv", "small"), ("dq", "dkv"), ("fz", "dz", "iq"))
SM_IK, SM_IW, SM_FF = 0, D_IDX, D_IDX + H_IDX


def _cparams(sem):
    return pltpu.CompilerParams(dimension_semantics=sem, vmem_limit_bytes=VMEM_LIMIT)


def _dot_nt(a, b):
    return lax.dot_general(a, b, (((1,), (1,)), ((), ())), preferred_element_type=F32)


def _dot(a, b):
    return jnp.dot(a, b, preferred_element_type=F32)


def _log_sigmoid(x):
    return -(jnp.maximum(-x, 0.0) + jnp.log(1.0 + jnp.exp(-jnp.abs(x))))


def _sigmoid(x):
    return 1.0 / (1.0 + jnp.exp(-x))


def _sort_key(x):
    b = pltpu.bitcast(x, I32)
    return b ^ ((b >> 31) & jnp.int32(0x7FFFFFFF))


def _tile_lanes(x, width):
    return x if width == LANES else jnp.concatenate([x] * (width // LANES), axis=1)


def _with_ones(v):
    return jnp.concatenate([v, jnp.ones_like(v)], axis=1)


def _cumsum_lanes(x):
    n = x.shape[-1]
    lane = lax.broadcasted_iota(I32, x.shape, x.ndim - 1)
    k = 1
    while k < n:
        x = x + jnp.where(lane >= k, pltpu.roll(x, k, axis=x.ndim - 1), 0.0)
        k *= 2
    return x


def _rmsnorm_kernel(x_ref, g_ref, o_ref):
    x = x_ref[...]
    ms = jnp.mean(x * x, axis=-1, keepdims=True)
    o_ref[...] = (x * lax.rsqrt(ms + EPS) * g_ref[...]).astype(BF16)


def _rmsnorm_in(x2, nin, tm):
    rows = x2.shape[0]
    return pl.pallas_call(
        _rmsnorm_kernel, out_shape=jax.ShapeDtypeStruct((rows, D_MODEL), BF16), grid=(rows // tm,),
        in_specs=[pl.BlockSpec((tm, D_MODEL), lambda i: (i, 0)), pl.BlockSpec((1, D_MODEL), lambda i: (0, 0))],
        out_specs=pl.BlockSpec((tm, D_MODEL), lambda i: (i, 0)),
        compiler_params=_cparams(("arbitrary",)), name="rmsnorm_in",
    )(x2, nin)


def _proj_kernel(xn_ref, w_ref, ws_ref, gain_ref, bf_ref, *outs, steps):
    j = pl.program_id(1)

    def main():
        return _dot_nt(xn_ref[...], w_ref[...])

    def head_norm(y, name, scale):
        g = gain_ref[PROJ_GAIN_ROW[name]:PROJ_GAIN_ROW[name] + 1, :]
        cols = []
        for c in range(y.shape[1] // HEAD_DIM):
            yh = y[:, c * HEAD_DIM:(c + 1) * HEAD_DIM]
            n = yh * lax.rsqrt(jnp.mean(yh * yh, axis=-1, keepdims=True) + EPS) * g
            cols.append(n * scale if scale != 1.0 else n)
        return jnp.concatenate(cols, axis=1)

    def store_kv(o_ref, y):
        for kv in range(KV_FOX):
            o_ref[pl.ds(kv, y.shape[0], stride=KV_FOX), :] = y[:, kv * HEAD_DIM:(kv + 1) * HEAD_DIM]

    def body(name, o):
        if name in ("fq", "dq"):
            o[0][...] = head_norm(main(), name, HEAD_DIM ** -0.5).astype(BF16)
        elif name in ("fkv", "dkv"):
            y = main()
            n = head_norm(y[:, :W_KV], name, 1.0)
            store_kv(o[0], n)
            o[1][...] = n.astype(BF16)
            store_kv(o[2], y[:, W_KV:])
            o[3][...] = y[:, W_KV:].astype(BF16)
        elif name in ("fz", "dz"):
            y = main()
            o[0][...] = (y * _sigmoid(y)).astype(BF16)
        elif name == "iq":
            o[0][...] = (main() * (D_IDX ** -0.5)).astype(BF16)
        else:
            ys = _dot_nt(xn_ref[...], ws_ref[...])
            o[0][...] = ys
            o[1][...] = _log_sigmoid(ys.T[SM_FF:SM_FF + H_FOX, :] + bf_ref[...])

    pos = 0
    for k, name in enumerate(steps):
        pl.when(j == k)(functools.partial(body, name, outs[pos:pos + PROJ_OUTS[name]]))
        pos += PROJ_OUTS[name]


def _proj(xn, w_seg, w_small, gains, bf, steps, tm):
    rows = xn.shape[0]
    main_steps = [n for n in steps if n != "small"]
    assert "small" not in steps[:-1]
    w = jnp.concatenate([w_seg[n] for n in main_steps], axis=0)
    sds = jax.ShapeDtypeStruct
    kinds = {
        "wide": (sds((rows, W_HALF), BF16), pl.BlockSpec((tm, W_HALF), lambda i, j: (i, 0))),
        "kv32": (sds((rows * KV_FOX, HEAD_DIM), F32), pl.BlockSpec((tm * KV_FOX, HEAD_DIM), lambda i, j: (i, 0))),
        "kv16": (sds((rows, W_KV), BF16), pl.BlockSpec((tm, W_KV), lambda i, j: (i, 0))),
        "small": (sds((rows, LANES), F32), pl.BlockSpec((tm, LANES), lambda i, j: (i, 0))),
        "logft": (sds((H_FOX, rows), F32), pl.BlockSpec((H_FOX, tm), lambda i, j: (0, i))),
    }
    layout = dict(fq=["wide"], dq=["wide"], fz=["wide"], dz=["wide"], iq=["wide"],
                  fkv=["kv32", "kv16", "kv32", "kv16"], dkv=["kv32", "kv16", "kv32", "kv16"],
                  small=["small", "logft"])
    flat = [kinds[kind] for n in steps for kind in layout[n]]
    outs = pl.pallas_call(
        functools.partial(_proj_kernel, steps=tuple(steps)),
        out_shape=[shape for shape, _ in flat], grid=(rows // tm, len(steps)),
        in_specs=[
            pl.BlockSpec((tm, D_MODEL), lambda i, j: (i, 0)),
            pl.BlockSpec((PROJ_TN, D_MODEL), lambda i, j: (jnp.minimum(j, len(main_steps) - 1), 0)),
            pl.BlockSpec((LANES, D_MODEL), lambda i, j: (0, 0)),
            pl.BlockSpec((len(PROJ_GAIN_ROW), HEAD_DIM), lambda i, j: (0, 0)),
            pl.BlockSpec((H_FOX, 1), lambda i, j: (0, 0)),
        ],
        out_specs=[spec for _, spec in flat],
        compiler_params=_cparams(("arbitrary", "arbitrary")), name="proj_" + steps[0],
    )(xn, w, w_small, gains, bf)
    result, pos = {}, 0
    for n in steps:
        result[n] = tuple(outs[pos:pos + PROJ_OUTS[n]])
        pos += PROJ_OUTS[n]
    return result


def _cumsum_kernel(x_ref, o_ref):
    o_ref[...] = _cumsum_lanes(x_ref[...])


def _cumsum_prompt(logft, t):
    rows = logft.shape[1]
    return pl.pallas_call(
        _cumsum_kernel, out_shape=jax.ShapeDtypeStruct(logft.shape, F32), grid=(rows // t,),
        in_specs=[pl.BlockSpec((H_FOX, t), lambda b: (0, b))],
        out_specs=pl.BlockSpec((H_FOX, t), lambda b: (0, b)),
        compiler_params=_cparams(("arbitrary",)), name="cumsum",
    )(logft)


def _fox_kernel(q_ref, k_ref, v_ref, c_ref, g_ref, o_ref, m_s, l_s, acc_s, *, tq):
    qi = pl.program_id(1)
    row = lax.broadcasted_iota(I32, (tq, tq), 0)
    col = lax.broadcasted_iota(I32, (tq, tq), 1)
    m_s[...] = jnp.full(m_s.shape, NEG_INF, F32)
    l_s[...] = jnp.zeros(l_s.shape, F32)
    acc_s[...] = jnp.zeros(acc_s.shape, F32)

    def chunk(j, diag):
        off = pl.multiple_of(j * tq, tq)
        for kv in range(KV_FOX):
            q2 = jnp.concatenate([q_ref[:, (2 * kv + g) * HEAD_DIM:(2 * kv + g + 1) * HEAD_DIM] for g in range(2)], axis=0)
            s = _dot_nt(q2, k_ref[pl.ds(off, tq), kv * HEAD_DIM:(kv + 1) * HEAD_DIM])
            ps, alphas = [], []
            for g in range(2):
                h = 2 * kv + g
                sg = s[g * tq:(g + 1) * tq] - c_ref[kv, g, pl.ds(j, 1), :]
                if diag:
                    sg = jnp.where(col <= row, sg, NEG_INF)
                m_old = m_s[h]
                m_new = jnp.maximum(m_old, jnp.broadcast_to(jnp.max(sg, axis=-1, keepdims=True), (tq, LANES)))
                alphas.append(jnp.exp(m_old - m_new))
                ps.append(jnp.exp(sg - _tile_lanes(m_new, tq)).astype(BF16))
                m_s[h] = m_new
            v = v_ref[pl.ds(off, tq), kv * HEAD_DIM:(kv + 1) * HEAD_DIM]
            pv = _dot(jnp.concatenate(ps, axis=0), _with_ones(v))
            for g in range(2):
                h = 2 * kv + g
                l_s[h] = alphas[g] * l_s[h] + pv[g * tq:(g + 1) * tq, HEAD_DIM:]
                acc_s[h] = alphas[g] * acc_s[h] + pv[g * tq:(g + 1) * tq, :HEAD_DIM]

    def off_diag(j, _):
        chunk(j, False)
        return 0

    lax.fori_loop(0, qi, off_diag, 0)
    chunk(qi, True)
    for h in range(H_FOX):
        o = acc_s[h] / l_s[h]
        o_ref[:, h * HEAD_DIM:(h + 1) * HEAD_DIM] = (o * g_ref[:, h * HEAD_DIM:(h + 1) * HEAD_DIM]).astype(BF16)


def _fox_prompt(fq16, fk16, fv16, c4, fg, b, t, tq):
    nq = t // tq
    qmap = lambda bi, qi: (bi * nq + qi, 0)
    bmap = lambda bi, qi: (bi, 0)
    return pl.pallas_call(
        functools.partial(_fox_kernel, tq=tq),
        out_shape=jax.ShapeDtypeStruct((b * t, W_HALF), BF16), grid=(b, nq),
        in_specs=[
            pl.BlockSpec((tq, W_HALF), qmap),
            pl.BlockSpec((t, W_KV), bmap),
            pl.BlockSpec((t, W_KV), bmap),
            pl.BlockSpec((KV_FOX, 2, nq, tq), lambda bi, qi: (0, 0, bi, 0)),
            pl.BlockSpec((tq, W_HALF), qmap),
        ],
        out_specs=pl.BlockSpec((tq, W_HALF), qmap),
        scratch_shapes=[pltpu.VMEM((H_FOX, tq, LANES), F32), pltpu.VMEM((H_FOX, tq, LANES), F32),
                        pltpu.VMEM((H_FOX, tq, HEAD_DIM), F32)],
        compiler_params=_cparams(("arbitrary", "arbitrary")), name="fox_prompt",
    )(fq16, fk16, fv16, c4, fg)


def _t5_bucket(d):
    max_exact = N_BUCKETS // 2
    d = jnp.maximum(d, 0)
    lr = jnp.log(jnp.maximum(d, 1).astype(F32) / max_exact) / math.log(MAX_DISTANCE / max_exact)
    large = jnp.minimum(max_exact + (lr * (N_BUCKETS - max_exact)).astype(I32), N_BUCKETS - 1)
    return jnp.where(d < max_exact, d, large)


def _bias_from_dist(dist, rb_ref, h):
    bucket = _t5_bucket(dist)
    out = jnp.zeros(dist.shape, F32)
    for bkt in range(N_BUCKETS):
        out = jnp.where(bucket == bkt, rb_ref[bkt, h], out)
    return out


def _bias_tab_kernel(rb_ref, ptab_ref, stab_ref, *, tq):
    r = lax.broadcasted_iota(I32, (tq, 2 * tq), 0)
    c = lax.broadcasted_iota(I32, (tq, 2 * tq), 1)
    for h in range(H_DSA):
        ptab_ref[h] = _bias_from_dist(r - c + tq, rb_ref, h)
    rows = 4 * H_DSA
    rr = lax.broadcasted_iota(I32, (rows, LANES), 0)
    pos = lax.broadcasted_iota(I32, (rows, LANES), 1)
    t = rr // H_DSA
    hh = rr % H_DSA
    dists = (jnp.full((rows, LANES), 2 * MAX_DISTANCE, I32), PAGE_SIZE + t - pos, t - pos)
    for k, dist in enumerate(dists):
        acc = jnp.zeros((rows, LANES), F32)
        for h in range(H_DSA):
            acc = jnp.where(hh == h, _bias_from_dist(dist, rb_ref, h), acc)
        stab_ref[k] = acc


def _bias_tables(rel_bias, tq):
    return pl.pallas_call(
        functools.partial(_bias_tab_kernel, tq=tq),
        out_shape=[jax.ShapeDtypeStruct((H_DSA, tq, 2 * tq), F32),
                   jax.ShapeDtypeStruct((3, 4 * H_DSA, LANES), F32)],
        in_specs=[pl.BlockSpec(memory_space=pltpu.SMEM)],
        name="bias_tables",
    )(rel_bias)


def _kth_largest_key(count_ge, shape, k):
    def step(i, cur):
        cand = cur + lax.shift_left(jnp.int32(1), jnp.int32(31) - i)
        return jnp.where(count_ge(cand) >= k, cand, cur)
    return lax.fori_loop(0, 32, step, jnp.full(shape, INT_MIN, I32))


def _dsa_kernel(rb_ref, iq_ref, ik2_ref, sm_ref, q_ref, k_ref, v_ref, g_ref, tab_ref, o_ref,
                key_s, keyt_s, wb_s, m_s, l_s, acc_s, *, tq, topk):
    qi = pl.program_id(1)
    n_pairs = H_IDX // 2
    row = lax.broadcasted_iota(I32, (tq, tq), 0)
    col = lax.broadcasted_iota(I32, (tq, tq), 1)
    lane = lax.broadcasted_iota(I32, (tq, LANES), 1)

    w = sm_ref[:, SM_IW:SM_IW + H_IDX] * (H_IDX ** -0.5)
    for h in range(H_IDX):
        wb_s[h] = jnp.broadcast_to(w[:, h:h + 1], (tq, LANES))
    iqs = jnp.concatenate([iq_ref[:, p * LANES:(p + 1) * LANES] for p in range(n_pairs)], axis=0)

    def score_chunk(j, _):
        off = pl.multiple_of(j * tq, tq)
        ik2 = ik2_ref[pl.ds(off, tq), :]
        rhs = jnp.concatenate([jnp.where(lane < D_IDX, ik2, 0), jnp.where(lane >= D_IDX, ik2, 0)], axis=0)
        s2 = _dot_nt(iqs, rhs)
        sc = jnp.zeros((tq, tq), F32)
        for p in range(n_pairs):
            for e in range(2):
                r = jnp.maximum(s2[p * tq:(p + 1) * tq, e * tq:(e + 1) * tq], 0.0)
                sc = sc + jnp.concatenate([wb_s[2 * p + e]] * (tq // LANES), axis=1) * r
        key = _sort_key(jnp.where(off + col <= qi * tq + row, sc, NEG_INF))
        key_s[j] = key
        keyt_s[j] = key.T
        return 0

    lax.fori_loop(0, qi + 1, score_chunk, 0)

    def count_ge(cand):
        def body(j, cnt):
            hit = jnp.where(keyt_s[j] >= cand, 1, 0)
            return cnt + jnp.sum(hit.reshape(tq // 8, 8, tq), axis=0)
        cnt = lax.fori_loop(0, qi + 1, body, jnp.zeros((8, tq), I32))
        return jnp.sum(cnt, axis=0, keepdims=True)

    thr_row = _kth_largest_key(count_ge, (1, tq), topk)
    thr_col = jnp.broadcast_to(thr_row, (LANES, tq)).T
    thr = jnp.concatenate([thr_col] * (tq // LANES), axis=1)

    m_s[...] = jnp.full(m_s.shape, NEG_INF, F32)
    l_s[...] = jnp.zeros(l_s.shape, F32)
    acc_s[...] = jnp.zeros(acc_s.shape, F32)

    def attend(j, mode):
        off = pl.multiple_of(j * tq, tq)
        sel = key_s[j] >= thr
        if mode == 2:
            sel = jnp.logical_and(sel, col <= row)
        for kv in range(KV_DSA):
            q2 = jnp.concatenate([q_ref[:, (2 * kv + g) * HEAD_DIM:(2 * kv + g + 1) * HEAD_DIM] for g in range(2)], axis=0)
            s = _dot_nt(q2, k_ref[pl.ds(off, tq), kv * HEAD_DIM:(kv + 1) * HEAD_DIM])
            ps, alphas = [], []
            for g in range(2):
                h = 2 * kv + g
                sg = s[g * tq:(g + 1) * tq]
                if mode == 0:
                    sg = sg + rb_ref[N_BUCKETS - 1, h]
                elif mode == 1:
                    sg = sg + tab_ref[h, :, :tq]
                else:
                    sg = sg + tab_ref[h, :, tq:]
                sg = jnp.where(sel, sg, NEG_INF)
                m_old = m_s[h]
                m_new = jnp.maximum(m_old, jnp.broadcast_to(jnp.max(sg, axis=-1, keepdims=True), (tq, LANES)))
                m_safe = jnp.where(m_new == NEG_INF, 0.0, m_new)
                alphas.append(jnp.exp(m_old - m_safe))
                ps.append(jnp.exp(sg - _tile_lanes(m_safe, tq)).astype(BF16))
                m_s[h] = m_new
            v = v_ref[pl.ds(off, tq), kv * HEAD_DIM:(kv + 1) * HEAD_DIM]
            pv = _dot(jnp.concatenate(ps, axis=0), _with_ones(v))
            for g in range(2):
                h = 2 * kv + g
                l_s[h] = alphas[g] * l_s[h] + pv[g * tq:(g + 1) * tq, HEAD_DIM:]
                acc_s[h] = alphas[g] * acc_s[h] + pv[g * tq:(g + 1) * tq, :HEAD_DIM]

    def far(j, _):
        attend(j, 0)
        return 0

    lax.fori_loop(0, jnp.maximum(qi - 1, 0), far, 0)

    @pl.when(qi >= 1)
    def _():
        attend(qi - 1, 1)

    attend(qi, 2)

    for h in range(H_DSA):
        o = acc_s[h] / l_s[h]
        o_ref[:, h * HEAD_DIM:(h + 1) * HEAD_DIM] = (o * g_ref[:, h * HEAD_DIM:(h + 1) * HEAD_DIM]).astype(BF16)


def _dsa_prompt(rel_bias, iq16, ik2, small, dq16, dk16, dv16, dg, ptab, b, t, tq, topk):
    nq = t // tq
    rows = b * t
    qmap = lambda bi, qi: (bi * nq + qi, 0)
    bmap = lambda bi, qi: (bi, 0)
    return pl.pallas_call(
        functools.partial(_dsa_kernel, tq=tq, topk=topk),
        out_shape=jax.ShapeDtypeStruct((rows, W_HALF), BF16), grid=(b, nq),
        in_specs=[
            pl.BlockSpec(memory_space=pltpu.SMEM),
            pl.BlockSpec((tq, H_IDX * D_IDX), qmap),
            pl.BlockSpec((t, LANES), bmap),
            pl.BlockSpec((tq, LANES), qmap),
            pl.BlockSpec((tq, W_HALF), qmap),
            pl.BlockSpec((t, W_KV), bmap),
            pl.BlockSpec((t, W_KV), bmap),
            pl.BlockSpec((tq, W_HALF), qmap),
            pl.BlockSpec((H_DSA, tq, 2 * tq), lambda bi, qi: (0, 0, 0), pipeline_mode=pl.Buffered(1)),
        ],
        out_specs=pl.BlockSpec((tq, W_HALF), qmap),
        scratch_shapes=[pltpu.VMEM((nq, tq, tq), I32), pltpu.VMEM((nq, tq, tq), I32),
                        pltpu.VMEM((H_IDX, tq, LANES), F32),
                        pltpu.VMEM((H_DSA, tq, LANES), F32), pltpu.VMEM((H_DSA, tq, LANES), F32),
                        pltpu.VMEM((H_DSA, tq, HEAD_DIM), F32)],
        compiler_params=_cparams(("arbitrary", "arbitrary")), name="dsa_prompt",
    )(rel_bias, iq16, ik2, small, dq16, dk16, dv16, dg, ptab)


def _page_specs(shape_tail, pps, new_step_tail):
    nd = len(shape_tail)
    return [pl.BlockSpec((1,) + shape_tail, functools.partial(
        lambda s, p, pt, i: (pt[s, p * pps + i],) + (0,) * nd, i=i)) for i in range(pps)]


def _smp_score_kernel(pt_ref, iq_ref, w_ref, *refs, pps, n_tok):
    ik_refs, iknew_ref, o_ref, onew_ref = refs[:pps], refs[pps], refs[pps + 1], refs[pps + 2]
    p = pl.program_id(1)
    iq = iq_ref[0]
    wcol = w_ref[0]

    def page_scores(ik_t):
        r = jnp.maximum(_dot(iq, ik_t.astype(BF16)), 0.0) * wcol
        return jnp.sum(r.reshape(n_tok, H_IDX, ik_t.shape[1]), axis=1)

    ik_all = jnp.concatenate([ik_refs[i][0] for i in range(pps)], axis=1)
    o_ref[0] = _sort_key(page_scores(ik_all))

    @pl.when(p == pl.num_programs(1) - 1)
    def _():
        sc = page_scores(iknew_ref[0])
        t = lax.broadcasted_iota(I32, (n_tok, LANES), 0)
        pos = lax.broadcasted_iota(I32, (n_tok, LANES), 1)
        onew_ref[0] = _sort_key(jnp.where(pos <= t, sc, NEG_INF))


def _smp_scores(page_table, iq_s, w_s, ik_pool, ik_new, pps):
    s, n_pages = page_table.shape
    n_tok = iq_s.shape[1] // H_IDX
    grid_spec = pltpu.PrefetchScalarGridSpec(
        num_scalar_prefetch=1, grid=(s, n_pages // pps),
        in_specs=[pl.BlockSpec((1, n_tok * H_IDX, D_IDX), lambda si, p, pt: (si, 0, 0)),
                  pl.BlockSpec((1, n_tok * H_IDX, 1), lambda si, p, pt: (si, 0, 0))]
                 + _page_specs((D_IDX, PAGE_SIZE), pps, None)
                 + [pl.BlockSpec((1, D_IDX, PAGE_SIZE), lambda si, p, pt: (si, 0, 0))],
        out_specs=[pl.BlockSpec((1, n_tok, pps * LANES), lambda si, p, pt: (si, 0, p)),
                   pl.BlockSpec((1, n_tok, LANES), lambda si, p, pt: (si, 0, 0))],
    )
    return pl.pallas_call(
        functools.partial(_smp_score_kernel, pps=pps, n_tok=n_tok),
        out_shape=[jax.ShapeDtypeStruct((s, n_tok, n_pages * LANES), I32),
                   jax.ShapeDtypeStruct((s, n_tok, LANES), I32)], grid_spec=grid_spec,
        compiler_params=_cparams(("arbitrary", "arbitrary")), name="smp_scores",
    )(page_table, iq_s, w_s, *([ik_pool] * pps), ik_new)


def _smp_thresh_kernel(key_ref, knew_ref, o_ref, *, topk):
    def count_ge(cand):
        hit = jnp.sum(jnp.where(key_ref[...] >= cand, 1, 0), axis=-1, keepdims=True)
        return hit + jnp.sum(jnp.where(knew_ref[...] >= cand, 1, 0), axis=-1, keepdims=True)

    thr = _kth_largest_key(count_ge, (key_ref.shape[0], 1), topk)
    o_ref[...] = jnp.broadcast_to(thr, o_ref.shape)


def _smp_thresh(keys, keys_new, topk):
    rows, n = keys.shape
    tr = _pick_tile(rows, 32)
    return pl.pallas_call(
        functools.partial(_smp_thresh_kernel, topk=topk),
        out_shape=jax.ShapeDtypeStruct((rows, LANES), I32), grid=(rows // tr,),
        in_specs=[pl.BlockSpec((tr, n), lambda i: (i, 0)), pl.BlockSpec((tr, LANES), lambda i: (i, 0))],
        out_specs=pl.BlockSpec((tr, LANES), lambda i: (i, 0)),
        compiler_params=_cparams(("arbitrary",)), name="smp_thresh",
    )(keys, keys_new)


def _smp_fox_kernel(pt_ref, q_ref, g_ref, *refs, pps, n_tok, n_kv):
    k_refs, v_refs, lf_refs = refs[:pps], refs[pps:2 * pps], refs[2 * pps:3 * pps]
    knew_ref, vnew_ref, lfnew_ref, o_ref, m_s, l_s, acc_s, carry_s = refs[3 * pps:]
    p = pl.program_id(1)
    last = pl.num_programs(1) - 1
    rows = q_ref.shape[1]
    heads = rows // n_tok
    grp = heads // n_kv
    q = q_ref[0]
    rr = lax.broadcasted_iota(I32, (rows, LANES), 0)
    pos = lax.broadcasted_iota(I32, (rows, LANES), 1)
    row_kv = (rr % heads) // grp
    qcat = jnp.concatenate([jnp.where(row_kv == kv, q, jnp.zeros_like(q)) for kv in range(n_kv)], axis=1)

    def heads_of(ref, kv):
        return ref[0, pl.ds(kv, PAGE_SIZE, stride=n_kv), :].astype(BF16)

    @pl.when(p == 0)
    def _():
        m_s[...] = jnp.full(m_s.shape, NEG_INF, F32)
        l_s[...] = jnp.zeros(l_s.shape, F32)
        acc_s[...] = jnp.zeros(acc_s.shape, F32)
        carry_s[...] = jnp.zeros(carry_s.shape, F32)

    def attend(kv_refs, bias, sel):
        kcat = jnp.concatenate([jnp.concatenate([heads_of(k_ref, kv) for kv in range(n_kv)], axis=1)
                                for k_ref, _ in kv_refs], axis=0)
        s = _dot_nt(qcat, kcat) + bias
        if sel is not None:
            s = jnp.where(sel, s, NEG_INF)
        m_old = m_s[...]
        m_new = jnp.maximum(m_old, jnp.max(s, axis=-1, keepdims=True))
        m_safe = jnp.where(m_new == NEG_INF, 0.0, m_new)
        alpha = jnp.exp(m_old - m_safe)
        pr = jnp.exp(s - m_safe)
        l_s[...] = alpha * l_s[...] + jnp.sum(pr, axis=-1, keepdims=True)
        m_s[...] = m_new
        pcat = jnp.concatenate([jnp.where(row_kv == kv, pr[:, i * LANES:(i + 1) * LANES], 0.0).astype(BF16)
                                for i in range(len(kv_refs)) for kv in range(n_kv)], axis=1)
        vcat = jnp.concatenate([heads_of(v_ref, kv) for _, v_ref in kv_refs for kv in range(n_kv)], axis=0)
        acc_s[...] = alpha * acc_s[...] + _dot(pcat, vcat)

    def fox_bias(lf_list):
        c = carry_s[...] + _cumsum_lanes(jnp.concatenate([r[0] for r in lf_list], axis=1))
        carry_s[...] = c[:, c.shape[1] - 1:]
        return -jnp.concatenate([c] * n_tok, axis=0)

    attend(list(zip(k_refs, v_refs)), fox_bias(lf_refs), None)

    @pl.when(p == last)
    def _():
        attend([(knew_ref, vnew_ref)], fox_bias([lfnew_ref]), pos <= rr // heads)
        o_ref[0] = ((acc_s[...] / l_s[...]) * g_ref[0]).astype(BF16)


def _smp_fox(page_table, q_s, g_s, k_pool, v_pool, lf_pool, k_new, v_new, lf_new, pps):
    s, n_pages = page_table.shape
    rows = q_s.shape[1]
    seq3 = lambda a, b: pl.BlockSpec((1, a, b), lambda si, p, pt: (si, 0, 0))
    in_specs = ([seq3(rows, HEAD_DIM), seq3(rows, HEAD_DIM)]
                + _page_specs((PAGE_SIZE * KV_FOX, HEAD_DIM), pps, None) * 2
                + _page_specs((H_FOX, PAGE_SIZE), pps, None)
                + [seq3(PAGE_SIZE * KV_FOX, HEAD_DIM)] * 2 + [seq3(H_FOX, PAGE_SIZE)])
    grid_spec = pltpu.PrefetchScalarGridSpec(
        num_scalar_prefetch=1, grid=(s, n_pages // pps), in_specs=in_specs, out_specs=seq3(rows, HEAD_DIM),
        scratch_shapes=[pltpu.VMEM((rows, 1), F32), pltpu.VMEM((rows, 1), F32), pltpu.VMEM((rows, HEAD_DIM), F32),
                        pltpu.VMEM((H_FOX, 1), F32)])
    return pl.pallas_call(
        functools.partial(_smp_fox_kernel, pps=pps, n_tok=rows // H_FOX, n_kv=KV_FOX),
        out_shape=jax.ShapeDtypeStruct((s, rows, HEAD_DIM), BF16), grid_spec=grid_spec,
        compiler_params=_cparams(("arbitrary", "arbitrary")), name="smp_attn_fox",
    )(page_table, q_s, g_s, *([k_pool] * pps), *([v_pool] * pps), *([lf_pool] * pps), k_new, v_new, lf_new)


def _smp_compact_kernel(key_ref, thr_ref, pthi_ref, ptlo_ref, phys_o, pos_o, *, n_tok, topk):
    n_pg = key_ref.shape[2]
    reps = topk // LANES
    pg = lax.broadcasted_iota(I32, (n_pg, LANES), 0).astype(BF16)
    strict_lower = (lax.broadcasted_iota(I32, (n_pg, n_pg), 1) < lax.broadcasted_iota(I32, (n_pg, n_pg), 0))
    ltri = jnp.where(strict_lower, 1.0, 0.0).astype(BF16)
    in_page = lax.broadcasted_iota(I32, (LANES, LANES), 0).astype(BF16)
    slot_p = lax.broadcasted_iota(I32, (topk, n_pg), 0).astype(F32)
    slot_l = lax.broadcasted_iota(I32, (topk, LANES), 0).astype(F32)
    rows_t = lambda x: jnp.concatenate([x.T] * reps, axis=0)

    for b, t in [(b, t) for b in range(key_ref.shape[0]) for t in range(n_tok)]:
        hit = jnp.where(key_ref[b, t] >= thr_ref[b, t:t + 1, :], 1.0, 0.0)
        lr = _cumsum_lanes(hit)
        cnt = jnp.broadcast_to(lr[:, LANES - 1:], (n_pg, LANES))
        off = _dot(ltri, cnt.astype(BF16))
        total = off[n_pg - 1:, :] + cnt[n_pg - 1:, :]
        off_t, cnt_t = rows_t(off), rows_t(cnt)
        owner = jnp.where(jnp.logical_and(off_t <= slot_p, slot_p < off_t + cnt_t), 1.0, 0.0).astype(BF16)
        to_slot = lambda x: _dot(owner, x.astype(BF16))
        rank_in_page = slot_l - to_slot(off) + 1.0
        pick = jnp.logical_and(to_slot(hit) > 0.5, to_slot(lr) == rank_in_page)
        pos_in_page = _dot(jnp.where(pick, 1.0, 0.0).astype(BF16), in_page)
        valid = slot_l < total
        pos = to_slot(pg) * PAGE_SIZE + pos_in_page
        phys = (_dot(owner, pthi_ref[b]) * PT_SPLIT + _dot(owner, ptlo_ref[b])) * PAGE_SIZE + pos_in_page
        phys_o[b, t] = jnp.where(valid, phys, 0.0).astype(I32)[:, :1]
        pos_o[b, t] = jnp.where(valid, pos, -1.0).astype(I32)[:, :1]


def _smp_compact(keys4, thr, pt_hi, pt_lo, topk):
    s, n_tok, n_pg, _ = keys4.shape
    bs = _pick_tile(s, 2)
    seq = lambda *tail: pl.BlockSpec((bs,) + tail, lambda si: (si,) + (0,) * len(tail))
    out = jax.ShapeDtypeStruct((s, n_tok, topk, 1), I32)
    return pl.pallas_call(
        functools.partial(_smp_compact_kernel, n_tok=n_tok, topk=topk), out_shape=[out, out], grid=(s // bs,),
        in_specs=[seq(n_tok, n_pg, LANES), seq(n_tok, LANES), seq(n_pg, LANES), seq(n_pg, LANES)],
        out_specs=[seq(n_tok, topk, 1), seq(n_tok, topk, 1)],
        compiler_params=_cparams(("arbitrary",)), name="smp_compact",
    )(keys4, thr, pt_hi, pt_lo)


def _sc_gather_rows(table, idx):
    n, d = idx.shape[0], table.shape[1]
    half = n // SC_GATHER_WINDOW // 2
    mesh = plsc.VectorSubcoreMesh(core_axis_name="core", subcore_axis_name="subcore")

    @functools.partial(pl.kernel, out_type=jax.ShapeDtypeStruct((n, d), table.dtype), mesh=mesh)
    def gather(x_hbm, i_hbm, o_hbm):
        def body(i_vmem, o_vmem):
            pltpu.sync_copy(x_hbm.at[i_vmem.at[0]], o_vmem)

        pltpu.emit_pipeline(
            body, grid=(2, half),
            in_specs=[pl.BlockSpec((1, SC_GATHER_WINDOW), index_map=lambda c, i: (0, c * half + i))],
            out_specs=[pl.BlockSpec((SC_GATHER_WINDOW, d), index_map=lambda c, i: (c * half + i, 0))],
            core_axis_name=("core", "subcore"), dimension_semantics=(pltpu.PARALLEL, pltpu.PARALLEL),
        )(i_hbm, o_hbm)

    return gather(table, idx.reshape(1, n))


def _smp_attn_g_kernel(rb_ref, q_ref, g_ref, kg_ref, vg_ref, pos_ref, knew_ref, vnew_ref, keynew_ref, thr_ref,
                       tab_ref, o_ref, *, n_tok, n_kv, n_past):
    heads = q_ref.shape[1] // n_tok
    grp = heads // n_kv
    topk = pos_ref.shape[2]
    row_kv = lax.broadcasted_iota(I32, (heads, LANES), 0) // grp
    lane = lax.broadcasted_iota(I32, (1, LANES), 1)

    def by_kv(x):
        return jnp.concatenate([jnp.where(row_kv[:, :1] == kv, x, jnp.zeros_like(x)) for kv in range(n_kv)], axis=1)

    def new_rows(ref, kv):
        return ref[0, pl.ds(kv, PAGE_SIZE, stride=n_kv), :].astype(BF16)

    knew = jnp.concatenate([new_rows(knew_ref, kv) for kv in range(n_kv)], axis=1)
    vnew = jnp.concatenate([new_rows(vnew_ref, kv) for kv in range(n_kv)], axis=0)

    for t in range(n_tok):
        rows = slice(t * heads, (t + 1) * heads)
        blk = lambda ref, kv: ref[(t * n_kv + kv) * topk:(t * n_kv + kv + 1) * topk, :].astype(BF16)
        qcat = by_kv(q_ref[0, rows, :])
        pos = pos_ref[0, t:t + 1, :]
        dist = n_past + t - pos
        bias = jnp.concatenate([_bias_from_dist(dist, rb_ref, h) for h in range(heads)], axis=0)
        s_g = _dot_nt(qcat, jnp.concatenate([blk(kg_ref, kv) for kv in range(n_kv)], axis=1)) + bias
        s_g = jnp.where(pos >= 0, s_g, NEG_INF)
        sel_n = jnp.logical_and(keynew_ref[0, t:t + 1, :] >= thr_ref[0, t:t + 1, :], lane <= t)
        s_n = jnp.where(sel_n, _dot_nt(qcat, knew) + tab_ref[2, rows, :], NEG_INF)
        m = jnp.maximum(jnp.max(s_g, axis=-1, keepdims=True), jnp.max(s_n, axis=-1, keepdims=True))
        p_g, p_n = jnp.exp(s_g - m), jnp.exp(s_n - m)
        denom = jnp.sum(p_g, axis=-1, keepdims=True) + jnp.sum(p_n, axis=-1, keepdims=True)
        o = _dot(by_kv(p_g).astype(BF16), jnp.concatenate([blk(vg_ref, kv) for kv in range(n_kv)], axis=0))
        o = o + _dot(by_kv(p_n).astype(BF16), vnew)
        o_ref[0, rows, :] = ((o / denom) * g_ref[0, rows, :]).astype(BF16)


def _smp_attn_g(rel_bias, q_s, g_s, kg, vg, posv, k_new, v_new, keys_new, thr, stab, n_past):
    s, rows, _ = q_s.shape
    n_tok, topk = posv.shape[1], posv.shape[2]
    seq = lambda *tail: pl.BlockSpec((1,) + tail, lambda si: (si,) + (0,) * len(tail))
    gathered = pl.BlockSpec((n_tok * KV_DSA * topk, HEAD_DIM), lambda si: (si, 0))
    return pl.pallas_call(
        functools.partial(_smp_attn_g_kernel, n_tok=n_tok, n_kv=KV_DSA, n_past=n_past),
        out_shape=jax.ShapeDtypeStruct((s, rows, HEAD_DIM), BF16), grid=(s,),
        in_specs=[pl.BlockSpec(memory_space=pltpu.SMEM), seq(rows, HEAD_DIM), seq(rows, HEAD_DIM), gathered, gathered,
                  seq(n_tok, topk), seq(PAGE_SIZE * KV_DSA, HEAD_DIM), seq(PAGE_SIZE * KV_DSA, HEAD_DIM),
                  seq(n_tok, LANES), seq(n_tok, LANES), pl.BlockSpec((3, rows, LANES), lambda si: (0, 0, 0))],
        out_specs=seq(rows, HEAD_DIM),
        compiler_params=_cparams(("arbitrary",)), name="smp_attn_gathered",
    )(rel_bias, q_s, g_s, kg, vg, posv, k_new, v_new, keys_new, thr, stab)


def _out_kernel(x_ref, gf_ref, gd_ref, p_ref, wo_ref, wpg_ref, wple_ref, gple_ref, o_ref):
    h = x_ref[...] + _dot(gf_ref[...], wo_ref[:W_HALF, :]) + _dot(gd_ref[...], wo_ref[W_HALF:, :])
    gate = _sigmoid(_dot(h.astype(BF16), wpg_ref[...]))
    e = _dot(p_ref[...].astype(BF16), wple_ref[...])
    e = e * lax.rsqrt(jnp.mean(e * e, axis=-1, keepdims=True) + EPS) * gple_ref[...]
    o_ref[...] = h + gate * e


def _out(x2, gf, gd, p2, wo, wpg, wple, gple, tm):
    rows = x2.shape[0]
    const = lambda shape: pl.BlockSpec(shape, lambda i: (0, 0), pipeline_mode=pl.Buffered(1))
    rmap = lambda i: (i, 0)
    return pl.pallas_call(
        _out_kernel, out_shape=jax.ShapeDtypeStruct((rows, D_MODEL), F32), grid=(rows // tm,),
        in_specs=[pl.BlockSpec((tm, D_MODEL), rmap), pl.BlockSpec((tm, W_HALF), rmap), pl.BlockSpec((tm, W_HALF), rmap),
                  pl.BlockSpec((tm, PLE_DIM), rmap),
                  const((D_MODEL, D_MODEL)), const((D_MODEL, D_MODEL)), const((PLE_DIM, D_MODEL)), const((1, D_MODEL))],
        out_specs=pl.BlockSpec((tm, D_MODEL), rmap),
        compiler_params=_cparams(("arbitrary",)), name="out",
    )(x2, gf, gd, p2, wo, wpg, wple, gple)


def _prep_w_in(w):
    wt = w.T.astype(BF16)
    points = [sum(SPLIT_SIZES[:i + 1]) for i in range(len(SPLIT_SIZES) - 1)]
    fq, fk, fv, ff, fz, dq, dk, dv, dz, iq, ik, iw = jnp.split(wt, points, axis=0)
    seg = dict(fq=fq, fkv=jnp.concatenate([fk, fv], axis=0), fz=fz, dq=dq, dkv=jnp.concatenate([dk, dv], axis=0),
               dz=dz, iq=iq)
    pad = jnp.zeros((LANES - D_IDX - H_IDX - H_FOX, w.shape[0]), BF16)
    return seg, jnp.concatenate([ik, iw, ff, pad], axis=0)


def _pick_tile(n, pref):
    t = pref
    while n % t:
        t //= 2
    return t


def kernel(x_prompt, x_sample, cache_fox_k, cache_fox_v, cache_fox_logf, cache_dsa_k, cache_dsa_v, cache_idx_k,
           page_table, p_prompt, p_sample, rel_bias, norm_in, w_in, b_f, q_norm_fox, k_norm_fox, q_norm_dsa,
           k_norm_dsa, w_out, w_ple, ple_norm, w_pg):
    b, t, _ = x_prompt.shape
    s, n_tok, _ = x_sample.shape
    n_pages = page_table.shape[1]
    n_past = n_pages * PAGE_SIZE
    n_pool = cache_fox_k.shape[1]
    assert cache_fox_k.shape[0] == 1 and n_tok * H_FOX == 32

    w_seg, w_small = _prep_w_in(w_in[0])
    nin = norm_in[0].reshape(1, D_MODEL)
    gains = jnp.concatenate([q_norm_fox, k_norm_fox, q_norm_dsa, k_norm_dsa], axis=0)
    bf = b_f[0].reshape(H_FOX, 1)

    def project(x2, calls, tm):
        xn = _rmsnorm_in(x2, nin, _pick_tile(x2.shape[0], 512))
        res = {}
        for steps in calls:
            res.update(_proj(xn, w_seg, w_small, gains, bf, steps, tm))
        (fq16,), (fk, fk16, fv, fv16), (fg,) = res["fq"], res["fkv"], res["fz"]
        (dq16,), (dk, dk16, dv, dv16), (dg,) = res["dq"], res["dkv"], res["dz"]
        return (fq16, fk, fk16, fv, fv16, fg, dq16, dk, dk16, dv, dv16, dg, res["iq"][0]) + res["small"]
    wo = w_out[0].astype(BF16)
    wpg = w_pg[0].astype(BF16)
    wple = w_ple[0].astype(BF16)
    gple = ple_norm[0].reshape(1, D_MODEL)

    tq_d = _pick_tile(t, 256)
    ptab, stab = _bias_tables(rel_bias, tq_d)

    rows_p = b * t
    xp = x_prompt.reshape(rows_p, D_MODEL)
    (fq16, fk, fk16, fv, fv16, fg, dq16, dk, dk16, dv, dv16, dg, iq16, small, logft) = project(
        xp, PROJ_CALLS, _pick_tile(rows_p, 1024))
    tq_f = _pick_tile(t, 256)
    ct = _cumsum_prompt(logft, t)
    c4 = ct.reshape(KV_FOX, 2, rows_p // tq_f, tq_f)
    gf = _fox_prompt(fq16, fk16, fv16, c4, fg, b, t, tq_f)
    ik16 = small[:, SM_IK:SM_IK + D_IDX].astype(BF16)
    ik2 = jnp.concatenate([ik16, ik16], axis=1)
    topk_p = min(TOPK_MAX, t // 4)
    gd = _dsa_prompt(rel_bias, iq16, ik2, small, dq16, dk16, dv16, dg, ptab, b, t, tq_d, topk_p)
    y_p = _out(xp, gf, gd, p_prompt[0].reshape(rows_p, PLE_DIM), wo, wpg, wple, gple, _pick_tile(rows_p, 256))

    rows_s = s * n_tok
    xs = x_sample.reshape(rows_s, D_MODEL)
    (sfq16, sfk, _, sfv, _, sfg, sdq16, sdk, _, sdv, _, sdg, siq16, ssmall, slogft) = project(
        xs, (tuple(n for steps in PROJ_CALLS for n in steps if n != "small") + ("small",),), _pick_tile(rows_s, 512))
    pps = _pick_tile(n_pages, 16)
    rows_q = n_tok * H_FOX

    def new_page(a):
        a = a.reshape(s, n_tok * KV_FOX, HEAD_DIM)
        return jnp.pad(a, ((0, 0), (0, (PAGE_SIZE - n_tok) * KV_FOX), (0, 0)))

    q_rows = lambda a: a.reshape(s, rows_q, HEAD_DIM)
    pool = lambda c: c[0].reshape(n_pool, PAGE_SIZE * KV_FOX, HEAD_DIM)
    lf_pool = jnp.transpose(cache_fox_logf[0], (0, 2, 1))
    lf_new = jnp.pad(jnp.transpose(slogft.reshape(H_FOX, s, n_tok), (1, 0, 2)), ((0, 0), (0, 0), (0, PAGE_SIZE - n_tok)))
    sgf = _smp_fox(page_table, q_rows(sfq16), q_rows(sfg), pool(cache_fox_k), pool(cache_fox_v), lf_pool,
                   new_page(sfk), new_page(sfv), lf_new, pps)

    iq_s = siq16.reshape(s, n_tok * H_IDX, D_IDX)
    w_s = (ssmall[:, SM_IW:SM_IW + H_IDX] * (H_IDX ** -0.5)).reshape(s, n_tok * H_IDX, 1)
    ik_new = jnp.pad(jnp.transpose(ssmall[:, SM_IK:SM_IK + D_IDX].reshape(s, n_tok, D_IDX), (0, 2, 1)),
                     ((0, 0), (0, 0), (0, PAGE_SIZE - n_tok)))
    keys, keys_new = _smp_scores(page_table, iq_s, w_s, jnp.transpose(cache_idx_k[0], (0, 2, 1)), ik_new,
                                 _pick_tile(n_pages, 64))
    topk_s = min(TOPK_MAX, (n_past + n_tok) // 4)
    thr = _smp_thresh(keys.reshape(rows_s, n_past), keys_new.reshape(rows_s, LANES), topk_s).reshape(s, n_tok, LANES)
    assert n_pool <= 256 * PT_SPLIT and topk_s % LANES == 0
    pt_col = lambda a: jnp.broadcast_to(a.astype(BF16)[:, :, None], (s, n_pages, LANES))
    phys, posv = _smp_compact(keys.reshape(s, n_tok, n_pages, PAGE_SIZE), thr,
                              pt_col(page_table // PT_SPLIT), pt_col(page_table % PT_SPLIT), topk_s)
    idx = (phys.reshape(s, n_tok, 1, topk_s) * KV_DSA + jnp.arange(KV_DSA, dtype=I32).reshape(1, 1, KV_DSA, 1)).reshape(-1)
    kv_rows = lambda c: c[0].reshape(n_pool * PAGE_SIZE * KV_DSA, HEAD_DIM)
    kg = _sc_gather_rows(kv_rows(cache_dsa_k), idx)
    vg = _sc_gather_rows(kv_rows(cache_dsa_v), idx)
    sgd = _smp_attn_g(rel_bias, q_rows(sdq16), q_rows(sdg), kg, vg, posv.reshape(s, n_tok, topk_s),
                      new_page(sdk), new_page(sdv), keys_new, thr, stab, n_past)
    y_s = _out(xs, sgf.reshape(rows_s, W_HALF), sgd.reshape(rows_s, W_HALF), p_sample[0].reshape(rows_s, PLE_DIM),
               wo, wpg, wple, gple, _pick_tile(rows_s, 256))

    def kv5(a, bb, tt):
        return a.reshape(1, bb, tt, KV_FOX, HEAD_DIM)

    def outs(bb, tt, fk_, fv_, logft_, dk_, dv_, small_):
        return (kv5(fk_, bb, tt), kv5(fv_, bb, tt), logft_.T.reshape(1, bb, tt, H_FOX),
                kv5(dk_, bb, tt), kv5(dv_, bb, tt), small_[:, SM_IK:SM_IK + D_IDX].reshape(1, bb, tt, D_IDX))

    return ((y_p.reshape(b, t, D_MODEL), y_s.reshape(s, n_tok, D_MODEL))
            + outs(b, t, fk, fv, logft, dk, dv, small)
            + outs(s, n_tok, sfk, sfv, slogft, sdk, sdv, ssmall))
```

```python
import functools
import math

import jax
import jax.numpy as jnp
from jax import lax
from jax.experimental import pallas as pl
from jax.experimental.pallas import tpu as pltpu
from jax.experimental.pallas import tpu_sc as plsc

F32 = jnp.float32
BF16 = jnp.bfloat16
I32 = jnp.int32

D_MODEL = 2048
HEAD_DIM = 128
H_FOX = 8
KV_FOX = 4
H_DSA = 8
KV_DSA = 4
H_IDX = 16
D_IDX = 64
TOPK_MAX = 256
N_BUCKETS = 32
MAX_DISTANCE = 128
PLE_DIM = 256
PAGE_SIZE = 128
EPS = 1e-6
W_HALF = H_FOX * HEAD_DIM
W_KV = KV_FOX * HEAD_DIM
SPLIT_SIZES = (W_HALF, W_KV, W_KV, H_FOX, W_HALF, W_HALF, W_KV, W_KV, W_HALF, H_IDX * D_IDX, D_IDX, H_IDX)

LANES = 128
INT_MIN = -(2 ** 31)
NEG_INF = float("-inf")
VMEM_LIMIT = 56 * 1024 * 1024
SC_GATHER_WINDOW = 128
PT_SPLIT = 64

PROJ_TN = W_HALF
PROJ_OUTS = dict(fq=1, fkv=4, fz=1, dq=1, dkv=4, dz=1, iq=1, small=2)
PROJ_GAIN_ROW = dict(fq=0, fkv=1, dq=2, dkv=3)
PROJ_CALLS = (("fq", "fkv", "small"), ("dq", "dkv"), ("fz", "dz", "iq"))
SM_IK, SM_IW, SM_FF = 0, D_IDX, D_IDX + H_IDX


def _cparams(sem):
    return pltpu.CompilerParams(dimension_semantics=sem, vmem_limit_bytes=VMEM_LIMIT)


def _dot_nt(a, b):
    return lax.dot_general(a, b, (((1,), (1,)), ((), ())), preferred_element_type=F32)


def _dot(a, b):
    return jnp.dot(a, b, preferred_element_type=F32)


def _log_sigmoid(x):
    return -(jnp.maximum(-x, 0.0) + jnp.log(1.0 + jnp.exp(-jnp.abs(x))))


def _sigmoid(x):
    return 1.0 / (1.0 + jnp.exp(-x))


def _sort_key(x):
    b = pltpu.bitcast(x, I32)
    return b ^ ((b >> 31) & jnp.int32(0x7FFFFFFF))


def _tile_lanes(x, width):
    return x if width == LANES else jnp.concatenate([x] * (width // LANES), axis=1)


def _with_ones(v):
    return jnp.concatenate([v, jnp.ones_like(v)], axis=1)


def _cumsum_lanes(x):
    n = x.shape[-1]
    lane = lax.broadcasted_iota(I32, x.shape, x.ndim - 1)
    k = 1
    while k < n:
        x = x + jnp.where(lane >= k, pltpu.roll(x, k, axis=x.ndim - 1), 0.0)
        k *= 2
    return x


def _rmsnorm_kernel(x_ref, g_ref, o_ref):
    x = x_ref[...]
    ms = jnp.mean(x * x, axis=-1, keepdims=True)
    o_ref[...] = (x * lax.rsqrt(ms + EPS) * g_ref[...]).astype(BF16)


def _rmsnorm_in(x2, nin, tm):
    rows = x2.shape[0]
    return pl.pallas_call(
        _rmsnorm_kernel, out_shape=jax.ShapeDtypeStruct((rows, D_MODEL), BF16), grid=(rows // tm,),
        in_specs=[pl.BlockSpec((tm, D_MODEL), lambda i: (i, 0)), pl.BlockSpec((1, D_MODEL), lambda i: (0, 0))],
        out_specs=pl.BlockSpec((tm, D_MODEL), lambda i: (i, 0)),
        compiler_params=_cparams(("arbitrary",)), name="rmsnorm_in",
    )(x2, nin)


def _proj_kernel(xn_ref, w_ref, ws_ref, gain_ref, bf_ref, *outs, steps):
    j = pl.program_id(1)

    def main():
        return _dot_nt(xn_ref[...], w_ref[...])

    def head_norm(y, name, scale):
        g = gain_ref[PROJ_GAIN_ROW[name]:PROJ_GAIN_ROW[name] + 1, :]
        cols = []
        for c in range(y.shape[1] // HEAD_DIM):
            yh = y[:, c * HEAD_DIM:(c + 1) * HEAD_DIM]
            n = yh * lax.rsqrt(jnp.mean(yh * yh, axis=-1, keepdims=True) + EPS) * g
            cols.append(n * scale if scale != 1.0 else n)
        return jnp.concatenate(cols, axis=1)

    def store_kv(o_ref, y):
        for kv in range(KV_FOX):
            o_ref[pl.ds(kv, y.shape[0], stride=KV_FOX), :] = y[:, kv * HEAD_DIM:(kv + 1) * HEAD_DIM]

    def body(name, o):
        if name in ("fq", "dq"):
            o[0][...] = head_norm(main(), name, HEAD_DIM ** -0.5).astype(BF16)
        elif name in ("fkv", "dkv"):
            y = main()
            n = head_norm(y[:, :W_KV], name, 1.0)
            store_kv(o[0], n)
            o[1][...] = n.astype(BF16)
            store_kv(o[2], y[:, W_KV:])
            o[3][...] = y[:, W_KV:].astype(BF16)
        elif name in ("fz", "dz"):
            y = main()
            o[0][...] = (y * _sigmoid(y)).astype(BF16)
        elif name == "iq":
            o[0][...] = (main() * (D_IDX ** -0.5)).astype(BF16)
        else:
            ys = _dot_nt(xn_ref[...], ws_ref[...])
            o[0][...] = ys
            o[1][...] = _log_sigmoid(ys.T[SM_FF:SM_FF + H_FOX, :] + bf_ref[...])

    pos = 0
    for k, name in enumerate(steps):
        pl.when(j == k)(functools.partial(body, name, outs[pos:pos + PROJ_OUTS[name]]))
        pos += PROJ_OUTS[name]


def _proj(xn, w_seg, w_small, gains, bf, steps, tm):
    rows = xn.shape[0]
    main_steps = [n for n in steps if n != "small"]
    assert "small" not in steps[:-1]
    w = jnp.concatenate([w_seg[n] for n in main_steps], axis=0)
    sds = jax.ShapeDtypeStruct
    kinds = {
        "wide": (sds((rows, W_HALF), BF16), pl.BlockSpec((tm, W_HALF), lambda i, j: (i, 0))),
        "kv32": (sds((rows * KV_FOX, HEAD_DIM), F32), pl.BlockSpec((tm * KV_FOX, HEAD_DIM), lambda i, j: (i, 0))),
        "kv16": (sds((rows, W_KV), BF16), pl.BlockSpec((tm, W_KV), lambda i, j: (i, 0))),
        "small": (sds((rows, LANES), F32), pl.BlockSpec((tm, LANES), lambda i, j: (i, 0))),
        "logft": (sds((H_FOX, rows), F32), pl.BlockSpec((H_FOX, tm), lambda i, j: (0, i))),
    }
    layout = dict(fq=["wide"], dq=["wide"], fz=["wide"], dz=["wide"], iq=["wide"],
                  fkv=["kv32", "kv16", "kv32", "kv16"], dkv=["kv32", "kv16", "kv32", "kv16"],
                  small=["small", "logft"])
    flat = [kinds[kind] for n in steps for kind in layout[n]]
    outs = pl.pallas_call(
        functools.partial(_proj_kernel, steps=tuple(steps)),
        out_shape=[shape for shape, _ in flat], grid=(rows // tm, len(steps)),
        in_specs=[
            pl.BlockSpec((tm, D_MODEL), lambda i, j: (i, 0)),
            pl.BlockSpec((PROJ_TN, D_MODEL), lambda i, j: (jnp.minimum(j, len(main_steps) - 1), 0)),
            pl.BlockSpec((LANES, D_MODEL), lambda i, j: (0, 0)),
            pl.BlockSpec((len(PROJ_GAIN_ROW), HEAD_DIM), lambda i, j: (0, 0)),
            pl.BlockSpec((H_FOX, 1), lambda i, j: (0, 0)),
        ],
        out_specs=[spec for _, spec in flat],
        compiler_params=_cparams(("arbitrary", "arbitrary")), name="proj_" + steps[0],
    )(xn, w, w_small, gains, bf)
    result, pos = {}, 0
    for n in steps:
        result[n] = tuple(outs[pos:pos + PROJ_OUTS[n]])
        pos += PROJ_OUTS[n]
    return result


def _cumsum_kernel(x_ref, o_ref):
    o_ref[...] = _cumsum_lanes(x_ref[...])


def _cumsum_prompt(logft, t):
    rows = logft.shape[1]
    return pl.pallas_call(
        _cumsum_kernel, out_shape=jax.ShapeDtypeStruct(logft.shape, F32), grid=(rows // t,),
        in_specs=[pl.BlockSpec((H_FOX, t), lambda b: (0, b))],
        out_specs=pl.BlockSpec((H_FOX, t), lambda b: (0, b)),
        compiler_params=_cparams(("arbitrary",)), name="cumsum",
    )(logft)


def _fox_kernel(q_ref, k_ref, v_ref, c_ref, g_ref, o_ref, m_s, l_s, acc_s, *, tq):
    qi = pl.program_id(1)
    row = lax.broadcasted_iota(I32, (tq, tq), 0)
    col = lax.broadcasted_iota(I32, (tq, tq), 1)
    m_s[...] = jnp.full(m_s.shape, NEG_INF, F32)
    l_s[...] = jnp.zeros(l_s.shape, F32)
    acc_s[...] = jnp.zeros(acc_s.shape, F32)

    def chunk(j, diag):
        off = pl.multiple_of(j * tq, tq)
        for kv in range(KV_FOX):
            q2 = jnp.concatenate([q_ref[:, (2 * kv + g) * HEAD_DIM:(2 * kv + g + 1) * HEAD_DIM] for g in range(2)], axis=0)
            s = _dot_nt(q2, k_ref[pl.ds(off, tq), kv * HEAD_DIM:(kv + 1) * HEAD_DIM])
            ps, alphas = [], []
            for g in range(2):
                h = 2 * kv + g
                sg = s[g * tq:(g + 1) * tq] - c_ref[kv, g, pl.ds(j, 1), :]
                if diag:
                    sg = jnp.where(col <= row, sg, NEG_INF)
                m_old = m_s[h]
                m_new = jnp.maximum(m_old, jnp.broadcast_to(jnp.max(sg, axis=-1, keepdims=True), (tq, LANES)))
                alphas.append(jnp.exp(m_old - m_new))
                ps.append(jnp.exp(sg - _tile_lanes(m_new, tq)).astype(BF16))
                m_s[h] = m_new
            v = v_ref[pl.ds(off, tq), kv * HEAD_DIM:(kv + 1) * HEAD_DIM]
            pv = _dot(jnp.concatenate(ps, axis=0), _with_ones(v))
            for g in range(2):
                h = 2 * kv + g
                l_s[h] = alphas[g] * l_s[h] + pv[g * tq:(g + 1) * tq, HEAD_DIM:]
                acc_s[h] = alphas[g] * acc_s[h] + pv[g * tq:(g + 1) * tq, :HEAD_DIM]

    def off_diag(j, _):
        chunk(j, False)
        return 0

    lax.fori_loop(0, qi, off_diag, 0)
    chunk(qi, True)
    for h in range(H_FOX):
        o = acc_s[h] / l_s[h]
        o_ref[:, h * HEAD_DIM:(h + 1) * HEAD_DIM] = (o * g_ref[:, h * HEAD_DIM:(h + 1) * HEAD_DIM]).astype(BF16)


def _fox_prompt(fq16, fk16, fv16, c4, fg, b, t, tq):
    nq = t // tq
    qmap = lambda bi, qi: (bi * nq + qi, 0)
    bmap = lambda bi, qi: (bi, 0)
    return pl.pallas_call(
        functools.partial(_fox_kernel, tq=tq),
        out_shape=jax.ShapeDtypeStruct((b * t, W_HALF), BF16), grid=(b, nq),
        in_specs=[
            pl.BlockSpec((tq, W_HALF), qmap),
            pl.BlockSpec((t, W_KV), bmap),
            pl.BlockSpec((t, W_KV), bmap),
            pl.BlockSpec((KV_FOX, 2, nq, tq), lambda bi, qi: (0, 0, bi, 0)),
            pl.BlockSpec((tq, W_HALF), qmap),
        ],
        out_specs=pl.BlockSpec((tq, W_HALF), qmap),
        scratch_shapes=[pltpu.VMEM((H_FOX, tq, LANES), F32), pltpu.VMEM((H_FOX, tq, LANES), F32),
                        pltpu.VMEM((H_FOX, tq, HEAD_DIM), F32)],
        compiler_params=_cparams(("arbitrary", "arbitrary")), name="fox_prompt",
    )(fq16, fk16, fv16, c4, fg)


def _t5_bucket(d):
    max_exact = N_BUCKETS // 2
    d = jnp.maximum(d, 0)
    lr = jnp.log(jnp.maximum(d, 1).astype(F32) / max_exact) / math.log(MAX_DISTANCE / max_exact)
    large = jnp.minimum(max_exact + (lr * (N_BUCKETS - max_exact)).astype(I32), N_BUCKETS - 1)
    return jnp.where(d < max_exact, d, large)


def _bias_from_dist(dist, rb_ref, h):
    bucket = _t5_bucket(dist)
    out = jnp.zeros(dist.shape, F32)
    for bkt in range(N_BUCKETS):
        out = jnp.where(bucket == bkt, rb_ref[bkt, h], out)
    return out


def _bias_tab_kernel(rb_ref, ptab_ref, stab_ref, *, tq):
    r = lax.broadcasted_iota(I32, (tq, 2 * tq), 0)
    c = lax.broadcasted_iota(I32, (tq, 2 * tq), 1)
    for h in range(H_DSA):
        ptab_ref[h] = _bias_from_dist(r - c + tq, rb_ref, h)
    rows = 4 * H_DSA
    rr = lax.broadcasted_iota(I32, (rows, LANES), 0)
    pos = lax.broadcasted_iota(I32, (rows, LANES), 1)
    t = rr // H_DSA
    hh = rr % H_DSA
    dists = (jnp.full((rows, LANES), 2 * MAX_DISTANCE, I32), PAGE_SIZE + t - pos, t - pos)
    for k, dist in enumerate(dists):
        acc = jnp.zeros((rows, LANES), F32)
        for h in range(H_DSA):
            acc = jnp.where(hh == h, _bias_from_dist(dist, rb_ref, h), acc)
        stab_ref[k] = acc


def _bias_tables(rel_bias, tq):
    return pl.pallas_call(
        functools.partial(_bias_tab_kernel, tq=tq),
        out_shape=[jax.ShapeDtypeStruct((H_DSA, tq, 2 * tq), F32),
                   jax.ShapeDtypeStruct((3, 4 * H_DSA, LANES), F32)],
        in_specs=[pl.BlockSpec(memory_space=pltpu.SMEM)],
        name="bias_tables",
    )(rel_bias)


def _kth_largest_key(count_ge, shape, k):
    def step(i, cur):
        cand = cur + lax.shift_left(jnp.int32(1), jnp.int32(31) - i)
        return jnp.where(count_ge(cand) >= k, cand, cur)
    return lax.fori_loop(0, 32, step, jnp.full(shape, INT_MIN, I32))


def _dsa_kernel(rb_ref, iq_ref, ik2_ref, sm_ref, q_ref, k_ref, v_ref, g_ref, tab_ref, o_ref,
                key_s, keyt_s, wb_s, m_s, l_s, acc_s, *, tq, topk):
    qi = pl.program_id(1)
    n_pairs = H_IDX // 2
    row = lax.broadcasted_iota(I32, (tq, tq), 0)
    col = lax.broadcasted_iota(I32, (tq, tq), 1)
    lane = lax.broadcasted_iota(I32, (tq, LANES), 1)

    w = sm_ref[:, SM_IW:SM_IW + H_IDX] * (H_IDX ** -0.5)
    for h in range(H_IDX):
        wb_s[h] = jnp.broadcast_to(w[:, h:h + 1], (tq, LANES))
    iqs = jnp.concatenate([iq_ref[:, p * LANES:(p + 1) * LANES] for p in range(n_pairs)], axis=0)

    def score_chunk(j, _):
        off = pl.multiple_of(j * tq, tq)
        ik2 = ik2_ref[pl.ds(off, tq), :]
        rhs = jnp.concatenate([jnp.where(lane < D_IDX, ik2, 0), jnp.where(lane >= D_IDX, ik2, 0)], axis=0)
        s2 = _dot_nt(iqs, rhs)
        sc = jnp.zeros((tq, tq), F32)
        for p in range(n_pairs):
            for e in range(2):
                r = jnp.maximum(s2[p * tq:(p + 1) * tq, e * tq:(e + 1) * tq], 0.0)
                sc = sc + jnp.concatenate([wb_s[2 * p + e]] * (tq // LANES), axis=1) * r
        key = _sort_key(jnp.where(off + col <= qi * tq + row, sc, NEG_INF))
        key_s[j] = key
        keyt_s[j] = key.T
        return 0

    lax.fori_loop(0, qi + 1, score_chunk, 0)

    def count_ge(cand):
        def body(j, cnt):
            hit = jnp.where(keyt_s[j] >= cand, 1, 0)
            return cnt + jnp.sum(hit.reshape(tq // 8, 8, tq), axis=0)
        cnt = lax.fori_loop(0, qi + 1, body, jnp.zeros((8, tq), I32))
        return jnp.sum(cnt, axis=0, keepdims=True)

    thr_row = _kth_largest_key(count_ge, (1, tq), topk)
    thr_col = jnp.broadcast_to(thr_row, (LANES, tq)).T
    thr = jnp.concatenate([thr_col] * (tq // LANES), axis=1)

    m_s[...] = jnp.full(m_s.shape, NEG_INF, F32)
    l_s[...] = jnp.zeros(l_s.shape, F32)
    acc_s[...] = jnp.zeros(acc_s.shape, F32)

    def attend(j, mode):
        off = pl.multiple_of(j * tq, tq)
        sel = key_s[j] >= thr
        if mode == 2:
            sel = jnp.logical_and(sel, col <= row)
        for kv in range(KV_DSA):
            q2 = jnp.concatenate([q_ref[:, (2 * kv + g) * HEAD_DIM:(2 * kv + g + 1) * HEAD_DIM] for g in range(2)], axis=0)
            s = _dot_nt(q2, k_ref[pl.ds(off, tq), kv * HEAD_DIM:(kv + 1) * HEAD_DIM])
            ps, alphas = [], []
            for g in range(2):
                h = 2 * kv + g
                sg = s[g * tq:(g + 1) * tq]
                if mode == 0:
                    sg = sg + rb_ref[N_BUCKETS - 1, h]
                elif mode == 1:
                    sg = sg + tab_ref[h, :, :tq]
                else:
                    sg = sg + tab_ref[h, :, tq:]
                sg = jnp.where(sel, sg, NEG_INF)
                m_old = m_s[h]
                m_new = jnp.maximum(m_old, jnp.broadcast_to(jnp.max(sg, axis=-1, keepdims=True), (tq, LANES)))
                m_safe = jnp.where(m_new == NEG_INF, 0.0, m_new)
                alphas.append(jnp.exp(m_old - m_safe))
                ps.append(jnp.exp(sg - _tile_lanes(m_safe, tq)).astype(BF16))
                m_s[h] = m_new
            v = v_ref[pl.ds(off, tq), kv * HEAD_DIM:(kv + 1) * HEAD_DIM]
            pv = _dot(jnp.concatenate(ps, axis=0), _with_ones(v))
            for g in range(2):
                h = 2 * kv + g
                l_s[h] = alphas[g] * l_s[h] + pv[g * tq:(g + 1) * tq, HEAD_DIM:]
                acc_s[h] = alphas[g] * acc_s[h] + pv[g * tq:(g + 1) * tq, :HEAD_DIM]

    def far(j, _):
        attend(j, 0)
        return 0

    lax.fori_loop(0, jnp.maximum(qi - 1, 0), far, 0)

    @pl.when(qi >= 1)
    def _():
        attend(qi - 1, 1)

    attend(qi, 2)

    for h in range(H_DSA):
        o = acc_s[h] / l_s[h]
        o_ref[:, h * HEAD_DIM:(h + 1) * HEAD_DIM] = (o * g_ref[:, h * HEAD_DIM:(h + 1) * HEAD_DIM]).astype(BF16)


def _dsa_prompt(rel_bias, iq16, ik2, small, dq16, dk16, dv16, dg, ptab, b, t, tq, topk):
    nq = t // tq
    rows = b * t
    qmap = lambda bi, qi: (bi * nq + qi, 0)
    bmap = lambda bi, qi: (bi, 0)
    return pl.pallas_call(
        functools.partial(_dsa_kernel, tq=tq, topk=topk),
        out_shape=jax.ShapeDtypeStruct((rows, W_HALF), BF16), grid=(b, nq),
        in_specs=[
            pl.BlockSpec(memory_space=pltpu.SMEM),
            pl.BlockSpec((tq, H_IDX * D_IDX), qmap),
            pl.BlockSpec((t, LANES), bmap),
            pl.BlockSpec((tq, LANES), qmap),
            pl.BlockSpec((tq, W_HALF), qmap),
            pl.BlockSpec((t, W_KV), bmap),
            pl.BlockSpec((t, W_KV), bmap),
            pl.BlockSpec((tq, W_HALF), qmap),
            pl.BlockSpec((H_DSA, tq, 2 * tq), lambda bi, qi: (0, 0, 0), pipeline_mode=pl.Buffered(1)),
        ],
        out_specs=pl.BlockSpec((tq, W_HALF), qmap),
        scratch_shapes=[pltpu.VMEM((nq, tq, tq), I32), pltpu.VMEM((nq, tq, tq), I32),
                        pltpu.VMEM((H_IDX, tq, LANES), F32),
                        pltpu.VMEM((H_DSA, tq, LANES), F32), pltpu.VMEM((H_DSA, tq, LANES), F32),
                        pltpu.VMEM((H_DSA, tq, HEAD_DIM), F32)],
        compiler_params=_cparams(("arbitrary", "arbitrary")), name="dsa_prompt",
    )(rel_bias, iq16, ik2, small, dq16, dk16, dv16, dg, ptab)


def _page_specs(shape_tail, pps, new_step_tail):
    nd = len(shape_tail)
    return [pl.BlockSpec((1,) + shape_tail, functools.partial(
        lambda s, p, pt, i: (pt[s, p * pps + i],) + (0,) * nd, i=i)) for i in range(pps)]


def _smp_score_kernel(pt_ref, iq_ref, w_ref, *refs, pps, n_tok):
    ik_refs, iknew_ref, o_ref, onew_ref = refs[:pps], refs[pps], refs[pps + 1], refs[pps + 2]
    p = pl.program_id(1)
    iq = iq_ref[0]
    wcol = w_ref[0]

    def page_scores(ik_t):
        r = jnp.maximum(_dot(iq, ik_t.astype(BF16)), 0.0) * wcol
        return jnp.sum(r.reshape(n_tok, H_IDX, ik_t.shape[1]), axis=1)

    ik_all = jnp.concatenate([ik_refs[i][0] for i in range(pps)], axis=1)
    o_ref[0] = _sort_key(page_scores(ik_all))

    @pl.when(p == pl.num_programs(1) - 1)
    def _():
        sc = page_scores(iknew_ref[0])
        t = lax.broadcasted_iota(I32, (n_tok, LANES), 0)
        pos = lax.broadcasted_iota(I32, (n_tok, LANES), 1)
        onew_ref[0] = _sort_key(jnp.where(pos <= t, sc, NEG_INF))


def _smp_scores(page_table, iq_s, w_s, ik_pool, ik_new, pps):
    s, n_pages = page_table.shape
    n_tok = iq_s.shape[1] // H_IDX
    grid_spec = pltpu.PrefetchScalarGridSpec(
        num_scalar_prefetch=1, grid=(s, n_pages // pps),
        in_specs=[pl.BlockSpec((1, n_tok * H_IDX, D_IDX), lambda si, p, pt: (si, 0, 0)),
                  pl.BlockSpec((1, n_tok * H_IDX, 1), lambda si, p, pt: (si, 0, 0))]
                 + _page_specs((D_IDX, PAGE_SIZE), pps, None)
                 + [pl.BlockSpec((1, D_IDX, PAGE_SIZE), lambda si, p, pt: (si, 0, 0))],
        out_specs=[pl.BlockSpec((1, n_tok, pps * LANES), lambda si, p, pt: (si, 0, p)),
                   pl.BlockSpec((1, n_tok, LANES), lambda si, p, pt: (si, 0, 0))],
    )
    return pl.pallas_call(
        functools.partial(_smp_score_kernel, pps=pps, n_tok=n_tok),
        out_shape=[jax.ShapeDtypeStruct((s, n_tok, n_pages * LANES), I32),
                   jax.ShapeDtypeStruct((s, n_tok, LANES), I32)], grid_spec=grid_spec,
        compiler_params=_cparams(("arbitrary", "arbitrary")), name="smp_scores",
    )(page_table, iq_s, w_s, *([ik_pool] * pps), ik_new)


def _smp_thresh_kernel(key_ref, knew_ref, o_ref, *, topk):
    def count_ge(cand):
        hit = jnp.sum(jnp.where(key_ref[...] >= cand, 1, 0), axis=-1, keepdims=True)
        return hit + jnp.sum(jnp.where(knew_ref[...] >= cand, 1, 0), axis=-1, keepdims=True)

    thr = _kth_largest_key(count_ge, (key_ref.shape[0], 1), topk)
    o_ref[...] = jnp.broadcast_to(thr, o_ref.shape)


def _smp_thresh(keys, keys_new, topk):
    rows, n = keys.shape
    tr = _pick_tile(rows, 32)
    return pl.pallas_call(
        functools.partial(_smp_thresh_kernel, topk=topk),
        out_shape=jax.ShapeDtypeStruct((rows, LANES), I32), grid=(rows // tr,),
        in_specs=[pl.BlockSpec((tr, n), lambda i: (i, 0)), pl.BlockSpec((tr, LANES), lambda i: (i, 0))],
        out_specs=pl.BlockSpec((tr, LANES), lambda i: (i, 0)),
        compiler_params=_cparams(("arbitrary",)), name="smp_thresh",
    )(keys, keys_new)


def _smp_fox_kernel(pt_ref, q_ref, g_ref, *refs, pps, n_tok, n_kv):
    k_refs, v_refs, lf_refs = refs[:pps], refs[pps:2 * pps], refs[2 * pps:3 * pps]
    knew_ref, vnew_ref, lfnew_ref, o_ref, m_s, l_s, acc_s, carry_s = refs[3 * pps:]
    p = pl.program_id(1)
    last = pl.num_programs(1) - 1
    rows = q_ref.shape[1]
    heads = rows // n_tok
    grp = heads // n_kv
    q = q_ref[0]
    rr = lax.broadcasted_iota(I32, (rows, LANES), 0)
    pos = lax.broadcasted_iota(I32, (rows, LANES), 1)
    row_kv = (rr % heads) // grp
    qcat = jnp.concatenate([jnp.where(row_kv == kv, q, jnp.zeros_like(q)) for kv in range(n_kv)], axis=1)

    def heads_of(ref, kv):
        return ref[0, pl.ds(kv, PAGE_SIZE, stride=n_kv), :].astype(BF16)

    @pl.when(p == 0)
    def _():
        m_s[...] = jnp.full(m_s.shape, NEG_INF, F32)
        l_s[...] = jnp.zeros(l_s.shape, F32)
        acc_s[...] = jnp.zeros(acc_s.shape, F32)
        carry_s[...] = jnp.zeros(carry_s.shape, F32)

    def attend(kv_refs, bias, sel):
        kcat = jnp.concatenate([jnp.concatenate([heads_of(k_ref, kv) for kv in range(n_kv)], axis=1)
                                for k_ref, _ in kv_refs], axis=0)
        s = _dot_nt(qcat, kcat) + bias
        if sel is not None:
            s = jnp.where(sel, s, NEG_INF)
        m_old = m_s[...]
        m_new = jnp.maximum(m_old, jnp.max(s, axis=-1, keepdims=True))
        m_safe = jnp.where(m_new == NEG_INF, 0.0, m_new)
        alpha = jnp.exp(m_old - m_safe)
        pr = jnp.exp(s - m_safe)
        l_s[...] = alpha * l_s[...] + jnp.sum(pr, axis=-1, keepdims=True)
        m_s[...] = m_new
        pcat = jnp.concatenate([jnp.where(row_kv == kv, pr[:, i * LANES:(i + 1) * LANES], 0.0).astype(BF16)
                                for i in range(len(kv_refs)) for kv in range(n_kv)], axis=1)
        vcat = jnp.concatenate([heads_of(v_ref, kv) for _, v_ref in kv_refs for kv in range(n_kv)], axis=0)
        acc_s[...] = alpha * acc_s[...] + _dot(pcat, vcat)

    def fox_bias(lf_list):
        c = carry_s[...] + _cumsum_lanes(jnp.concatenate([r[0] for r in lf_list], axis=1))
        carry_s[...] = c[:, c.shape[1] - 1:]
        return -jnp.concatenate([c] * n_tok, axis=0)

    attend(list(zip(k_refs, v_refs)), fox_bias(lf_refs), None)

    @pl.when(p == last)
    def _():
        attend([(knew_ref, vnew_ref)], fox_bias([lfnew_ref]), pos <= rr // heads)
        o_ref[0] = ((acc_s[...] / l_s[...]) * g_ref[0]).astype(BF16)


def _smp_fox(page_table, q_s, g_s, k_pool, v_pool, lf_pool, k_new, v_new, lf_new, pps):
    s, n_pages = page_table.shape
    rows = q_s.shape[1]
    seq3 = lambda a, b: pl.BlockSpec((1, a, b), lambda si, p, pt: (si, 0, 0))
    in_specs = ([seq3(rows, HEAD_DIM), seq3(rows, HEAD_DIM)]
                + _page_specs((PAGE_SIZE * KV_FOX, HEAD_DIM), pps, None) * 2
                + _page_specs((H_FOX, PAGE_SIZE), pps, None)
                + [seq3(PAGE_SIZE * KV_FOX, HEAD_DIM)] * 2 + [seq3(H_FOX, PAGE_SIZE)])
    grid_spec = pltpu.PrefetchScalarGridSpec(
        num_scalar_prefetch=1, grid=(s, n_pages // pps), in_specs=in_specs, out_specs=seq3(rows, HEAD_DIM),
        scratch_shapes=[pltpu.VMEM((rows, 1), F32), pltpu.VMEM((rows, 1), F32), pltpu.VMEM((rows, HEAD_DIM), F32),
                        pltpu.VMEM((H_FOX, 1), F32)])
    return pl.pallas_call(
        functools.partial(_smp_fox_kernel, pps=pps, n_tok=rows // H_FOX, n_kv=KV_FOX),
        out_shape=jax.ShapeDtypeStruct((s, rows, HEAD_DIM), BF16), grid_spec=grid_spec,
        compiler_params=_cparams(("arbitrary", "arbitrary")), name="smp_attn_fox",
    )(page_table, q_s, g_s, *([k_pool] * pps), *([v_pool] * pps), *([lf_pool] * pps), k_new, v_new, lf_new)


def _smp_compact_kernel(key_ref, thr_ref, pthi_ref, ptlo_ref, phys_o, pos_o, *, n_tok, topk):
    n_pg = key_ref.shape[2]
    reps = topk // LANES
    pg = lax.broadcasted_iota(I32, (n_pg, LANES), 0).astype(BF16)
    strict_lower = (lax.broadcasted_iota(I32, (n_pg, n_pg), 1) < lax.broadcasted_iota(I32, (n_pg, n_pg), 0))
    ltri = jnp.where(strict_lower, 1.0, 0.0).astype(BF16)
    in_page = lax.broadcasted_iota(I32, (LANES, LANES), 0).astype(BF16)
    slot_p = lax.broadcasted_iota(I32, (topk, n_pg), 0).astype(F32)
    slot_l = lax.broadcasted_iota(I32, (topk, LANES), 0).astype(F32)
    rows_t = lambda x: jnp.concatenate([x.T] * reps, axis=0)

    for b, t in [(b, t) for b in range(key_ref.shape[0]) for t in range(n_tok)]:
        hit = jnp.where(key_ref[b, t] >= thr_ref[b, t:t + 1, :], 1.0, 0.0)
        lr = _cumsum_lanes(hit)
        cnt = jnp.broadcast_to(lr[:, LANES - 1:], (n_pg, LANES))
        off = _dot(ltri, cnt.astype(BF16))
        total = off[n_pg - 1:, :] + cnt[n_pg - 1:, :]
        off_t, cnt_t = rows_t(off), rows_t(cnt)
        owner = jnp.where(jnp.logical_and(off_t <= slot_p, slot_p < off_t + cnt_t), 1.0, 0.0).astype(BF16)
        to_slot = lambda x: _dot(owner, x.astype(BF16))
        rank_in_page = slot_l - to_slot(off) + 1.0
        pick = jnp.logical_and(to_slot(hit) > 0.5, to_slot(lr) == rank_in_page)
        pos_in_page = _dot(jnp.where(pick, 1.0, 0.0).astype(BF16), in_page)
        valid = slot_l < total
        pos = to_slot(pg) * PAGE_SIZE + pos_in_page
        phys = (_dot(owner, pthi_ref[b]) * PT_SPLIT + _dot(owner, ptlo_ref[b])) * PAGE_SIZE + pos_in_page
        phys_o[b, t] = jnp.where(valid, phys, 0.0).astype(I32)[:, :1]
        pos_o[b, t] = jnp.where(valid, pos, -1.0).astype(I32)[:, :1]


def _smp_compact(keys4, thr, pt_hi, pt_lo, topk):
    s, n_tok, n_pg, _ = keys4.shape
    bs = _pick_tile(s, 4)
    seq = lambda *tail: pl.BlockSpec((bs,) + tail, lambda si: (si,) + (0,) * len(tail))
    out = jax.ShapeDtypeStruct((s, n_tok, topk, 1), I32)
    return pl.pallas_call(
        functools.partial(_smp_compact_kernel, n_tok=n_tok, topk=topk), out_shape=[out, out], grid=(s // bs,),
        in_specs=[seq(n_tok, n_pg, LANES), seq(n_tok, LANES), seq(n_pg, LANES), seq(n_pg, LANES)],
        out_specs=[seq(n_tok, topk, 1), seq(n_tok, topk, 1)],
        compiler_params=_cparams(("arbitrary",)), name="smp_compact",
    )(keys4, thr, pt_hi, pt_lo)


def _sc_gather_rows(table, idx):
    n, d = idx.shape[0], table.shape[1]
    half = n // SC_GATHER_WINDOW // 2
    mesh = plsc.VectorSubcoreMesh(core_axis_name="core", subcore_axis_name="subcore")

    @functools.partial(pl.kernel, out_type=jax.ShapeDtypeStruct((n, d), table.dtype), mesh=mesh)
    def gather(x_hbm, i_hbm, o_hbm):
        def body(i_vmem, o_vmem):
            pltpu.sync_copy(x_hbm.at[i_vmem.at[0]], o_vmem)

        pltpu.emit_pipeline(
            body, grid=(2, half),
            in_specs=[pl.BlockSpec((1, SC_GATHER_WINDOW), index_map=lambda c, i: (0, c * half + i))],
            out_specs=[pl.BlockSpec((SC_GATHER_WINDOW, d), index_map=lambda c, i: (c * half + i, 0))],
            core_axis_name=("core", "subcore"), dimension_semantics=(pltpu.PARALLEL, pltpu.PARALLEL),
        )(i_hbm, o_hbm)

    return gather(table, idx.reshape(1, n))


def _smp_attn_g_kernel(rb_ref, q_ref, g_ref, kg_ref, vg_ref, pos_ref, knew_ref, vnew_ref, keynew_ref, thr_ref,
                       tab_ref, o_ref, *, n_tok, n_kv, n_past):
    heads = q_ref.shape[1] // n_tok
    grp = heads // n_kv
    topk = pos_ref.shape[2]
    row_kv = lax.broadcasted_iota(I32, (heads, LANES), 0) // grp
    lane = lax.broadcasted_iota(I32, (1, LANES), 1)

    def by_kv(x):
        return jnp.concatenate([jnp.where(row_kv[:, :1] == kv, x, jnp.zeros_like(x)) for kv in range(n_kv)], axis=1)

    def new_rows(ref, kv):
        return ref[0, pl.ds(kv, PAGE_SIZE, stride=n_kv), :].astype(BF16)

    knew = jnp.concatenate([new_rows(knew_ref, kv) for kv in range(n_kv)], axis=1)
    vnew = jnp.concatenate([new_rows(vnew_ref, kv) for kv in range(n_kv)], axis=0)

    for t in range(n_tok):
        rows = slice(t * heads, (t + 1) * heads)
        blk = lambda ref, kv: ref[(t * n_kv + kv) * topk:(t * n_kv + kv + 1) * topk, :].astype(BF16)
        qcat = by_kv(q_ref[0, rows, :])
        pos = pos_ref[0, t:t + 1, :]
        dist = n_past + t - pos
        bias = jnp.concatenate([_bias_from_dist(dist, rb_ref, h) for h in range(heads)], axis=0)
        s_g = _dot_nt(qcat, jnp.concatenate([blk(kg_ref, kv) for kv in range(n_kv)], axis=1)) + bias
        s_g = jnp.where(pos >= 0, s_g, NEG_INF)
        sel_n = jnp.logical_and(keynew_ref[0, t:t + 1, :] >= thr_ref[0, t:t + 1, :], lane <= t)
        s_n = jnp.where(sel_n, _dot_nt(qcat, knew) + tab_ref[2, rows, :], NEG_INF)
        m = jnp.maximum(jnp.max(s_g, axis=-1, keepdims=True), jnp.max(s_n, axis=-1, keepdims=True))
        p_g, p_n = jnp.exp(s_g - m), jnp.exp(s_n - m)
        denom = jnp.sum(p_g, axis=-1, keepdims=True) + jnp.sum(p_n, axis=-1, keepdims=True)
        o = _dot(by_kv(p_g).astype(BF16), jnp.concatenate([blk(vg_ref, kv) for kv in range(n_kv)], axis=0))
        o = o + _dot(by_kv(p_n).astype(BF16), vnew)
        o_ref[0, rows, :] = ((o / denom) * g_ref[0, rows, :]).astype(BF16)


def _smp_attn_g(rel_bias, q_s, g_s, kg, vg, posv, k_new, v_new, keys_new, thr, stab, n_past):
    s, rows, _ = q_s.shape
    n_tok, topk = posv.shape[1], posv.shape[2]
    seq = lambda *tail: pl.BlockSpec((1,) + tail, lambda si: (si,) + (0,) * len(tail))
    gathered = pl.BlockSpec((n_tok * KV_DSA * topk, HEAD_DIM), lambda si: (si, 0))
    return pl.pallas_call(
        functools.partial(_smp_attn_g_kernel, n_tok=n_tok, n_kv=KV_DSA, n_past=n_past),
        out_shape=jax.ShapeDtypeStruct((s, rows, HEAD_DIM), BF16), grid=(s,),
        in_specs=[pl.BlockSpec(memory_space=pltpu.SMEM), seq(rows, HEAD_DIM), seq(rows, HEAD_DIM), gathered, gathered,
                  seq(n_tok, topk), seq(PAGE_SIZE * KV_DSA, HEAD_DIM), seq(PAGE_SIZE * KV_DSA, HEAD_DIM),
                  seq(n_tok, LANES), seq(n_tok, LANES), pl.BlockSpec((3, rows, LANES), lambda si: (0, 0, 0))],
        out_specs=seq(rows, HEAD_DIM),
        compiler_params=_cparams(("arbitrary",)), name="smp_attn_gathered",
    )(rel_bias, q_s, g_s, kg, vg, posv, k_new, v_new, keys_new, thr, stab)


def _out_kernel(x_ref, gf_ref, gd_ref, p_ref, wo_ref, wpg_ref, wple_ref, gple_ref, o_ref):
    h = x_ref[...] + _dot(gf_ref[...], wo_ref[:W_HALF, :]) + _dot(gd_ref[...], wo_ref[W_HALF:, :])
    gate = _sigmoid(_dot(h.astype(BF16), wpg_ref[...]))
    e = _dot(p_ref[...].astype(BF16), wple_ref[...])
    e = e * lax.rsqrt(jnp.mean(e * e, axis=-1, keepdims=True) + EPS) * gple_ref[...]
    o_ref[...] = h + gate * e


def _out(x2, gf, gd, p2, wo, wpg, wple, gple, tm):
    rows = x2.shape[0]
    const = lambda shape: pl.BlockSpec(shape, lambda i: (0, 0), pipeline_mode=pl.Buffered(1))
    rmap = lambda i: (i, 0)
    return pl.pallas_call(
        _out_kernel, out_shape=jax.ShapeDtypeStruct((rows, D_MODEL), F32), grid=(rows // tm,),
        in_specs=[pl.BlockSpec((tm, D_MODEL), rmap), pl.BlockSpec((tm, W_HALF), rmap), pl.BlockSpec((tm, W_HALF), rmap),
                  pl.BlockSpec((tm, PLE_DIM), rmap),
                  const((D_MODEL, D_MODEL)), const((D_MODEL, D_MODEL)), const((PLE_DIM, D_MODEL)), const((1, D_MODEL))],
        out_specs=pl.BlockSpec((tm, D_MODEL), rmap),
        compiler_params=_cparams(("arbitrary",)), name="out",
    )(x2, gf, gd, p2, wo, wpg, wple, gple)


def _prep_w_in(w):
    wt = w.T.astype(BF16)
    points = [sum(SPLIT_SIZES[:i + 1]) for i in range(len(SPLIT_SIZES) - 1)]
    fq, fk, fv, ff, fz, dq, dk, dv, dz, iq, ik, iw = jnp.split(wt, points, axis=0)
    seg = dict(fq=fq, fkv=jnp.concatenate([fk, fv], axis=0), fz=fz, dq=dq, dkv=jnp.concatenate([dk, dv], axis=0),
               dz=dz, iq=iq)
    pad = jnp.zeros((LANES - D_IDX - H_IDX - H_FOX, w.shape[0]), BF16)
    return seg, jnp.concatenate([ik, iw, ff, pad], axis=0)


def _pick_tile(n, pref):
    t = pref
    while n % t:
        t //= 2
    return t


def kernel(x_prompt, x_sample, cache_fox_k, cache_fox_v, cache_fox_logf, cache_dsa_k, cache_dsa_v, cache_idx_k,
           page_table, p_prompt, p_sample, rel_bias, norm_in, w_in, b_f, q_norm_fox, k_norm_fox, q_norm_dsa,
           k_norm_dsa, w_out, w_ple, ple_norm, w_pg):
    b, t, _ = x_prompt.shape
    s, n_tok, _ = x_sample.shape
    n_pages = page_table.shape[1]
    n_past = n_pages * PAGE_SIZE
    n_pool = cache_fox_k.shape[1]
    assert cache_fox_k.shape[0] == 1 and n_tok * H_FOX == 32

    w_seg, w_small = _prep_w_in(w_in[0])
    nin = norm_in[0].reshape(1, D_MODEL)
    gains = jnp.concatenate([q_norm_fox, k_norm_fox, q_norm_dsa, k_norm_dsa], axis=0)
    bf = b_f[0].reshape(H_FOX, 1)

    def project(x2, calls, tm):
        xn = _rmsnorm_in(x2, nin, _pick_tile(x2.shape[0], 1024))
        res = {}
        for steps in calls:
            res.update(_proj(xn, w_seg, w_small, gains, bf, steps, tm))
        (fq16,), (fk, fk16, fv, fv16), (fg,) = res["fq"], res["fkv"], res["fz"]
        (dq16,), (dk, dk16, dv, dv16), (dg,) = res["dq"], res["dkv"], res["dz"]
        return (fq16, fk, fk16, fv, fv16, fg, dq16, dk, dk16, dv, dv16, dg, res["iq"][0]) + res["small"]
    wo = w_out[0].astype(BF16)
    wpg = w_pg[0].astype(BF16)
    wple = w_ple[0].astype(BF16)
    gple = ple_norm[0].reshape(1, D_MODEL)

    tq_d = _pick_tile(t, 256)
    ptab, stab = _bias_tables(rel_bias, tq_d)

    rows_p = b * t
    xp = x_prompt.reshape(rows_p, D_MODEL)
    (fq16, fk, fk16, fv, fv16, fg, dq16, dk, dk16, dv, dv16, dg, iq16, small, logft) = project(
        xp, PROJ_CALLS, _pick_tile(rows_p, 1024))
    tq_f = _pick_tile(t, 256)
    ct = _cumsum_prompt(logft, t)
    c4 = ct.reshape(KV_FOX, 2, rows_p // tq_f, tq_f)
    gf = _fox_prompt(fq16, fk16, fv16, c4, fg, b, t, tq_f)
    ik16 = small[:, SM_IK:SM_IK + D_IDX].astype(BF16)
    ik2 = jnp.concatenate([ik16, ik16], axis=1)
    topk_p = min(TOPK_MAX, t // 4)
    gd = _dsa_prompt(rel_bias, iq16, ik2, small, dq16, dk16, dv16, dg, ptab, b, t, tq_d, topk_p)
    y_p = _out(xp, gf, gd, p_prompt[0].reshape(rows_p, PLE_DIM), wo, wpg, wple, gple, _pick_tile(rows_p, 256))

    rows_s = s * n_tok
    xs = x_sample.reshape(rows_s, D_MODEL)
    (sfq16, sfk, _, sfv, _, sfg, sdq16, sdk, _, sdv, _, sdg, siq16, ssmall, slogft) = project(
        xs, (tuple(n for steps in PROJ_CALLS for n in steps if n != "small") + ("small",),), _pick_tile(rows_s, 512))
    pps = _pick_tile(n_pages, 32)
    rows_q = n_tok * H_FOX

    def new_page(a):
        a = a.reshape(s, n_tok * KV_FOX, HEAD_DIM)
        return jnp.pad(a, ((0, 0), (0, (PAGE_SIZE - n_tok) * KV_FOX), (0, 0)))

    q_rows = lambda a: a.reshape(s, rows_q, HEAD_DIM)
    pool = lambda c: c[0].reshape(n_pool, PAGE_SIZE * KV_FOX, HEAD_DIM)
    lf_pool = jnp.transpose(cache_fox_logf[0], (0, 2, 1))
    lf_new = jnp.pad(jnp.transpose(slogft.reshape(H_FOX, s, n_tok), (1, 0, 2)), ((0, 0), (0, 0), (0, PAGE_SIZE - n_tok)))
    sgf = _smp_fox(page_table, q_rows(sfq16), q_rows(sfg), pool(cache_fox_k), pool(cache_fox_v), lf_pool,
                   new_page(sfk), new_page(sfv), lf_new, pps)

    iq_s = siq16.reshape(s, n_tok * H_IDX, D_IDX)
    w_s = (ssmall[:, SM_IW:SM_IW + H_IDX] * (H_IDX ** -0.5)).reshape(s, n_tok * H_IDX, 1)
    ik_new = jnp.pad(jnp.transpose(ssmall[:, SM_IK:SM_IK + D_IDX].reshape(s, n_tok, D_IDX), (0, 2, 1)),
                     ((0, 0), (0, 0), (0, PAGE_SIZE - n_tok)))
    keys, keys_new = _smp_scores(page_table, iq_s, w_s, jnp.transpose(cache_idx_k[0], (0, 2, 1)), ik_new,
                                 _pick_tile(n_pages, 64))
    topk_s = min(TOPK_MAX, (n_past + n_tok) // 4)
    thr = _smp_thresh(keys.reshape(rows_s, n_past), keys_new.reshape(rows_s, LANES), topk_s).reshape(s, n_tok, LANES)
    assert n_pool <= 256 * PT_SPLIT and topk_s % LANES == 0
    pt_col = lambda a: jnp.broadcast_to(a.astype(BF16)[:, :, None], (s, n_pages, LANES))
    phys, posv = _smp_compact(keys.reshape(s, n_tok, n_pages, PAGE_SIZE), thr,
                              pt_col(page_table // PT_SPLIT), pt_col(page_table % PT_SPLIT), topk_s)
    idx = (phys.reshape(s, n_tok, 1, topk_s) * KV_DSA + jnp.arange(KV_DSA, dtype=I32).reshape(1, 1, KV_DSA, 1)).reshape(-1)
    kv_rows = lambda c: c[0].reshape(n_pool * PAGE_SIZE * KV_DSA, HEAD_DIM)
    kg = _sc_gather_rows(kv_rows(cache_dsa_k), idx)
    vg = _sc_gather_rows(kv_rows(cache_dsa_v), idx)
    sgd = _smp_attn_g(rel_bias, q_rows(sdq16), q_rows(sdg), kg, vg, posv.reshape(s, n_tok, topk_s),
                      new_page(sdk), new_page(sdv), keys_new, thr, stab, n_past)
    y_s = _out(xs, sgf.reshape(rows_s, W_HALF), sgd.reshape(rows_s, W_HALF), p_sample[0].reshape(rows_s, PLE_DIM),
               wo, wpg, wple, gple, _pick_tile(rows_s, 256))

    def kv5(a, bb, tt):
        return a.reshape(1, bb, tt, KV_FOX, HEAD_DIM)

    def outs(bb, tt, fk_, fv_, logft_, dk_, dv_, small_):
        return (kv5(fk_, bb, tt), kv5(fv_, bb, tt), logft_.T.reshape(1, bb, tt, H_FOX),
                kv5(dk_, bb, tt), kv5(dv_, bb, tt), small_[:, SM_IK:SM_IK + D_IDX].reshape(1, bb, tt, D_IDX))

    return ((y_p.reshape(b, t, D_MODEL), y_s.reshape(s, n_tok, D_MODEL))
            + outs(b, t, fk, fv, logft, dk, dv, small)
            + outs(s, n_tok, sfk, sfv, slogft, sdk, sdv, ssmall))
```

```python
import functools
import math

import jax
import jax.numpy as jnp
from jax import lax
from jax.experimental import pallas as pl
from jax.experimental.pallas import tpu as pltpu
from jax.experimental.pallas import tpu_sc as plsc

F32 = jnp.float32
BF16 = jnp.bfloat16
I32 = jnp.int32

D_MODEL = 2048
HEAD_DIM = 128
H_FOX = 8
KV_FOX = 4
H_DSA = 8
KV_DSA = 4
H_IDX = 16
D_IDX = 64
TOPK_MAX = 256
N_BUCKETS = 32
MAX_DISTANCE = 128
PLE_DIM = 256
PAGE_SIZE = 128
EPS = 1e-6
W_HALF = H_FOX * HEAD_DIM
W_KV = KV_FOX * HEAD_DIM
SPLIT_SIZES = (W_HALF, W_KV, W_KV, H_FOX, W_HALF, W_HALF, W_KV, W_KV, W_HALF, H_IDX * D_IDX, D_IDX, H_IDX)

LANES = 128
INT_MIN = -(2 ** 31)
NEG_INF = float("-inf")
VMEM_LIMIT = 56 * 1024 * 1024
SC_GATHER_WINDOW = 128
PT_SPLIT = 64

PROJ_TN = W_HALF
PROJ_OUTS = dict(fq=1, fkv=4, fz=1, dq=1, dkv=4, dz=1, iq=1, small=2)
PROJ_GAIN_ROW = dict(fq=0, fkv=1, dq=2, dkv=3)
PROJ_CALLS = (("fq", "fkv", "small"), ("dq", "dkv"), ("fz", "dz", "iq"))
PROJ_ORDER = tuple(n for steps in PROJ_CALLS for n in steps if n != "small")
SM_IK, SM_IW, SM_FF = 0, D_IDX, D_IDX + H_IDX


def _cparams(sem):
    return pltpu.CompilerParams(dimension_semantics=sem, vmem_limit_bytes=VMEM_LIMIT)


def _dot_nt(a, b):
    return lax.dot_general(a, b, (((1,), (1,)), ((), ())), preferred_element_type=F32)


def _dot(a, b):
    return jnp.dot(a, b, preferred_element_type=F32)


def _log_sigmoid(x):
    return -(jnp.maximum(-x, 0.0) + jnp.log(1.0 + jnp.exp(-jnp.abs(x))))


def _sigmoid(x):
    return 1.0 / (1.0 + jnp.exp(-x))


def _sort_key(x):
    b = pltpu.bitcast(x, I32)
    return b ^ ((b >> 31) & jnp.int32(0x7FFFFFFF))


def _tile_lanes(x, width):
    return x if width == LANES else jnp.concatenate([x] * (width // LANES), axis=1)


def _with_ones(v):
    return jnp.concatenate([v, jnp.ones_like(v)], axis=1)


def _cumsum_lanes(x):
    n = x.shape[-1]
    lane = lax.broadcasted_iota(I32, x.shape, x.ndim - 1)
    k = 1
    while k < n:
        x = x + jnp.where(lane >= k, pltpu.roll(x, k, axis=x.ndim - 1), 0.0)
        k *= 2
    return x


def _proj_kernel(x_ref, nin_ref, w_ref, ws_ref, gain_ref, bf_ref, *outs, steps, norm):
    j = pl.program_id(1)
    if norm:
        xn_ref, outs = outs[0], outs[1:]

        @pl.when(j == 0)
        def _():
            x = x_ref[...]
            ms = jnp.mean(x * x, axis=-1, keepdims=True)
            xn_ref[...] = (x * lax.rsqrt(ms + EPS) * nin_ref[...]).astype(BF16)
    else:
        xn_ref = x_ref

    def main():
        return _dot_nt(xn_ref[...], w_ref[...])

    def head_norm(y, name, scale):
        g = gain_ref[PROJ_GAIN_ROW[name]:PROJ_GAIN_ROW[name] + 1, :]
        cols = []
        for c in range(y.shape[1] // HEAD_DIM):
            yh = y[:, c * HEAD_DIM:(c + 1) * HEAD_DIM]
            n = yh * lax.rsqrt(jnp.mean(yh * yh, axis=-1, keepdims=True) + EPS) * g
            cols.append(n * scale if scale != 1.0 else n)
        return jnp.concatenate(cols, axis=1)

    def store_kv(o_ref, y):
        for kv in range(KV_FOX):
            o_ref[pl.ds(kv, y.shape[0], stride=KV_FOX), :] = y[:, kv * HEAD_DIM:(kv + 1) * HEAD_DIM]

    def body(name, o):
        if name in ("fq", "dq"):
            o[0][...] = head_norm(main(), name, HEAD_DIM ** -0.5).astype(BF16)
        elif name in ("fkv", "dkv"):
            y = main()
            n = head_norm(y[:, :W_KV], name, 1.0)
            store_kv(o[0], n)
            o[1][...] = n.astype(BF16)
            store_kv(o[2], y[:, W_KV:])
            o[3][...] = y[:, W_KV:].astype(BF16)
        elif name in ("fz", "dz"):
            y = main()
            o[0][...] = (y * _sigmoid(y)).astype(BF16)
        elif name == "iq":
            o[0][...] = (main() * (D_IDX ** -0.5)).astype(BF16)
        else:
            ys = _dot_nt(xn_ref[...], ws_ref[...])
            o[0][...] = ys
            o[1][...] = _log_sigmoid(ys.T[SM_FF:SM_FF + H_FOX, :] + bf_ref[...])

    pos = 0
    for k, name in enumerate(steps):
        pl.when(j == k)(functools.partial(body, name, outs[pos:pos + PROJ_OUTS[name]]))
        pos += PROJ_OUTS[name]


def _proj(x, nin, w, w_small, gains, bf, steps, tm, norm):
    rows = x.shape[0]
    main_steps = [n for n in steps if n != "small"]
    assert "small" not in steps[:-1]
    base = PROJ_ORDER.index(main_steps[0])
    assert tuple(main_steps) == PROJ_ORDER[base:base + len(main_steps)]
    sds = jax.ShapeDtypeStruct
    kinds = {
        "wide": (sds((rows, W_HALF), BF16), pl.BlockSpec((tm, W_HALF), lambda i, j: (i, 0))),
        "kv32": (sds((rows * KV_FOX, HEAD_DIM), F32), pl.BlockSpec((tm * KV_FOX, HEAD_DIM), lambda i, j: (i, 0))),
        "kv16": (sds((rows, W_KV), BF16), pl.BlockSpec((tm, W_KV), lambda i, j: (i, 0))),
        "small": (sds((rows, LANES), F32), pl.BlockSpec((tm, LANES), lambda i, j: (i, 0))),
        "logft": (sds((H_FOX, rows), F32), pl.BlockSpec((H_FOX, tm), lambda i, j: (0, i))),
    }
    layout = dict(fq=["wide"], dq=["wide"], fz=["wide"], dz=["wide"], iq=["wide"],
                  fkv=["kv32", "kv16", "kv32", "kv16"], dkv=["kv32", "kv16", "kv32", "kv16"],
                  small=["small", "logft"])
    flat = [kinds[kind] for n in steps for kind in layout[n]]
    if norm:
        flat = [(sds((rows, D_MODEL), BF16), pl.BlockSpec((tm, D_MODEL), lambda i, j: (i, 0)))] + flat
    outs = pl.pallas_call(
        functools.partial(_proj_kernel, steps=tuple(steps), norm=norm),
        out_shape=[shape for shape, _ in flat], grid=(rows // tm, len(steps)),
        in_specs=[
            pl.BlockSpec((tm, D_MODEL), lambda i, j: (i, 0)),
            pl.BlockSpec((1, D_MODEL), lambda i, j: (0, 0)),
            pl.BlockSpec((PROJ_TN, D_MODEL), lambda i, j: (base + jnp.minimum(j, len(main_steps) - 1), 0)),
            pl.BlockSpec((LANES, D_MODEL), lambda i, j: (0, 0)),
            pl.BlockSpec((len(PROJ_GAIN_ROW), HEAD_DIM), lambda i, j: (0, 0)),
            pl.BlockSpec((H_FOX, 1), lambda i, j: (0, 0)),
        ],
        out_specs=[spec for _, spec in flat],
        compiler_params=_cparams(("arbitrary", "arbitrary")), name="proj_" + steps[0],
    )(x, nin, w, w_small, gains, bf)
    result, pos = {}, 0
    if norm:
        result["xn"], outs = outs[0], outs[1:]
    for n in steps:
        result[n] = tuple(outs[pos:pos + PROJ_OUTS[n]])
        pos += PROJ_OUTS[n]
    return result


def _cumsum_kernel(x_ref, o_ref):
    o_ref[...] = _cumsum_lanes(x_ref[...])


def _cumsum_prompt(logft, t):
    rows = logft.shape[1]
    return pl.pallas_call(
        _cumsum_kernel, out_shape=jax.ShapeDtypeStruct(logft.shape, F32), grid=(rows // t,),
        in_specs=[pl.BlockSpec((H_FOX, t), lambda b: (0, b))],
        out_specs=pl.BlockSpec((H_FOX, t), lambda b: (0, b)),
        compiler_params=_cparams(("arbitrary",)), name="cumsum",
    )(logft)


def _fox_kernel(q_ref, k_ref, v_ref, c_ref, g_ref, o_ref, m_s, l_s, acc_s, *, tq):
    qi = pl.program_id(1)
    row = lax.broadcasted_iota(I32, (tq, tq), 0)
    col = lax.broadcasted_iota(I32, (tq, tq), 1)
    m_s[...] = jnp.full(m_s.shape, NEG_INF, F32)
    l_s[...] = jnp.zeros(l_s.shape, F32)
    acc_s[...] = jnp.zeros(acc_s.shape, F32)

    def chunk(j, diag):
        off = pl.multiple_of(j * tq, tq)
        for kv in range(KV_FOX):
            q2 = jnp.concatenate([q_ref[:, (2 * kv + g) * HEAD_DIM:(2 * kv + g + 1) * HEAD_DIM] for g in range(2)], axis=0)
            s = _dot_nt(q2, k_ref[pl.ds(off, tq), kv * HEAD_DIM:(kv + 1) * HEAD_DIM])
            ps, alphas = [], []
            for g in range(2):
                h = 2 * kv + g
                sg = s[g * tq:(g + 1) * tq] - c_ref[kv, g, pl.ds(j, 1), :]
                if diag:
                    sg = jnp.where(col <= row, sg, NEG_INF)
                m_old = m_s[h]
                m_new = jnp.maximum(m_old, jnp.broadcast_to(jnp.max(sg, axis=-1, keepdims=True), (tq, LANES)))
                alphas.append(jnp.exp(m_old - m_new))
                ps.append(jnp.exp(sg - _tile_lanes(m_new, tq)).astype(BF16))
                m_s[h] = m_new
            v = v_ref[pl.ds(off, tq), kv * HEAD_DIM:(kv + 1) * HEAD_DIM]
            pv = _dot(jnp.concatenate(ps, axis=0), _with_ones(v))
            for g in range(2):
                h = 2 * kv + g
                l_s[h] = alphas[g] * l_s[h] + pv[g * tq:(g + 1) * tq, HEAD_DIM:]
                acc_s[h] = alphas[g] * acc_s[h] + pv[g * tq:(g + 1) * tq, :HEAD_DIM]

    def off_diag(j, _):
        chunk(j, False)
        return 0

    lax.fori_loop(0, qi, off_diag, 0)
    chunk(qi, True)
    for h in range(H_FOX):
        o = acc_s[h] / l_s[h]
        o_ref[:, h * HEAD_DIM:(h + 1) * HEAD_DIM] = (o * g_ref[:, h * HEAD_DIM:(h + 1) * HEAD_DIM]).astype(BF16)


def _fox_prompt(fq16, fk16, fv16, c4, fg, b, t, tq):
    nq = t // tq
    qmap = lambda bi, qi: (bi * nq + qi, 0)
    bmap = lambda bi, qi: (bi, 0)
    return pl.pallas_call(
        functools.partial(_fox_kernel, tq=tq),
        out_shape=jax.ShapeDtypeStruct((b * t, W_HALF), BF16), grid=(b, nq),
        in_specs=[
            pl.BlockSpec((tq, W_HALF), qmap),
            pl.BlockSpec((t, W_KV), bmap),
            pl.BlockSpec((t, W_KV), bmap),
            pl.BlockSpec((KV_FOX, 2, nq, tq), lambda bi, qi: (0, 0, bi, 0)),
            pl.BlockSpec((tq, W_HALF), qmap),
        ],
        out_specs=pl.BlockSpec((tq, W_HALF), qmap),
        scratch_shapes=[pltpu.VMEM((H_FOX, tq, LANES), F32), pltpu.VMEM((H_FOX, tq, LANES), F32),
                        pltpu.VMEM((H_FOX, tq, HEAD_DIM), F32)],
        compiler_params=_cparams(("arbitrary", "arbitrary")), name="fox_prompt",
    )(fq16, fk16, fv16, c4, fg)


def _t5_bucket(d):
    max_exact = N_BUCKETS // 2
    d = jnp.maximum(d, 0)
    lr = jnp.log(jnp.maximum(d, 1).astype(F32) / max_exact) / math.log(MAX_DISTANCE / max_exact)
    large = jnp.minimum(max_exact + (lr * (N_BUCKETS - max_exact)).astype(I32), N_BUCKETS - 1)
    return jnp.where(d < max_exact, d, large)


def _bias_from_dist(dist, rb_ref, h):
    bucket = _t5_bucket(dist)
    out = jnp.zeros(dist.shape, F32)
    for bkt in range(N_BUCKETS):
        out = jnp.where(bucket == bkt, rb_ref[bkt, h], out)
    return out


def _bias_tab_kernel(rb_ref, ptab_ref, stab_ref, *, tq):
    r = lax.broadcasted_iota(I32, (tq, 2 * tq), 0)
    c = lax.broadcasted_iota(I32, (tq, 2 * tq), 1)
    for h in range(H_DSA):
        ptab_ref[h] = _bias_from_dist(r - c + tq, rb_ref, h)
    rows = 4 * H_DSA
    rr = lax.broadcasted_iota(I32, (rows, LANES), 0)
    pos = lax.broadcasted_iota(I32, (rows, LANES), 1)
    t = rr // H_DSA
    hh = rr % H_DSA
    dists = (jnp.full((rows, LANES), 2 * MAX_DISTANCE, I32), PAGE_SIZE + t - pos, t - pos)
    for k, dist in enumerate(dists):
        acc = jnp.zeros((rows, LANES), F32)
        for h in range(H_DSA):
            acc = jnp.where(hh == h, _bias_from_dist(dist, rb_ref, h), acc)
        stab_ref[k] = acc


def _bias_tables(rel_bias, tq):
    return pl.pallas_call(
        functools.partial(_bias_tab_kernel, tq=tq),
        out_shape=[jax.ShapeDtypeStruct((H_DSA, tq, 2 * tq), F32),
                   jax.ShapeDtypeStruct((3, 4 * H_DSA, LANES), F32)],
        in_specs=[pl.BlockSpec(memory_space=pltpu.SMEM)],
        name="bias_tables",
    )(rel_bias)


def _kth_largest_key(count_ge, shape, k):
    def step(i, cur):
        cand = cur + lax.shift_left(jnp.int32(1), jnp.int32(31) - i)
        return jnp.where(count_ge(cand) >= k, cand, cur)
    return lax.fori_loop(0, 32, step, jnp.full(shape, INT_MIN, I32))


def _dsa_kernel(rb_ref, iq_ref, ik2_ref, sm_ref, q_ref, k_ref, v_ref, g_ref, tab_ref, o_ref,
                key_s, keyt_s, wb_s, m_s, l_s, acc_s, *, tq, topk):
    qi = pl.program_id(1)
    n_pairs = H_IDX // 2
    row = lax.broadcasted_iota(I32, (tq, tq), 0)
    col = lax.broadcasted_iota(I32, (tq, tq), 1)
    lane = lax.broadcasted_iota(I32, (tq, LANES), 1)

    w = sm_ref[:, SM_IW:SM_IW + H_IDX] * (H_IDX ** -0.5)
    for h in range(H_IDX):
        wb_s[h] = jnp.broadcast_to(w[:, h:h + 1], (tq, LANES))
    iqs = jnp.concatenate([iq_ref[:, p * LANES:(p + 1) * LANES] for p in range(n_pairs)], axis=0)

    def score_chunk(j, _):
        off = pl.multiple_of(j * tq, tq)
        ik2 = ik2_ref[pl.ds(off, tq), :]
        rhs = jnp.concatenate([jnp.where(lane < D_IDX, ik2, 0), jnp.where(lane >= D_IDX, ik2, 0)], axis=0)
        s2 = _dot_nt(iqs, rhs)
        sc = jnp.zeros((tq, tq), F32)
        for p in range(n_pairs):
            for e in range(2):
                r = jnp.maximum(s2[p * tq:(p + 1) * tq, e * tq:(e + 1) * tq], 0.0)
                sc = sc + jnp.concatenate([wb_s[2 * p + e]] * (tq // LANES), axis=1) * r
        key = _sort_key(jnp.where(off + col <= qi * tq + row, sc, NEG_INF))
        key_s[j] = key
        keyt_s[j] = key.T
        return 0

    lax.fori_loop(0, qi + 1, score_chunk, 0)

    def count_ge(cand):
        def body(j, cnt):
            hit = jnp.where(keyt_s[j] >= cand, 1, 0)
            return cnt + jnp.sum(hit.reshape(tq // 8, 8, tq), axis=0)
        cnt = lax.fori_loop(0, qi + 1, body, jnp.zeros((8, tq), I32))
        return jnp.sum(cnt, axis=0, keepdims=True)

    thr_row = _kth_largest_key(count_ge, (1, tq), topk)
    thr_col = jnp.broadcast_to(thr_row, (LANES, tq)).T
    thr = jnp.concatenate([thr_col] * (tq // LANES), axis=1)

    m_s[...] = jnp.full(m_s.shape, NEG_INF, F32)
    l_s[...] = jnp.zeros(l_s.shape, F32)
    acc_s[...] = jnp.zeros(acc_s.shape, F32)

    def attend(j, mode):
        off = pl.multiple_of(j * tq, tq)
        sel = key_s[j] >= thr
        if mode == 2:
            sel = jnp.logical_and(sel, col <= row)
        for kv in range(KV_DSA):
            q2 = jnp.concatenate([q_ref[:, (2 * kv + g) * HEAD_DIM:(2 * kv + g + 1) * HEAD_DIM] for g in range(2)], axis=0)
            s = _dot_nt(q2, k_ref[pl.ds(off, tq), kv * HEAD_DIM:(kv + 1) * HEAD_DIM])
            ps, alphas = [], []
            for g in range(2):
                h = 2 * kv + g
                sg = s[g * tq:(g + 1) * tq]
                if mode == 0:
                    sg = sg + rb_ref[N_BUCKETS - 1, h]
                elif mode == 1:
                    sg = sg + tab_ref[h, :, :tq]
                else:
                    sg = sg + tab_ref[h, :, tq:]
                sg = jnp.where(sel, sg, NEG_INF)
                m_old = m_s[h]
                m_new = jnp.maximum(m_old, jnp.broadcast_to(jnp.max(sg, axis=-1, keepdims=True), (tq, LANES)))
                m_safe = jnp.where(m_new == NEG_INF, 0.0, m_new)
                alphas.append(jnp.exp(m_old - m_safe))
                ps.append(jnp.exp(sg - _tile_lanes(m_safe, tq)).astype(BF16))
                m_s[h] = m_new
            v = v_ref[pl.ds(off, tq), kv * HEAD_DIM:(kv + 1) * HEAD_DIM]
            pv = _dot(jnp.concatenate(ps, axis=0), _with_ones(v))
            for g in range(2):
                h = 2 * kv + g
                l_s[h] = alphas[g] * l_s[h] + pv[g * tq:(g + 1) * tq, HEAD_DIM:]
                acc_s[h] = alphas[g] * acc_s[h] + pv[g * tq:(g + 1) * tq, :HEAD_DIM]

    def far(j, _):
        attend(j, 0)
        return 0

    lax.fori_loop(0, jnp.maximum(qi - 1, 0), far, 0)

    @pl.when(qi >= 1)
    def _():
        attend(qi - 1, 1)

    attend(qi, 2)

    for h in range(H_DSA):
        o = acc_s[h] / l_s[h]
        o_ref[:, h * HEAD_DIM:(h + 1) * HEAD_DIM] = (o * g_ref[:, h * HEAD_DIM:(h + 1) * HEAD_DIM]).astype(BF16)


def _dsa_prompt(rel_bias, iq16, ik2, small, dq16, dk16, dv16, dg, ptab, b, t, tq, topk):
    nq = t // tq
    rows = b * t
    qmap = lambda bi, qi: (bi * nq + qi, 0)
    bmap = lambda bi, qi: (bi, 0)
    return pl.pallas_call(
        functools.partial(_dsa_kernel, tq=tq, topk=topk),
        out_shape=jax.ShapeDtypeStruct((rows, W_HALF), BF16), grid=(b, nq),
        in_specs=[
            pl.BlockSpec(memory_space=pltpu.SMEM),
            pl.BlockSpec((tq, H_IDX * D_IDX), qmap),
            pl.BlockSpec((t, LANES), bmap),
            pl.BlockSpec((tq, LANES), qmap),
            pl.BlockSpec((tq, W_HALF), qmap),
            pl.BlockSpec((t, W_KV), bmap),
            pl.BlockSpec((t, W_KV), bmap),
            pl.BlockSpec((tq, W_HALF), qmap),
            pl.BlockSpec((H_DSA, tq, 2 * tq), lambda bi, qi: (0, 0, 0), pipeline_mode=pl.Buffered(1)),
        ],
        out_specs=pl.BlockSpec((tq, W_HALF), qmap),
        scratch_shapes=[pltpu.VMEM((nq, tq, tq), I32), pltpu.VMEM((nq, tq, tq), I32),
                        pltpu.VMEM((H_IDX, tq, LANES), F32),
                        pltpu.VMEM((H_DSA, tq, LANES), F32), pltpu.VMEM((H_DSA, tq, LANES), F32),
                        pltpu.VMEM((H_DSA, tq, HEAD_DIM), F32)],
        compiler_params=_cparams(("arbitrary", "arbitrary")), name="dsa_prompt",
    )(rel_bias, iq16, ik2, small, dq16, dk16, dv16, dg, ptab)


def _page_specs(shape_tail, pps, new_step_tail):
    nd = len(shape_tail)
    return [pl.BlockSpec((1,) + shape_tail, functools.partial(
        lambda s, p, pt, i: (pt[s, p * pps + i],) + (0,) * nd, i=i)) for i in range(pps)]


def _smp_score_kernel(pt_ref, iq_ref, w_ref, *refs, pps, n_tok):
    ik_refs, iknew_ref, o_ref, onew_ref = refs[:pps], refs[pps], refs[pps + 1], refs[pps + 2]
    p = pl.program_id(1)
    iq = iq_ref[0]
    wcol = w_ref[0]

    def page_scores(ik_t):
        r = jnp.maximum(_dot(iq, ik_t.astype(BF16)), 0.0) * wcol
        return jnp.sum(r.reshape(n_tok, H_IDX, ik_t.shape[1]), axis=1)

    ik_all = jnp.concatenate([ik_refs[i][0] for i in range(pps)], axis=1)
    o_ref[0] = _sort_key(page_scores(ik_all))

    @pl.when(p == pl.num_programs(1) - 1)
    def _():
        sc = page_scores(iknew_ref[0])
        t = lax.broadcasted_iota(I32, (n_tok, LANES), 0)
        pos = lax.broadcasted_iota(I32, (n_tok, LANES), 1)
        onew_ref[0] = _sort_key(jnp.where(pos <= t, sc, NEG_INF))


def _smp_scores(page_table, iq_s, w_s, ik_pool, ik_new, pps):
    s, n_pages = page_table.shape
    n_tok = iq_s.shape[1] // H_IDX
    grid_spec = pltpu.PrefetchScalarGridSpec(
        num_scalar_prefetch=1, grid=(s, n_pages // pps),
        in_specs=[pl.BlockSpec((1, n_tok * H_IDX, D_IDX), lambda si, p, pt: (si, 0, 0)),
                  pl.BlockSpec((1, n_tok * H_IDX, 1), lambda si, p, pt: (si, 0, 0))]
                 + _page_specs((D_IDX, PAGE_SIZE), pps, None)
                 + [pl.BlockSpec((1, D_IDX, PAGE_SIZE), lambda si, p, pt: (si, 0, 0))],
        out_specs=[pl.BlockSpec((1, n_tok, pps * LANES), lambda si, p, pt: (si, 0, p)),
                   pl.BlockSpec((1, n_tok, LANES), lambda si, p, pt: (si, 0, 0))],
    )
    return pl.pallas_call(
        functools.partial(_smp_score_kernel, pps=pps, n_tok=n_tok),
        out_shape=[jax.ShapeDtypeStruct((s, n_tok, n_pages * LANES), I32),
                   jax.ShapeDtypeStruct((s, n_tok, LANES), I32)], grid_spec=grid_spec,
        compiler_params=_cparams(("arbitrary", "arbitrary")), name="smp_scores",
    )(page_table, iq_s, w_s, *([ik_pool] * pps), ik_new)


def _smp_thresh_kernel(key_ref, knew_ref, o_ref, *, topk):
    def count_ge(cand):
        hit = jnp.sum(jnp.where(key_ref[...] >= cand, 1, 0), axis=-1, keepdims=True)
        return hit + jnp.sum(jnp.where(knew_ref[...] >= cand, 1, 0), axis=-1, keepdims=True)

    thr = _kth_largest_key(count_ge, (key_ref.shape[0], 1), topk)
    o_ref[...] = jnp.broadcast_to(thr, o_ref.shape)


def _smp_thresh(keys, keys_new, topk):
    rows, n = keys.shape
    tr = _pick_tile(rows, 32)
    return pl.pallas_call(
        functools.partial(_smp_thresh_kernel, topk=topk),
        out_shape=jax.ShapeDtypeStruct((rows, LANES), I32), grid=(rows // tr,),
        in_specs=[pl.BlockSpec((tr, n), lambda i: (i, 0)), pl.BlockSpec((tr, LANES), lambda i: (i, 0))],
        out_specs=pl.BlockSpec((tr, LANES), lambda i: (i, 0)),
        compiler_params=_cparams(("arbitrary",)), name="smp_thresh",
    )(keys, keys_new)


def _smp_fox_kernel(pt_ref, q_ref, g_ref, *refs, pps, n_tok, n_kv):
    k_refs, v_refs, lf_refs = refs[:pps], refs[pps:2 * pps], refs[2 * pps:3 * pps]
    knew_ref, vnew_ref, lfnew_ref, o_ref, m_s, l_s, acc_s, carry_s = refs[3 * pps:]
    p = pl.program_id(1)
    last = pl.num_programs(1) - 1
    rows = q_ref.shape[1]
    heads = rows // n_tok
    grp = heads // n_kv
    q = q_ref[0]
    rr = lax.broadcasted_iota(I32, (rows, LANES), 0)
    pos = lax.broadcasted_iota(I32, (rows, LANES), 1)
    row_kv = (rr % heads) // grp
    qcat = jnp.concatenate([jnp.where(row_kv == kv, q, jnp.zeros_like(q)) for kv in range(n_kv)], axis=1)

    def heads_of(ref, kv):
        return ref[0, pl.ds(kv, PAGE_SIZE, stride=n_kv), :].astype(BF16)

    @pl.when(p == 0)
    def _():
        m_s[...] = jnp.full(m_s.shape, NEG_INF, F32)
        l_s[...] = jnp.zeros(l_s.shape, F32)
        acc_s[...] = jnp.zeros(acc_s.shape, F32)
        carry_s[...] = jnp.zeros(carry_s.shape, F32)

    def attend(kv_refs, bias, sel):
        kcat = jnp.concatenate([jnp.concatenate([heads_of(k_ref, kv) for kv in range(n_kv)], axis=1)
                                for k_ref, _ in kv_refs], axis=0)
        s = _dot_nt(qcat, kcat) + bias
        if sel is not None:
            s = jnp.where(sel, s, NEG_INF)
        m_old = m_s[...]
        m_new = jnp.maximum(m_old, jnp.max(s, axis=-1, keepdims=True))
        m_safe = jnp.where(m_new == NEG_INF, 0.0, m_new)
        alpha = jnp.exp(m_old - m_safe)
        pr = jnp.exp(s - m_safe)
        l_s[...] = alpha * l_s[...] + jnp.sum(pr, axis=-1, keepdims=True)
        m_s[...] = m_new
        pcat = jnp.concatenate([jnp.where(row_kv == kv, pr[:, i * LANES:(i + 1) * LANES], 0.0).astype(BF16)
                                for i in range(len(kv_refs)) for kv in range(n_kv)], axis=1)
        vcat = jnp.concatenate([heads_of(v_ref, kv) for _, v_ref in kv_refs for kv in range(n_kv)], axis=0)
        acc_s[...] = alpha * acc_s[...] + _dot(pcat, vcat)

    def fox_bias(lf_list):
        c = carry_s[...] + _cumsum_lanes(jnp.concatenate([r[0] for r in lf_list], axis=1))
        carry_s[...] = c[:, c.shape[1] - 1:]
        return -jnp.concatenate([c] * n_tok, axis=0)

    attend(list(zip(k_refs, v_refs)), fox_bias(lf_refs), None)

    @pl.when(p == last)
    def _():
        attend([(knew_ref, vnew_ref)], fox_bias([lfnew_ref]), pos <= rr // heads)
        o_ref[0] = ((acc_s[...] / l_s[...]) * g_ref[0]).astype(BF16)


def _smp_fox(page_table, q_s, g_s, k_pool, v_pool, lf_pool, k_new, v_new, lf_new, pps):
    s, n_pages = page_table.shape
    rows = q_s.shape[1]
    seq3 = lambda a, b: pl.BlockSpec((1, a, b), lambda si, p, pt: (si, 0, 0))
    in_specs = ([seq3(rows, HEAD_DIM), seq3(rows, HEAD_DIM)]
                + _page_specs((PAGE_SIZE * KV_FOX, HEAD_DIM), pps, None) * 2
                + _page_specs((H_FOX, PAGE_SIZE), pps, None)
                + [seq3(PAGE_SIZE * KV_FOX, HEAD_DIM)] * 2 + [seq3(H_FOX, PAGE_SIZE)])
    grid_spec = pltpu.PrefetchScalarGridSpec(
        num_scalar_prefetch=1, grid=(s, n_pages // pps), in_specs=in_specs, out_specs=seq3(rows, HEAD_DIM),
        scratch_shapes=[pltpu.VMEM((rows, 1), F32), pltpu.VMEM((rows, 1), F32), pltpu.VMEM((rows, HEAD_DIM), F32),
                        pltpu.VMEM((H_FOX, 1), F32)])
    return pl.pallas_call(
        functools.partial(_smp_fox_kernel, pps=pps, n_tok=rows // H_FOX, n_kv=KV_FOX),
        out_shape=jax.ShapeDtypeStruct((s, rows, HEAD_DIM), BF16), grid_spec=grid_spec,
        compiler_params=_cparams(("arbitrary", "arbitrary")), name="smp_attn_fox",
    )(page_table, q_s, g_s, *([k_pool] * pps), *([v_pool] * pps), *([lf_pool] * pps), k_new, v_new, lf_new)


def _smp_compact_kernel(key_ref, thr_ref, pthi_ref, ptlo_ref, phys_o, pos_o, *, n_tok, topk):
    n_pg = key_ref.shape[2]
    reps = topk // LANES
    pg = lax.broadcasted_iota(I32, (n_pg, LANES), 0).astype(BF16)
    strict_lower = (lax.broadcasted_iota(I32, (n_pg, n_pg), 1) < lax.broadcasted_iota(I32, (n_pg, n_pg), 0))
    ltri = jnp.where(strict_lower, 1.0, 0.0).astype(BF16)
    in_page = lax.broadcasted_iota(I32, (LANES, LANES), 0).astype(BF16)
    slot_p = lax.broadcasted_iota(I32, (topk, n_pg), 0).astype(F32)
    slot_l = lax.broadcasted_iota(I32, (topk, LANES), 0).astype(F32)
    rows_t = lambda x: jnp.concatenate([x.T] * reps, axis=0)

    for b, t in [(b, t) for b in range(key_ref.shape[0]) for t in range(n_tok)]:
        hit = jnp.where(key_ref[b, t] >= thr_ref[b, t:t + 1, :], 1.0, 0.0)
        lr = _cumsum_lanes(hit)
        cnt = jnp.broadcast_to(lr[:, LANES - 1:], (n_pg, LANES))
        off = _dot(ltri, cnt.astype(BF16))
        total = off[n_pg - 1:, :] + cnt[n_pg - 1:, :]
        off_t, cnt_t = rows_t(off), rows_t(cnt)
        owner = jnp.where(jnp.logical_and(off_t <= slot_p, slot_p < off_t + cnt_t), 1.0, 0.0).astype(BF16)
        to_slot = lambda x: _dot(owner, x.astype(BF16))
        rank_in_page = slot_l - to_slot(off) + 1.0
        pick = jnp.logical_and(to_slot(hit) > 0.5, to_slot(lr) == rank_in_page)
        pos_in_page = _dot(jnp.where(pick, 1.0, 0.0).astype(BF16), in_page)
        valid = slot_l < total
        pos = to_slot(pg) * PAGE_SIZE + pos_in_page
        phys = (_dot(owner, pthi_ref[b]) * PT_SPLIT + _dot(owner, ptlo_ref[b])) * PAGE_SIZE + pos_in_page
        phys_o[b, t] = jnp.where(valid, phys, 0.0).astype(I32)[:, :1]
        pos_o[b, t] = jnp.where(valid, pos, -1.0).astype(I32)[:, :1]


def _smp_compact(keys4, thr, pt_hi, pt_lo, topk):
    s, n_tok, n_pg, _ = keys4.shape
    bs = _pick_tile(s, 4)
    seq = lambda *tail: pl.BlockSpec((bs,) + tail, lambda si: (si,) + (0,) * len(tail))
    out = jax.ShapeDtypeStruct((s, n_tok, topk, 1), I32)
    return pl.pallas_call(
        functools.partial(_smp_compact_kernel, n_tok=n_tok, topk=topk), out_shape=[out, out], grid=(s // bs,),
        in_specs=[seq(n_tok, n_pg, LANES), seq(n_tok, LANES), seq(n_pg, LANES), seq(n_pg, LANES)],
        out_specs=[seq(n_tok, topk, 1), seq(n_tok, topk, 1)],
        compiler_params=_cparams(("arbitrary",)), name="smp_compact",
    )(keys4, thr, pt_hi, pt_lo)


def _sc_gather_rows(table, idx):
    n, d = idx.shape[0], table.shape[1]
    half = n // SC_GATHER_WINDOW // 2
    mesh = plsc.VectorSubcoreMesh(core_axis_name="core", subcore_axis_name="subcore")

    @functools.partial(pl.kernel, out_type=jax.ShapeDtypeStruct((n, d), table.dtype), mesh=mesh)
    def gather(x_hbm, i_hbm, o_hbm):
        def body(i_vmem, o_vmem):
            pltpu.sync_copy(x_hbm.at[i_vmem.at[0]], o_vmem)

        pltpu.emit_pipeline(
            body, grid=(2, half),
            in_specs=[pl.BlockSpec((1, SC_GATHER_WINDOW), index_map=lambda c, i: (0, c * half + i))],
            out_specs=[pl.BlockSpec((SC_GATHER_WINDOW, d), index_map=lambda c, i: (c * half + i, 0))],
            core_axis_name=("core", "subcore"), dimension_semantics=(pltpu.PARALLEL, pltpu.PARALLEL),
        )(i_hbm, o_hbm)

    return gather(table, idx.reshape(1, n))


def _smp_attn_g_kernel(rb_ref, q_ref, g_ref, kg_ref, vg_ref, pos_ref, knew_ref, vnew_ref, keynew_ref, thr_ref,
                       tab_ref, o_ref, *, n_tok, n_kv, n_past):
    heads = q_ref.shape[1] // n_tok
    grp = heads // n_kv
    topk = pos_ref.shape[2]
    row_kv = lax.broadcasted_iota(I32, (heads, LANES), 0) // grp
    lane = lax.broadcasted_iota(I32, (1, LANES), 1)

    def by_kv(x):
        return jnp.concatenate([jnp.where(row_kv[:, :1] == kv, x, jnp.zeros_like(x)) for kv in range(n_kv)], axis=1)

    def new_rows(ref, kv):
        return ref[0, pl.ds(kv, PAGE_SIZE, stride=n_kv), :].astype(BF16)

    knew = jnp.concatenate([new_rows(knew_ref, kv) for kv in range(n_kv)], axis=1)
    vnew = jnp.concatenate([new_rows(vnew_ref, kv) for kv in range(n_kv)], axis=0)

    for t in range(n_tok):
        rows = slice(t * heads, (t + 1) * heads)
        blk = lambda ref, kv: ref[(t * n_kv + kv) * topk:(t * n_kv + kv + 1) * topk, :].astype(BF16)
        qcat = by_kv(q_ref[0, rows, :])
        pos = pos_ref[0, t:t + 1, :]
        dist = n_past + t - pos
        bias = jnp.concatenate([_bias_from_dist(dist, rb_ref, h) for h in range(heads)], axis=0)
        s_g = _dot_nt(qcat, jnp.concatenate([blk(kg_ref, kv) for kv in range(n_kv)], axis=1)) + bias
        s_g = jnp.where(pos >= 0, s_g, NEG_INF)
        sel_n = jnp.logical_and(keynew_ref[0, t:t + 1, :] >= thr_ref[0, t:t + 1, :], lane <= t)
        s_n = jnp.where(sel_n, _dot_nt(qcat, knew) + tab_ref[2, rows, :], NEG_INF)
        m = jnp.maximum(jnp.max(s_g, axis=-1, keepdims=True), jnp.max(s_n, axis=-1, keepdims=True))
        p_g, p_n = jnp.exp(s_g - m), jnp.exp(s_n - m)
        denom = jnp.sum(p_g, axis=-1, keepdims=True) + jnp.sum(p_n, axis=-1, keepdims=True)
        o = _dot(by_kv(p_g).astype(BF16), jnp.concatenate([blk(vg_ref, kv) for kv in range(n_kv)], axis=0))
        o = o + _dot(by_kv(p_n).astype(BF16), vnew)
        o_ref[0, rows, :] = ((o / denom) * g_ref[0, rows, :]).astype(BF16)


def _smp_attn_g(rel_bias, q_s, g_s, kg, vg, posv, k_new, v_new, keys_new, thr, stab, n_past):
    s, rows, _ = q_s.shape
    n_tok, topk = posv.shape[1], posv.shape[2]
    seq = lambda *tail: pl.BlockSpec((1,) + tail, lambda si: (si,) + (0,) * len(tail))
    gathered = pl.BlockSpec((n_tok * KV_DSA * topk, HEAD_DIM), lambda si: (si, 0))
    return pl.pallas_call(
        functools.partial(_smp_attn_g_kernel, n_tok=n_tok, n_kv=KV_DSA, n_past=n_past),
        out_shape=jax.ShapeDtypeStruct((s, rows, HEAD_DIM), BF16), grid=(s,),
        in_specs=[pl.BlockSpec(memory_space=pltpu.SMEM), seq(rows, HEAD_DIM), seq(rows, HEAD_DIM), gathered, gathered,
                  seq(n_tok, topk), seq(PAGE_SIZE * KV_DSA, HEAD_DIM), seq(PAGE_SIZE * KV_DSA, HEAD_DIM),
                  seq(n_tok, LANES), seq(n_tok, LANES), pl.BlockSpec((3, rows, LANES), lambda si: (0, 0, 0))],
        out_specs=seq(rows, HEAD_DIM),
        compiler_params=_cparams(("arbitrary",)), name="smp_attn_gathered",
    )(rel_bias, q_s, g_s, kg, vg, posv, k_new, v_new, keys_new, thr, stab)


def _out_kernel(x_ref, gf_ref, gd_ref, p_ref, wo_ref, wpg_ref, wple_ref, gple_ref, o_ref):
    h = x_ref[...] + _dot(gf_ref[...], wo_ref[:W_HALF, :]) + _dot(gd_ref[...], wo_ref[W_HALF:, :])
    gate = _sigmoid(_dot(h.astype(BF16), wpg_ref[...]))
    e = _dot(p_ref[...].astype(BF16), wple_ref[...])
    e = e * lax.rsqrt(jnp.mean(e * e, axis=-1, keepdims=True) + EPS) * gple_ref[...]
    o_ref[...] = h + gate * e


def _out(x2, gf, gd, p2, wo, wpg, wple, gple, tm):
    rows = x2.shape[0]
    const = lambda shape: pl.BlockSpec(shape, lambda i: (0, 0), pipeline_mode=pl.Buffered(1))
    rmap = lambda i: (i, 0)
    return pl.pallas_call(
        _out_kernel, out_shape=jax.ShapeDtypeStruct((rows, D_MODEL), F32), grid=(rows // tm,),
        in_specs=[pl.BlockSpec((tm, D_MODEL), rmap), pl.BlockSpec((tm, W_HALF), rmap), pl.BlockSpec((tm, W_HALF), rmap),
                  pl.BlockSpec((tm, PLE_DIM), rmap),
                  const((D_MODEL, D_MODEL)), const((D_MODEL, D_MODEL)), const((PLE_DIM, D_MODEL)), const((1, D_MODEL))],
        out_specs=pl.BlockSpec((tm, D_MODEL), rmap),
        compiler_params=_cparams(("arbitrary",)), name="out",
    )(x2, gf, gd, p2, wo, wpg, wple, gple)


def _prep_w_in(w):
    wt = w.T.astype(BF16)
    points = [sum(SPLIT_SIZES[:i + 1]) for i in range(len(SPLIT_SIZES) - 1)]
    fq, fk, fv, ff, fz, dq, dk, dv, dz, iq, ik, iw = jnp.split(wt, points, axis=0)
    seg = dict(fq=[fq], fkv=[fk, fv], fz=[fz], dq=[dq], dkv=[dk, dv], dz=[dz], iq=[iq])
    main = jnp.concatenate([part for n in PROJ_ORDER for part in seg[n]], axis=0)
    pad = jnp.zeros((LANES - D_IDX - H_IDX - H_FOX, w.shape[0]), BF16)
    return main, jnp.concatenate([ik, iw, ff, pad], axis=0)


def _pick_tile(n, pref):
    t = pref
    while n % t:
        t //= 2
    return t


def kernel(x_prompt, x_sample, cache_fox_k, cache_fox_v, cache_fox_logf, cache_dsa_k, cache_dsa_v, cache_idx_k,
           page_table, p_prompt, p_sample, rel_bias, norm_in, w_in, b_f, q_norm_fox, k_norm_fox, q_norm_dsa,
           k_norm_dsa, w_out, w_ple, ple_norm, w_pg):
    b, t, _ = x_prompt.shape
    s, n_tok, _ = x_sample.shape
    n_pages = page_table.shape[1]
    n_past = n_pages * PAGE_SIZE
    n_pool = cache_fox_k.shape[1]
    assert cache_fox_k.shape[0] == 1 and n_tok * H_FOX == 32

    w_main, w_small = _prep_w_in(w_in[0])
    nin = norm_in[0].reshape(1, D_MODEL)
    gains = jnp.concatenate([q_norm_fox, k_norm_fox, q_norm_dsa, k_norm_dsa], axis=0)
    bf = b_f[0].reshape(H_FOX, 1)

    def project(x2, calls, tm):
        res = _proj(x2, nin, w_main, w_small, gains, bf, calls[0], tm, True)
        for steps in calls[1:]:
            res.update(_proj(res["xn"], nin, w_main, w_small, gains, bf, steps, tm, False))
        (fq16,), (fk, fk16, fv, fv16), (fg,) = res["fq"], res["fkv"], res["fz"]
        (dq16,), (dk, dk16, dv, dv16), (dg,) = res["dq"], res["dkv"], res["dz"]
        return (fq16, fk, fk16, fv, fv16, fg, dq16, dk, dk16, dv, dv16, dg, res["iq"][0]) + res["small"]
    wo = w_out[0].astype(BF16)
    wpg = w_pg[0].astype(BF16)
    wple = w_ple[0].astype(BF16)
    gple = ple_norm[0].reshape(1, D_MODEL)

    tq_d = _pick_tile(t, 256)
    ptab, stab = _bias_tables(rel_bias, tq_d)

    rows_p = b * t
    xp = x_prompt.reshape(rows_p, D_MODEL)
    (fq16, fk, fk16, fv, fv16, fg, dq16, dk, dk16, dv, dv16, dg, iq16, small, logft) = project(
        xp, PROJ_CALLS, _pick_tile(rows_p, 1024))
    tq_f = _pick_tile(t, 256)
    ct = _cumsum_prompt(logft, t)
    c4 = ct.reshape(KV_FOX, 2, rows_p // tq_f, tq_f)
    gf = _fox_prompt(fq16, fk16, fv16, c4, fg, b, t, tq_f)
    ik16 = small[:, SM_IK:SM_IK + D_IDX].astype(BF16)
    ik2 = jnp.concatenate([ik16, ik16], axis=1)
    topk_p = min(TOPK_MAX, t // 4)
    gd = _dsa_prompt(rel_bias, iq16, ik2, small, dq16, dk16, dv16, dg, ptab, b, t, tq_d, topk_p)
    y_p = _out(xp, gf, gd, p_prompt[0].reshape(rows_p, PLE_DIM), wo, wpg, wple, gple, _pick_tile(rows_p, 256))

    rows_s = s * n_tok
    xs = x_sample.reshape(rows_s, D_MODEL)
    (sfq16, sfk, _, sfv, _, sfg, sdq16, sdk, _, sdv, _, sdg, siq16, ssmall, slogft) = project(
        xs, (tuple(n for steps in PROJ_CALLS for n in steps if n != "small") + ("small",),), _pick_tile(rows_s, 512))
    pps = _pick_tile(n_pages, 32)
    rows_q = n_tok * H_FOX

    def new_page(a):
        a = a.reshape(s, n_tok * KV_FOX, HEAD_DIM)
        return jnp.pad(a, ((0, 0), (0, (PAGE_SIZE - n_tok) * KV_FOX), (0, 0)))

    q_rows = lambda a: a.reshape(s, rows_q, HEAD_DIM)
    pool = lambda c: c[0].reshape(n_pool, PAGE_SIZE * KV_FOX, HEAD_DIM)
    lf_pool = jnp.transpose(cache_fox_logf[0], (0, 2, 1))
    lf_new = jnp.pad(jnp.transpose(slogft.reshape(H_FOX, s, n_tok), (1, 0, 2)), ((0, 0), (0, 0), (0, PAGE_SIZE - n_tok)))
    sgf = _smp_fox(page_table, q_rows(sfq16), q_rows(sfg), pool(cache_fox_k), pool(cache_fox_v), lf_pool,
                   new_page(sfk), new_page(sfv), lf_new, pps)

    iq_s = siq16.reshape(s, n_tok * H_IDX, D_IDX)
    w_s = (ssmall[:, SM_IW:SM_IW + H_IDX] * (H_IDX ** -0.5)).reshape(s, n_tok * H_IDX, 1)
    ik_new = jnp.pad(jnp.transpose(ssmall[:, SM_IK:SM_IK + D_IDX].reshape(s, n_tok, D_IDX), (0, 2, 1)),
                     ((0, 0), (0, 0), (0, PAGE_SIZE - n_tok)))
    keys, keys_new = _smp_scores(page_table, iq_s, w_s, jnp.transpose(cache_idx_k[0], (0, 2, 1)), ik_new,
                                 _pick_tile(n_pages, 64))
    topk_s = min(TOPK_MAX, (n_past + n_tok) // 4)
    thr = _smp_thresh(keys.reshape(rows_s, n_past), keys_new.reshape(rows_s, LANES), topk_s).reshape(s, n_tok, LANES)
    assert n_pool <= 256 * PT_SPLIT and topk_s % LANES == 0
    pt_col = lambda a: jnp.broadcast_to(a.astype(BF16)[:, :, None], (s, n_pages, LANES))
    phys, posv = _smp_compact(keys.reshape(s, n_tok, n_pages, PAGE_SIZE), thr,
                              pt_col(page_table // PT_SPLIT), pt_col(page_table % PT_SPLIT), topk_s)
    idx = (phys.reshape(s, n_tok, 1, topk_s) * KV_DSA + jnp.arange(KV_DSA, dtype=I32).reshape(1, 1, KV_DSA, 1)).reshape(-1)
    kv_rows = lambda c: c[0].reshape(n_pool * PAGE_SIZE * KV_DSA, HEAD_DIM)
    kg = _sc_gather_rows(kv_rows(cache_dsa_k), idx)
    vg = _sc_gather_rows(kv_rows(cache_dsa_v), idx)
    sgd = _smp_attn_g(rel_bias, q_rows(sdq16), q_rows(sdg), kg, vg, posv.reshape(s, n_tok, topk_s),
                      new_page(sdk), new_page(sdv), keys_new, thr, stab, n_past)
    y_s = _out(xs, sgf.reshape(rows_s, W_HALF), sgd.reshape(rows_s, W_HALF), p_sample[0].reshape(rows_s, PLE_DIM),
               wo, wpg, wple, gple, _pick_tile(rows_s, 256))

    def kv5(a, bb, tt):
        return a.reshape(1, bb, tt, KV_FOX, HEAD_DIM)

    def outs(bb, tt, fk_, fv_, logft_, dk_, dv_, small_):
        return (kv5(fk_, bb, tt), kv5(fv_, bb, tt), logft_.T.reshape(1, bb, tt, H_FOX),
                kv5(dk_, bb, tt), kv5(dv_, bb, tt), small_[:, SM_IK:SM_IK + D_IDX].reshape(1, bb, tt, D_IDX))

    return ((y_p.reshape(b, t, D_MODEL), y_s.reshape(s, n_tok, D_MODEL))
            + outs(b, t, fk, fv, logft, dk, dv, small)
            + outs(s, n_tok, sfk, sfv, slogft, sdk, sdv, ssmall))
```

```python
import functools
import math

import jax
import jax.numpy as jnp
from jax import lax
from jax.experimental import pallas as pl
from jax.experimental.pallas import tpu as pltpu
from jax.experimental.pallas import tpu_sc as plsc

F32 = jnp.float32
BF16 = jnp.bfloat16
I32 = jnp.int32

D_MODEL = 2048
HEAD_DIM = 128
H_FOX = 8
KV_FOX = 4
H_DSA = 8
KV_DSA = 4
H_IDX = 16
D_IDX = 64
TOPK_MAX = 256
N_BUCKETS = 32
MAX_DISTANCE = 128
PLE_DIM = 256
PAGE_SIZE = 128
EPS = 1e-6
W_HALF = H_FOX * HEAD_DIM
W_KV = KV_FOX * HEAD_DIM
SPLIT_SIZES = (W_HALF, W_KV, W_KV, H_FOX, W_HALF, W_HALF, W_KV, W_KV, W_HALF, H_IDX * D_IDX, D_IDX, H_IDX)

LANES = 128
INT_MIN = -(2 ** 31)
NEG_INF = float("-inf")
VMEM_LIMIT = 56 * 1024 * 1024
SC_GATHER_WINDOW = 128
PT_SPLIT = 64
LOG2E = math.log2(math.e)
QK_SCALE = HEAD_DIM ** -0.5 * LOG2E

PROJ_TN = W_HALF
PROJ_OUTS = dict(fq=1, fkv=4, fz=1, dq=1, dkv=4, dz=1, iq=1, small=2)
PROJ_GAIN_ROW = dict(fq=0, fkv=1, dq=2, dkv=3)
PROJ_CALLS = (("fq", "fkv", "small"), ("dq", "dkv"), ("fz", "dz", "iq"))
PROJ_ORDER = tuple(n for steps in PROJ_CALLS for n in steps if n != "small")
SM_IK, SM_IW, SM_FF = 0, D_IDX, D_IDX + H_IDX


def _cparams(sem):
    return pltpu.CompilerParams(dimension_semantics=sem, vmem_limit_bytes=VMEM_LIMIT)


def _dot_nt(a, b):
    return lax.dot_general(a, b, (((1,), (1,)), ((), ())), preferred_element_type=F32)


def _dot(a, b):
    return jnp.dot(a, b, preferred_element_type=F32)


def _log_sigmoid(x):
    return -(jnp.maximum(-x, 0.0) + jnp.log(1.0 + jnp.exp(-jnp.abs(x))))


def _sigmoid(x):
    return 1.0 / (1.0 + jnp.exp(-x))


def _sort_key(x):
    b = pltpu.bitcast(x, I32)
    return b ^ ((b >> 31) & jnp.int32(0x7FFFFFFF))


def _tile_lanes(x, width):
    return x if width == LANES else jnp.concatenate([x] * (width // LANES), axis=1)


def _with_ones(v):
    return jnp.concatenate([v, jnp.ones_like(v)], axis=1)


def _cumsum_lanes(x):
    n = x.shape[-1]
    lane = lax.broadcasted_iota(I32, x.shape, x.ndim - 1)
    k = 1
    while k < n:
        x = x + jnp.where(lane >= k, pltpu.roll(x, k, axis=x.ndim - 1), 0.0)
        k *= 2
    return x


def _proj_kernel(x_ref, nin_ref, w_ref, ws_ref, gain_ref, bf_ref, *outs, steps, norm):
    j = pl.program_id(1)
    if norm:
        xn_ref, outs = outs[0], outs[1:]

        @pl.when(j == 0)
        def _():
            x = x_ref[...]
            ms = jnp.mean(x * x, axis=-1, keepdims=True)
            xn_ref[...] = (x * lax.rsqrt(ms + EPS) * nin_ref[...]).astype(BF16)
    else:
        xn_ref = x_ref

    def main():
        return _dot_nt(xn_ref[...], w_ref[...])

    def head_norm(y, name, scale):
        g = gain_ref[PROJ_GAIN_ROW[name]:PROJ_GAIN_ROW[name] + 1, :]
        cols = []
        for c in range(y.shape[1] // HEAD_DIM):
            yh = y[:, c * HEAD_DIM:(c + 1) * HEAD_DIM]
            n = yh * lax.rsqrt(jnp.mean(yh * yh, axis=-1, keepdims=True) + EPS) * g
            cols.append(n * scale if scale != 1.0 else n)
        return jnp.concatenate(cols, axis=1)

    def store_kv(o_ref, y):
        for kv in range(KV_FOX):
            o_ref[pl.ds(kv, y.shape[0], stride=KV_FOX), :] = y[:, kv * HEAD_DIM:(kv + 1) * HEAD_DIM]

    def body(name, o):
        if name in ("fq", "dq"):
            o[0][...] = head_norm(main(), name, QK_SCALE).astype(BF16)
        elif name in ("fkv", "dkv"):
            y = main()
            n = head_norm(y[:, :W_KV], name, 1.0)
            store_kv(o[0], n)
            o[1][...] = n.astype(BF16)
            store_kv(o[2], y[:, W_KV:])
            o[3][...] = y[:, W_KV:].astype(BF16)
        elif name in ("fz", "dz"):
            y = main()
            o[0][...] = (y * _sigmoid(y)).astype(BF16)
        elif name == "iq":
            o[0][...] = (main() * (D_IDX ** -0.5)).astype(BF16)
        else:
            ys = _dot_nt(xn_ref[...], ws_ref[...])
            o[0][...] = ys
            o[1][...] = _log_sigmoid(ys.T[SM_FF:SM_FF + H_FOX, :] + bf_ref[...])

    pos = 0
    for k, name in enumerate(steps):
        pl.when(j == k)(functools.partial(body, name, outs[pos:pos + PROJ_OUTS[name]]))
        pos += PROJ_OUTS[name]


def _proj(x, nin, w, w_small, gains, bf, steps, tm, norm):
    rows = x.shape[0]
    main_steps = [n for n in steps if n != "small"]
    assert "small" not in steps[:-1]
    base = PROJ_ORDER.index(main_steps[0])
    assert tuple(main_steps) == PROJ_ORDER[base:base + len(main_steps)]
    sds = jax.ShapeDtypeStruct
    kinds = {
        "wide": (sds((rows, W_HALF), BF16), pl.BlockSpec((tm, W_HALF), lambda i, j: (i, 0))),
        "kv32": (sds((rows * KV_FOX, HEAD_DIM), F32), pl.BlockSpec((tm * KV_FOX, HEAD_DIM), lambda i, j: (i, 0))),
        "kv16": (sds((rows, W_KV), BF16), pl.BlockSpec((tm, W_KV), lambda i, j: (i, 0))),
        "small": (sds((rows, LANES), F32), pl.BlockSpec((tm, LANES), lambda i, j: (i, 0))),
        "logft": (sds((H_FOX, rows), F32), pl.BlockSpec((H_FOX, tm), lambda i, j: (0, i))),
    }
    layout = dict(fq=["wide"], dq=["wide"], fz=["wide"], dz=["wide"], iq=["wide"],
                  fkv=["kv32", "kv16", "kv32", "kv16"], dkv=["kv32", "kv16", "kv32", "kv16"],
                  small=["small", "logft"])
    flat = [kinds[kind] for n in steps for kind in layout[n]]
    if norm:
        flat = [(sds((rows, D_MODEL), BF16), pl.BlockSpec((tm, D_MODEL), lambda i, j: (i, 0)))] + flat
    outs = pl.pallas_call(
        functools.partial(_proj_kernel, steps=tuple(steps), norm=norm),
        out_shape=[shape for shape, _ in flat], grid=(rows // tm, len(steps)),
        in_specs=[
            pl.BlockSpec((tm, D_MODEL), lambda i, j: (i, 0)),
            pl.BlockSpec((1, D_MODEL), lambda i, j: (0, 0)),
            pl.BlockSpec((PROJ_TN, D_MODEL), lambda i, j: (base + jnp.minimum(j, len(main_steps) - 1), 0)),
            pl.BlockSpec((LANES, D_MODEL), lambda i, j: (0, 0)),
            pl.BlockSpec((len(PROJ_GAIN_ROW), HEAD_DIM), lambda i, j: (0, 0)),
            pl.BlockSpec((H_FOX, 1), lambda i, j: (0, 0)),
        ],
        out_specs=[spec for _, spec in flat],
        compiler_params=_cparams(("arbitrary", "arbitrary")), name="proj_" + steps[0],
    )(x, nin, w, w_small, gains, bf)
    result, pos = {}, 0
    if norm:
        result["xn"], outs = outs[0], outs[1:]
    for n in steps:
        result[n] = tuple(outs[pos:pos + PROJ_OUTS[n]])
        pos += PROJ_OUTS[n]
    return result


def _cumsum_kernel(x_ref, o_ref):
    o_ref[...] = _cumsum_lanes(x_ref[...]) * LOG2E


def _cumsum_prompt(logft, t):
    rows = logft.shape[1]
    return pl.pallas_call(
        _cumsum_kernel, out_shape=jax.ShapeDtypeStruct(logft.shape, F32), grid=(rows // t,),
        in_specs=[pl.BlockSpec((H_FOX, t), lambda b: (0, b))],
        out_specs=pl.BlockSpec((H_FOX, t), lambda b: (0, b)),
        compiler_params=_cparams(("arbitrary",)), name="cumsum",
    )(logft)


def _fox_kernel(q_ref, k_ref, v_ref, c_ref, g_ref, o_ref, m_s, l_s, acc_s, *, tq):
    qi = pl.program_id(1)
    row = lax.broadcasted_iota(I32, (tq, tq), 0)
    col = lax.broadcasted_iota(I32, (tq, tq), 1)
    m_s[...] = jnp.full(m_s.shape, NEG_INF, F32)
    l_s[...] = jnp.zeros(l_s.shape, F32)
    acc_s[...] = jnp.zeros(acc_s.shape, F32)

    def chunk(j, diag):
        off = pl.multiple_of(j * tq, tq)
        for kv in range(KV_FOX):
            q2 = jnp.concatenate([q_ref[:, (2 * kv + g) * HEAD_DIM:(2 * kv + g + 1) * HEAD_DIM] for g in range(2)], axis=0)
            s = _dot_nt(q2, k_ref[pl.ds(off, tq), kv * HEAD_DIM:(kv + 1) * HEAD_DIM])
            ps, alphas = [], []
            for g in range(2):
                h = 2 * kv + g
                sg = s[g * tq:(g + 1) * tq] - c_ref[kv, g, pl.ds(j, 1), :]
                if diag:
                    sg = jnp.where(col <= row, sg, NEG_INF)
                m_old = m_s[h]
                m_new = jnp.maximum(m_old, jnp.broadcast_to(jnp.max(sg, axis=-1, keepdims=True), (tq, LANES)))
                alphas.append(jnp.exp2(m_old - m_new))
                ps.append(jnp.exp2(sg - _tile_lanes(m_new, tq)).astype(BF16))
                m_s[h] = m_new
            v = v_ref[pl.ds(off, tq), kv * HEAD_DIM:(kv + 1) * HEAD_DIM]
            pv = _dot(jnp.concatenate(ps, axis=0), _with_ones(v))
            for g in range(2):
                h = 2 * kv + g
                l_s[h] = alphas[g] * l_s[h] + pv[g * tq:(g + 1) * tq, HEAD_DIM:]
                acc_s[h] = alphas[g] * acc_s[h] + pv[g * tq:(g + 1) * tq, :HEAD_DIM]

    def off_diag(j, _):
        chunk(j, False)
        return 0

    lax.fori_loop(0, qi, off_diag, 0)
    chunk(qi, True)
    for h in range(H_FOX):
        o = acc_s[h] / l_s[h]
        o_ref[:, h * HEAD_DIM:(h + 1) * HEAD_DIM] = (o * g_ref[:, h * HEAD_DIM:(h + 1) * HEAD_DIM]).astype(BF16)


def _fox_prompt(fq16, fk16, fv16, c4, fg, b, t, tq):
    nq = t // tq
    qmap = lambda bi, qi: (bi * nq + qi, 0)
    bmap = lambda bi, qi: (bi, 0)
    return pl.pallas_call(
        functools.partial(_fox_kernel, tq=tq),
        out_shape=jax.ShapeDtypeStruct((b * t, W_HALF), BF16), grid=(b, nq),
        in_specs=[
            pl.BlockSpec((tq, W_HALF), qmap),
            pl.BlockSpec((t, W_KV), bmap),
            pl.BlockSpec((t, W_KV), bmap),
            pl.BlockSpec((KV_FOX, 2, nq, tq), lambda bi, qi: (0, 0, bi, 0)),
            pl.BlockSpec((tq, W_HALF), qmap),
        ],
        out_specs=pl.BlockSpec((tq, W_HALF), qmap),
        scratch_shapes=[pltpu.VMEM((H_FOX, tq, LANES), F32), pltpu.VMEM((H_FOX, tq, LANES), F32),
                        pltpu.VMEM((H_FOX, tq, HEAD_DIM), F32)],
        compiler_params=_cparams(("arbitrary", "arbitrary")), name="fox_prompt",
    )(fq16, fk16, fv16, c4, fg)


def _t5_bucket(d):
    max_exact = N_BUCKETS // 2
    d = jnp.maximum(d, 0)
    lr = jnp.log(jnp.maximum(d, 1).astype(F32) / max_exact) / math.log(MAX_DISTANCE / max_exact)
    large = jnp.minimum(max_exact + (lr * (N_BUCKETS - max_exact)).astype(I32), N_BUCKETS - 1)
    return jnp.where(d < max_exact, d, large)


def _bias_from_dist(dist, rb_ref, h):
    bucket = _t5_bucket(dist)
    out = jnp.zeros(dist.shape, F32)
    for bkt in range(N_BUCKETS):
        out = jnp.where(bucket == bkt, rb_ref[bkt, h], out)
    return out * LOG2E


def _bias_tab_kernel(rb_ref, ptab_ref, stab_ref, *, tq):
    r = lax.broadcasted_iota(I32, (tq, 2 * tq), 0)
    c = lax.broadcasted_iota(I32, (tq, 2 * tq), 1)
    for h in range(H_DSA):
        ptab_ref[h] = _bias_from_dist(r - c + tq, rb_ref, h) - rb_ref[N_BUCKETS - 1, h] * LOG2E
    rows = 4 * H_DSA
    rr = lax.broadcasted_iota(I32, (rows, LANES), 0)
    pos = lax.broadcasted_iota(I32, (rows, LANES), 1)
    t = rr // H_DSA
    hh = rr % H_DSA
    dists = (jnp.full((rows, LANES), 2 * MAX_DISTANCE, I32), PAGE_SIZE + t - pos, t - pos)
    for k, dist in enumerate(dists):
        acc = jnp.zeros((rows, LANES), F32)
        for h in range(H_DSA):
            acc = jnp.where(hh == h, _bias_from_dist(dist, rb_ref, h), acc)
        stab_ref[k] = acc


def _bias_tables(rel_bias, tq):
    return pl.pallas_call(
        functools.partial(_bias_tab_kernel, tq=tq),
        out_shape=[jax.ShapeDtypeStruct((H_DSA, tq, 2 * tq), F32),
                   jax.ShapeDtypeStruct((3, 4 * H_DSA, LANES), F32)],
        in_specs=[pl.BlockSpec(memory_space=pltpu.SMEM)],
        name="bias_tables",
    )(rel_bias)


def _kth_largest_key(count_ge, shape, k):
    def step(i, cur):
        cand = cur + lax.shift_left(jnp.int32(1), jnp.int32(31) - i)
        return jnp.where(count_ge(cand) >= k, cand, cur)
    return lax.fori_loop(0, 32, step, jnp.full(shape, INT_MIN, I32))


def _dsa_kernel(rb_ref, iq_ref, ik2_ref, sm_ref, q_ref, k_ref, v_ref, g_ref, tab_ref, o_ref,
                key_s, keyt_s, wb_s, m_s, l_s, acc_s, *, tq, topk):
    qi = pl.program_id(1)
    n_pairs = H_IDX // 2
    row = lax.broadcasted_iota(I32, (tq, tq), 0)
    col = lax.broadcasted_iota(I32, (tq, tq), 1)
    lane = lax.broadcasted_iota(I32, (tq, LANES), 1)

    w = sm_ref[:, SM_IW:SM_IW + H_IDX] * (H_IDX ** -0.5)
    for h in range(H_IDX):
        wb_s[h] = jnp.broadcast_to(w[:, h:h + 1], (tq, LANES))
    iqs = jnp.concatenate([iq_ref[:, p * LANES:(p + 1) * LANES] for p in range(n_pairs)], axis=0)

    def score_chunk(j, _):
        off = pl.multiple_of(j * tq, tq)
        ik2 = ik2_ref[pl.ds(off, tq), :]
        rhs = jnp.concatenate([jnp.where(lane < D_IDX, ik2, 0), jnp.where(lane >= D_IDX, ik2, 0)], axis=0)
        s2 = _dot_nt(iqs, rhs)
        sc = jnp.zeros((tq, tq), F32)
        for p in range(n_pairs):
            for e in range(2):
                r = jnp.maximum(s2[p * tq:(p + 1) * tq, e * tq:(e + 1) * tq], 0.0)
                sc = sc + jnp.concatenate([wb_s[2 * p + e]] * (tq // LANES), axis=1) * r
        key = _sort_key(jnp.where(off + col <= qi * tq + row, sc, NEG_INF))
        key_s[j] = key
        keyt_s[j] = key.T
        return 0

    lax.fori_loop(0, qi + 1, score_chunk, 0)

    def count_ge(cand):
        def body(j, cnt):
            hit = jnp.where(keyt_s[j] >= cand, 1, 0)
            return cnt + jnp.sum(hit.reshape(tq // 8, 8, tq), axis=0)
        cnt = lax.fori_loop(0, qi + 1, body, jnp.zeros((8, tq), I32))
        return jnp.sum(cnt, axis=0, keepdims=True)

    thr_row = _kth_largest_key(count_ge, (1, tq), topk)
    thr_col = jnp.broadcast_to(thr_row, (LANES, tq)).T
    thr = jnp.concatenate([thr_col] * (tq // LANES), axis=1)

    m_s[...] = jnp.full(m_s.shape, NEG_INF, F32)
    l_s[...] = jnp.zeros(l_s.shape, F32)
    acc_s[...] = jnp.zeros(acc_s.shape, F32)

    def attend(j, mode):
        off = pl.multiple_of(j * tq, tq)
        sel = key_s[j] >= thr
        if mode == 2:
            sel = jnp.logical_and(sel, col <= row)
        for kv in range(KV_DSA):
            q2 = jnp.concatenate([q_ref[:, (2 * kv + g) * HEAD_DIM:(2 * kv + g + 1) * HEAD_DIM] for g in range(2)], axis=0)
            s = _dot_nt(q2, k_ref[pl.ds(off, tq), kv * HEAD_DIM:(kv + 1) * HEAD_DIM])
            ps, alphas = [], []
            for g in range(2):
                h = 2 * kv + g
                sg = s[g * tq:(g + 1) * tq]
                if mode == 1:
                    sg = sg + tab_ref[h, :, :tq]
                elif mode == 2:
                    sg = sg + tab_ref[h, :, tq:]
                sg = jnp.where(sel, sg, NEG_INF)
                m_old = m_s[h]
                m_new = jnp.maximum(m_old, jnp.broadcast_to(jnp.max(sg, axis=-1, keepdims=True), (tq, LANES)))
                m_safe = jnp.where(m_new == NEG_INF, 0.0, m_new)
                alphas.append(jnp.exp2(m_old - m_safe))
                ps.append(jnp.exp2(sg - _tile_lanes(m_safe, tq)).astype(BF16))
                m_s[h] = m_new
            v = v_ref[pl.ds(off, tq), kv * HEAD_DIM:(kv + 1) * HEAD_DIM]
            pv = _dot(jnp.concatenate(ps, axis=0), _with_ones(v))
            for g in range(2):
                h = 2 * kv + g
                l_s[h] = alphas[g] * l_s[h] + pv[g * tq:(g + 1) * tq, HEAD_DIM:]
                acc_s[h] = alphas[g] * acc_s[h] + pv[g * tq:(g + 1) * tq, :HEAD_DIM]

    def far(j, _):
        attend(j, 0)
        return 0

    lax.fori_loop(0, jnp.maximum(qi - 1, 0), far, 0)

    @pl.when(qi >= 1)
    def _():
        attend(qi - 1, 1)

    attend(qi, 2)

    for h in range(H_DSA):
        o = acc_s[h] / l_s[h]
        o_ref[:, h * HEAD_DIM:(h + 1) * HEAD_DIM] = (o * g_ref[:, h * HEAD_DIM:(h + 1) * HEAD_DIM]).astype(BF16)


def _dsa_prompt(rel_bias, iq16, ik2, small, dq16, dk16, dv16, dg, ptab, b, t, tq, topk):
    nq = t // tq
    rows = b * t
    qmap = lambda bi, qi: (bi * nq + qi, 0)
    bmap = lambda bi, qi: (bi, 0)
    return pl.pallas_call(
        functools.partial(_dsa_kernel, tq=tq, topk=topk),
        out_shape=jax.ShapeDtypeStruct((rows, W_HALF), BF16), grid=(b, nq),
        in_specs=[
            pl.BlockSpec(memory_space=pltpu.SMEM),
            pl.BlockSpec((tq, H_IDX * D_IDX), qmap),
            pl.BlockSpec((t, LANES), bmap),
            pl.BlockSpec((tq, LANES), qmap),
            pl.BlockSpec((tq, W_HALF), qmap),
            pl.BlockSpec((t, W_KV), bmap),
            pl.BlockSpec((t, W_KV), bmap),
            pl.BlockSpec((tq, W_HALF), qmap),
            pl.BlockSpec((H_DSA, tq, 2 * tq), lambda bi, qi: (0, 0, 0), pipeline_mode=pl.Buffered(1)),
        ],
        out_specs=pl.BlockSpec((tq, W_HALF), qmap),
        scratch_shapes=[pltpu.VMEM((nq, tq, tq), I32), pltpu.VMEM((nq, tq, tq), I32),
                        pltpu.VMEM((H_IDX, tq, LANES), F32),
                        pltpu.VMEM((H_DSA, tq, LANES), F32), pltpu.VMEM((H_DSA, tq, LANES), F32),
                        pltpu.VMEM((H_DSA, tq, HEAD_DIM), F32)],
        compiler_params=_cparams(("arbitrary", "arbitrary")), name="dsa_prompt",
    )(rel_bias, iq16, ik2, small, dq16, dk16, dv16, dg, ptab)


def _page_specs(shape_tail, pps, new_step_tail):
    nd = len(shape_tail)
    return [pl.BlockSpec((1,) + shape_tail, functools.partial(
        lambda s, p, pt, i: (pt[s, p * pps + i],) + (0,) * nd, i=i)) for i in range(pps)]


def _smp_score_kernel(pt_ref, iq_ref, w_ref, *refs, pps, n_tok):
    ik_refs, iknew_ref, o_ref, onew_ref = refs[:pps], refs[pps], refs[pps + 1], refs[pps + 2]
    p = pl.program_id(1)
    iq = iq_ref[0]
    wcol = w_ref[0]

    def page_scores(ik_t):
        r = jnp.maximum(_dot(iq, ik_t.astype(BF16)), 0.0) * wcol
        return jnp.sum(r.reshape(n_tok, H_IDX, ik_t.shape[1]), axis=1)

    ik_all = jnp.concatenate([ik_refs[i][0] for i in range(pps)], axis=1)
    o_ref[0] = _sort_key(page_scores(ik_all))

    @pl.when(p == pl.num_programs(1) - 1)
    def _():
        sc = page_scores(iknew_ref[0])
        t = lax.broadcasted_iota(I32, (n_tok, LANES), 0)
        pos = lax.broadcasted_iota(I32, (n_tok, LANES), 1)
        onew_ref[0] = _sort_key(jnp.where(pos <= t, sc, NEG_INF))


def _smp_scores(page_table, iq_s, w_s, ik_pool, ik_new, pps):
    s, n_pages = page_table.shape
    n_tok = iq_s.shape[1] // H_IDX
    grid_spec = pltpu.PrefetchScalarGridSpec(
        num_scalar_prefetch=1, grid=(s, n_pages // pps),
        in_specs=[pl.BlockSpec((1, n_tok * H_IDX, D_IDX), lambda si, p, pt: (si, 0, 0)),
                  pl.BlockSpec((1, n_tok * H_IDX, 1), lambda si, p, pt: (si, 0, 0))]
                 + _page_specs((D_IDX, PAGE_SIZE), pps, None)
                 + [pl.BlockSpec((1, D_IDX, PAGE_SIZE), lambda si, p, pt: (si, 0, 0))],
        out_specs=[pl.BlockSpec((1, n_tok, pps * LANES), lambda si, p, pt: (si, 0, p)),
                   pl.BlockSpec((1, n_tok, LANES), lambda si, p, pt: (si, 0, 0))],
    )
    return pl.pallas_call(
        functools.partial(_smp_score_kernel, pps=pps, n_tok=n_tok),
        out_shape=[jax.ShapeDtypeStruct((s, n_tok, n_pages * LANES), I32),
                   jax.ShapeDtypeStruct((s, n_tok, LANES), I32)], grid_spec=grid_spec,
        compiler_params=_cparams(("arbitrary", "arbitrary")), name="smp_scores",
    )(page_table, iq_s, w_s, *([ik_pool] * pps), ik_new)


def _smp_thresh_kernel(key_ref, knew_ref, o_ref, *, topk):
    def count_ge(cand):
        hit = jnp.sum(jnp.where(key_ref[...] >= cand, 1, 0), axis=-1, keepdims=True)
        return hit + jnp.sum(jnp.where(knew_ref[...] >= cand, 1, 0), axis=-1, keepdims=True)

    thr = _kth_largest_key(count_ge, (key_ref.shape[0], 1), topk)
    o_ref[...] = jnp.broadcast_to(thr, o_ref.shape)


def _smp_thresh(keys, keys_new, topk):
    rows, n = keys.shape
    tr = _pick_tile(rows, 32)
    return pl.pallas_call(
        functools.partial(_smp_thresh_kernel, topk=topk),
        out_shape=jax.ShapeDtypeStruct((rows, LANES), I32), grid=(rows // tr,),
        in_specs=[pl.BlockSpec((tr, n), lambda i: (i, 0)), pl.BlockSpec((tr, LANES), lambda i: (i, 0))],
        out_specs=pl.BlockSpec((tr, LANES), lambda i: (i, 0)),
        compiler_params=_cparams(("arbitrary",)), name="smp_thresh",
    )(keys, keys_new)


def _smp_fox_kernel(pt_ref, q_ref, g_ref, *refs, pps, n_tok, n_kv):
    k_refs, v_refs, lf_refs = refs[:pps], refs[pps:2 * pps], refs[2 * pps:3 * pps]
    knew_ref, vnew_ref, lfnew_ref, o_ref, m_s, l_s, acc_s, carry_s = refs[3 * pps:]
    p = pl.program_id(1)
    last = pl.num_programs(1) - 1
    rows = q_ref.shape[1]
    heads = rows // n_tok
    grp = heads // n_kv
    q = q_ref[0]
    rr = lax.broadcasted_iota(I32, (rows, LANES), 0)
    pos = lax.broadcasted_iota(I32, (rows, LANES), 1)
    row_kv = (rr % heads) // grp
    qcat = jnp.concatenate([jnp.where(row_kv == kv, q, jnp.zeros_like(q)) for kv in range(n_kv)], axis=1)

    def heads_of(ref, kv):
        return ref[0, pl.ds(kv, PAGE_SIZE, stride=n_kv), :].astype(BF16)

    @pl.when(p == 0)
    def _():
        m_s[...] = jnp.full(m_s.shape, NEG_INF, F32)
        l_s[...] = jnp.zeros(l_s.shape, F32)
        acc_s[...] = jnp.zeros(acc_s.shape, F32)
        carry_s[...] = jnp.zeros(carry_s.shape, F32)

    def attend(kv_refs, bias, sel):
        kcat = jnp.concatenate([jnp.concatenate([heads_of(k_ref, kv) for kv in range(n_kv)], axis=1)
                                for k_ref, _ in kv_refs], axis=0)
        s = _dot_nt(qcat, kcat) + bias
        if sel is not None:
            s = jnp.where(sel, s, NEG_INF)
        m_old = m_s[...]
        m_new = jnp.maximum(m_old, jnp.max(s, axis=-1, keepdims=True))
        m_safe = jnp.where(m_new == NEG_INF, 0.0, m_new)
        alpha = jnp.exp2(m_old - m_safe)
        pr = jnp.exp2(s - m_safe)
        l_s[...] = alpha * l_s[...] + jnp.sum(pr, axis=-1, keepdims=True)
        m_s[...] = m_new
        pcat = jnp.concatenate([jnp.where(row_kv == kv, pr[:, i * LANES:(i + 1) * LANES], 0.0).astype(BF16)
                                for i in range(len(kv_refs)) for kv in range(n_kv)], axis=1)
        vcat = jnp.concatenate([heads_of(v_ref, kv) for _, v_ref in kv_refs for kv in range(n_kv)], axis=0)
        acc_s[...] = alpha * acc_s[...] + _dot(pcat, vcat)

    def fox_bias(lf_list):
        c = carry_s[...] + _cumsum_lanes(jnp.concatenate([r[0] for r in lf_list], axis=1))
        carry_s[...] = c[:, c.shape[1] - 1:]
        return jnp.concatenate([c] * n_tok, axis=0) * (-LOG2E)

    attend(list(zip(k_refs, v_refs)), fox_bias(lf_refs), None)

    @pl.when(p == last)
    def _():
        attend([(knew_ref, vnew_ref)], fox_bias([lfnew_ref]), pos <= rr // heads)
        o_ref[0] = ((acc_s[...] / l_s[...]) * g_ref[0]).astype(BF16)


def _smp_fox(page_table, q_s, g_s, k_pool, v_pool, lf_pool, k_new, v_new, lf_new, pps):
    s, n_pages = page_table.shape
    rows = q_s.shape[1]
    seq3 = lambda a, b: pl.BlockSpec((1, a, b), lambda si, p, pt: (si, 0, 0))
    in_specs = ([seq3(rows, HEAD_DIM), seq3(rows, HEAD_DIM)]
                + _page_specs((PAGE_SIZE * KV_FOX, HEAD_DIM), pps, None) * 2
                + _page_specs((H_FOX, PAGE_SIZE), pps, None)
                + [seq3(PAGE_SIZE * KV_FOX, HEAD_DIM)] * 2 + [seq3(H_FOX, PAGE_SIZE)])
    grid_spec = pltpu.PrefetchScalarGridSpec(
        num_scalar_prefetch=1, grid=(s, n_pages // pps), in_specs=in_specs, out_specs=seq3(rows, HEAD_DIM),
        scratch_shapes=[pltpu.VMEM((rows, 1), F32), pltpu.VMEM((rows, 1), F32), pltpu.VMEM((rows, HEAD_DIM), F32),
                        pltpu.VMEM((H_FOX, 1), F32)])
    return pl.pallas_call(
        functools.partial(_smp_fox_kernel, pps=pps, n_tok=rows // H_FOX, n_kv=KV_FOX),
        out_shape=jax.ShapeDtypeStruct((s, rows, HEAD_DIM), BF16), grid_spec=grid_spec,
        compiler_params=_cparams(("arbitrary", "arbitrary")), name="smp_attn_fox",
    )(page_table, q_s, g_s, *([k_pool] * pps), *([v_pool] * pps), *([lf_pool] * pps), k_new, v_new, lf_new)


def _smp_compact_kernel(key_ref, thr_ref, pthi_ref, ptlo_ref, phys_o, pos_o, *, n_tok, topk):
    n_pg = key_ref.shape[2]
    reps = topk // LANES
    pg = lax.broadcasted_iota(I32, (n_pg, LANES), 0).astype(BF16)
    strict_lower = (lax.broadcasted_iota(I32, (n_pg, n_pg), 1) < lax.broadcasted_iota(I32, (n_pg, n_pg), 0))
    ltri = jnp.where(strict_lower, 1.0, 0.0).astype(BF16)
    in_page = lax.broadcasted_iota(I32, (LANES, LANES), 0).astype(BF16)
    slot_p = lax.broadcasted_iota(I32, (topk, n_pg), 0).astype(F32)
    slot_l = lax.broadcasted_iota(I32, (topk, LANES), 0).astype(F32)
    rows_t = lambda x: jnp.concatenate([x.T] * reps, axis=0)

    for b, t in [(b, t) for b in range(key_ref.shape[0]) for t in range(n_tok)]:
        hit = jnp.where(key_ref[b, t] >= thr_ref[b, t:t + 1, :], 1.0, 0.0)
        lr = _cumsum_lanes(hit)
        cnt = jnp.broadcast_to(lr[:, LANES - 1:], (n_pg, LANES))
        off = _dot(ltri, cnt.astype(BF16))
        total = off[n_pg - 1:, :] + cnt[n_pg - 1:, :]
        off_t, cnt_t = rows_t(off), rows_t(cnt)
        owner = jnp.where(jnp.logical_and(off_t <= slot_p, slot_p < off_t + cnt_t), 1.0, 0.0).astype(BF16)
        to_slot = lambda x: _dot(owner, x.astype(BF16))
        rank_in_page = slot_l - to_slot(off) + 1.0
        pick = jnp.logical_and(to_slot(hit) > 0.5, to_slot(lr) == rank_in_page)
        pos_in_page = _dot(jnp.where(pick, 1.0, 0.0).astype(BF16), in_page)
        valid = slot_l < total
        pos = to_slot(pg) * PAGE_SIZE + pos_in_page
        phys = (_dot(owner, pthi_ref[b]) * PT_SPLIT + _dot(owner, ptlo_ref[b])) * PAGE_SIZE + pos_in_page
        phys_o[b, t] = jnp.where(valid, phys, 0.0).astype(I32)[:, :1]
        pos_o[b, t] = jnp.where(valid, pos, -1.0).astype(I32)[:, :1]


def _smp_compact(keys4, thr, pt_hi, pt_lo, topk):
    s, n_tok, n_pg, _ = keys4.shape
    bs = _pick_tile(s, 4)
    seq = lambda *tail: pl.BlockSpec((bs,) + tail, lambda si: (si,) + (0,) * len(tail))
    out = jax.ShapeDtypeStruct((s, n_tok, topk, 1), I32)
    return pl.pallas_call(
        functools.partial(_smp_compact_kernel, n_tok=n_tok, topk=topk), out_shape=[out, out], grid=(s // bs,),
        in_specs=[seq(n_tok, n_pg, LANES), seq(n_tok, LANES), seq(n_pg, LANES), seq(n_pg, LANES)],
        out_specs=[seq(n_tok, topk, 1), seq(n_tok, topk, 1)],
        compiler_params=_cparams(("arbitrary",)), name="smp_compact",
    )(keys4, thr, pt_hi, pt_lo)


def _sc_gather_rows(table, idx):
    n, d = idx.shape[0], table.shape[1]
    half = n // SC_GATHER_WINDOW // 2
    mesh = plsc.VectorSubcoreMesh(core_axis_name="core", subcore_axis_name="subcore")

    @functools.partial(pl.kernel, out_type=jax.ShapeDtypeStruct((n, d), table.dtype), mesh=mesh)
    def gather(x_hbm, i_hbm, o_hbm):
        def body(i_vmem, o_vmem):
            pltpu.sync_copy(x_hbm.at[i_vmem.at[0]], o_vmem)

        pltpu.emit_pipeline(
            body, grid=(2, half),
            in_specs=[pl.BlockSpec((1, SC_GATHER_WINDOW), index_map=lambda c, i: (0, c * half + i))],
            out_specs=[pl.BlockSpec((SC_GATHER_WINDOW, d), index_map=lambda c, i: (c * half + i, 0))],
            core_axis_name=("core", "subcore"), dimension_semantics=(pltpu.PARALLEL, pltpu.PARALLEL),
        )(i_hbm, o_hbm)

    return gather(table, idx.reshape(1, n))


def _smp_attn_g_kernel(rb_ref, q_ref, g_ref, kg_ref, vg_ref, pos_ref, knew_ref, vnew_ref, keynew_ref, thr_ref,
                       tab_ref, o_ref, *, n_tok, n_kv, n_past):
    heads = q_ref.shape[1] // n_tok
    grp = heads // n_kv
    topk = pos_ref.shape[2]
    row_kv = lax.broadcasted_iota(I32, (heads, LANES), 0) // grp
    lane = lax.broadcasted_iota(I32, (1, LANES), 1)

    def by_kv(x):
        return jnp.concatenate([jnp.where(row_kv[:, :1] == kv, x, jnp.zeros_like(x)) for kv in range(n_kv)], axis=1)

    def new_rows(ref, kv):
        return ref[0, pl.ds(kv, PAGE_SIZE, stride=n_kv), :].astype(BF16)

    knew = jnp.concatenate([new_rows(knew_ref, kv) for kv in range(n_kv)], axis=1)
    vnew = jnp.concatenate([new_rows(vnew_ref, kv) for kv in range(n_kv)], axis=0)

    for t in range(n_tok):
        rows = slice(t * heads, (t + 1) * heads)
        blk = lambda ref, kv: ref[(t * n_kv + kv) * topk:(t * n_kv + kv + 1) * topk, :].astype(BF16)
        qcat = by_kv(q_ref[0, rows, :])
        pos = pos_ref[0, t:t + 1, :]
        dist = n_past + t - pos
        bias = jnp.concatenate([_bias_from_dist(dist, rb_ref, h) for h in range(heads)], axis=0)
        s_g = _dot_nt(qcat, jnp.concatenate([blk(kg_ref, kv) for kv in range(n_kv)], axis=1)) + bias
        s_g = jnp.where(pos >= 0, s_g, NEG_INF)
        sel_n = jnp.logical_and(keynew_ref[0, t:t + 1, :] >= thr_ref[0, t:t + 1, :], lane <= t)
        s_n = jnp.where(sel_n, _dot_nt(qcat, knew) + tab_ref[2, rows, :], NEG_INF)
        m = jnp.maximum(jnp.max(s_g, axis=-1, keepdims=True), jnp.max(s_n, axis=-1, keepdims=True))
        p_g, p_n = jnp.exp2(s_g - m), jnp.exp2(s_n - m)
        denom = jnp.sum(p_g, axis=-1, keepdims=True) + jnp.sum(p_n, axis=-1, keepdims=True)
        o = _dot(by_kv(p_g).astype(BF16), jnp.concatenate([blk(vg_ref, kv) for kv in range(n_kv)], axis=0))
        o = o + _dot(by_kv(p_n).astype(BF16), vnew)
        o_ref[0, rows, :] = ((o / denom) * g_ref[0, rows, :]).astype(BF16)


def _smp_attn_g(rel_bias, q_s, g_s, kg, vg, posv, k_new, v_new, keys_new, thr, stab, n_past):
    s, rows, _ = q_s.shape
    n_tok, topk = posv.shape[1], posv.shape[2]
    seq = lambda *tail: pl.BlockSpec((1,) + tail, lambda si: (si,) + (0,) * len(tail))
    gathered = pl.BlockSpec((n_tok * KV_DSA * topk, HEAD_DIM), lambda si: (si, 0))
    return pl.pallas_call(
        functools.partial(_smp_attn_g_kernel, n_tok=n_tok, n_kv=KV_DSA, n_past=n_past),
        out_shape=jax.ShapeDtypeStruct((s, rows, HEAD_DIM), BF16), grid=(s,),
        in_specs=[pl.BlockSpec(memory_space=pltpu.SMEM), seq(rows, HEAD_DIM), seq(rows, HEAD_DIM), gathered, gathered,
                  seq(n_tok, topk), seq(PAGE_SIZE * KV_DSA, HEAD_DIM), seq(PAGE_SIZE * KV_DSA, HEAD_DIM),
                  seq(n_tok, LANES), seq(n_tok, LANES), pl.BlockSpec((3, rows, LANES), lambda si: (0, 0, 0))],
        out_specs=seq(rows, HEAD_DIM),
        compiler_params=_cparams(("arbitrary",)), name="smp_attn_gathered",
    )(rel_bias, q_s, g_s, kg, vg, posv, k_new, v_new, keys_new, thr, stab)


def _out_kernel(x_ref, gf_ref, gd_ref, p_ref, wo_ref, wpg_ref, wple_ref, gple_ref, o_ref):
    h = x_ref[...] + _dot(gf_ref[...], wo_ref[:W_HALF, :]) + _dot(gd_ref[...], wo_ref[W_HALF:, :])
    gate = _sigmoid(_dot(h.astype(BF16), wpg_ref[...]))
    e = _dot(p_ref[...].astype(BF16), wple_ref[...])
    e = e * lax.rsqrt(jnp.mean(e * e, axis=-1, keepdims=True) + EPS) * gple_ref[...]
    o_ref[...] = h + gate * e


def _out(x2, gf, gd, p2, wo, wpg, wple, gple, tm):
    rows = x2.shape[0]
    const = lambda shape: pl.BlockSpec(shape, lambda i: (0, 0), pipeline_mode=pl.Buffered(1))
    rmap = lambda i: (i, 0)
    return pl.pallas_call(
        _out_kernel, out_shape=jax.ShapeDtypeStruct((rows, D_MODEL), F32), grid=(rows // tm,),
        in_specs=[pl.BlockSpec((tm, D_MODEL), rmap), pl.BlockSpec((tm, W_HALF), rmap), pl.BlockSpec((tm, W_HALF), rmap),
                  pl.BlockSpec((tm, PLE_DIM), rmap),
                  const((D_MODEL, D_MODEL)), const((D_MODEL, D_MODEL)), const((PLE_DIM, D_MODEL)), const((1, D_MODEL))],
        out_specs=pl.BlockSpec((tm, D_MODEL), rmap),
        compiler_params=_cparams(("arbitrary",)), name="out",
    )(x2, gf, gd, p2, wo, wpg, wple, gple)


def _prep_w_in(w):
    wt = w.T.astype(BF16)
    points = [sum(SPLIT_SIZES[:i + 1]) for i in range(len(SPLIT_SIZES) - 1)]
    fq, fk, fv, ff, fz, dq, dk, dv, dz, iq, ik, iw = jnp.split(wt, points, axis=0)
    seg = dict(fq=[fq], fkv=[fk, fv], fz=[fz], dq=[dq], dkv=[dk, dv], dz=[dz], iq=[iq])
    main = jnp.concatenate([part for n in PROJ_ORDER for part in seg[n]], axis=0)
    pad = jnp.zeros((LANES - D_IDX - H_IDX - H_FOX, w.shape[0]), BF16)
    return main, jnp.concatenate([ik, iw, ff, pad], axis=0)


def _pick_tile(n, pref):
    t = pref
    while n % t:
        t //= 2
    return t


def kernel(x_prompt, x_sample, cache_fox_k, cache_fox_v, cache_fox_logf, cache_dsa_k, cache_dsa_v, cache_idx_k,
           page_table, p_prompt, p_sample, rel_bias, norm_in, w_in, b_f, q_norm_fox, k_norm_fox, q_norm_dsa,
           k_norm_dsa, w_out, w_ple, ple_norm, w_pg):
    b, t, _ = x_prompt.shape
    s, n_tok, _ = x_sample.shape
    n_pages = page_table.shape[1]
    n_past = n_pages * PAGE_SIZE
    n_pool = cache_fox_k.shape[1]
    assert cache_fox_k.shape[0] == 1 and n_tok * H_FOX == 32

    w_main, w_small = _prep_w_in(w_in[0])
    nin = norm_in[0].reshape(1, D_MODEL)
    gains = jnp.concatenate([q_norm_fox, k_norm_fox, q_norm_dsa, k_norm_dsa], axis=0)
    bf = b_f[0].reshape(H_FOX, 1)

    def project(x2, calls, tm):
        res = _proj(x2, nin, w_main, w_small, gains, bf, calls[0], tm, True)
        for steps in calls[1:]:
            res.update(_proj(res["xn"], nin, w_main, w_small, gains, bf, steps, tm, False))
        (fq16,), (fk, fk16, fv, fv16), (fg,) = res["fq"], res["fkv"], res["fz"]
        (dq16,), (dk, dk16, dv, dv16), (dg,) = res["dq"], res["dkv"], res["dz"]
        return (fq16, fk, fk16, fv, fv16, fg, dq16, dk, dk16, dv, dv16, dg, res["iq"][0]) + res["small"]
    wo = w_out[0].astype(BF16)
    wpg = w_pg[0].astype(BF16)
    wple = w_ple[0].astype(BF16)
    gple = ple_norm[0].reshape(1, D_MODEL)

    tq_d = _pick_tile(t, 256)
    ptab, stab = _bias_tables(rel_bias, tq_d)

    rows_p = b * t
    xp = x_prompt.reshape(rows_p, D_MODEL)
    (fq16, fk, fk16, fv, fv16, fg, dq16, dk, dk16, dv, dv16, dg, iq16, small, logft) = project(
        xp, PROJ_CALLS, _pick_tile(rows_p, 1024))
    tq_f = _pick_tile(t, 256)
    ct = _cumsum_prompt(logft, t)
    c4 = ct.reshape(KV_FOX, 2, rows_p // tq_f, tq_f)
    gf = _fox_prompt(fq16, fk16, fv16, c4, fg, b, t, tq_f)
    ik16 = small[:, SM_IK:SM_IK + D_IDX].astype(BF16)
    ik2 = jnp.concatenate([ik16, ik16], axis=1)
    topk_p = min(TOPK_MAX, t // 4)
    gd = _dsa_prompt(rel_bias, iq16, ik2, small, dq16, dk16, dv16, dg, ptab, b, t, tq_d, topk_p)
    y_p = _out(xp, gf, gd, p_prompt[0].reshape(rows_p, PLE_DIM), wo, wpg, wple, gple, _pick_tile(rows_p, 256))

    rows_s = s * n_tok
    xs = x_sample.reshape(rows_s, D_MODEL)
    (sfq16, sfk, _, sfv, _, sfg, sdq16, sdk, _, sdv, _, sdg, siq16, ssmall, slogft) = project(
        xs, (tuple(n for steps in PROJ_CALLS for n in steps if n != "small") + ("small",),), _pick_tile(rows_s, 512))
    pps = _pick_tile(n_pages, 32)
    rows_q = n_tok * H_FOX

    def new_page(a):
        a = a.reshape(s, n_tok * KV_FOX, HEAD_DIM)
        return jnp.pad(a, ((0, 0), (0, (PAGE_SIZE - n_tok) * KV_FOX), (0, 0)))

    q_rows = lambda a: a.reshape(s, rows_q, HEAD_DIM)
    pool = lambda c: c[0].reshape(n_pool, PAGE_SIZE * KV_FOX, HEAD_DIM)
    lf_pool = jnp.transpose(cache_fox_logf[0], (0, 2, 1))
    lf_new = jnp.pad(jnp.transpose(slogft.reshape(H_FOX, s, n_tok), (1, 0, 2)), ((0, 0), (0, 0), (0, PAGE_SIZE - n_tok)))
    sgf = _smp_fox(page_table, q_rows(sfq16), q_rows(sfg), pool(cache_fox_k), pool(cache_fox_v), lf_pool,
                   new_page(sfk), new_page(sfv), lf_new, pps)

    iq_s = siq16.reshape(s, n_tok * H_IDX, D_IDX)
    w_s = (ssmall[:, SM_IW:SM_IW + H_IDX] * (H_IDX ** -0.5)).reshape(s, n_tok * H_IDX, 1)
    ik_new = jnp.pad(jnp.transpose(ssmall[:, SM_IK:SM_IK + D_IDX].reshape(s, n_tok, D_IDX), (0, 2, 1)),
                     ((0, 0), (0, 0), (0, PAGE_SIZE - n_tok)))
    keys, keys_new = _smp_scores(page_table, iq_s, w_s, jnp.transpose(cache_idx_k[0], (0, 2, 1)), ik_new,
                                 _pick_tile(n_pages, 64))
    topk_s = min(TOPK_MAX, (n_past + n_tok) // 4)
    thr = _smp_thresh(keys.reshape(rows_s, n_past), keys_new.reshape(rows_s, LANES), topk_s).reshape(s, n_tok, LANES)
    assert n_pool <= 256 * PT_SPLIT and topk_s % LANES == 0
    pt_col = lambda a: jnp.broadcast_to(a.astype(BF16)[:, :, None], (s, n_pages, LANES))
    phys, posv = _smp_compact(keys.reshape(s, n_tok, n_pages, PAGE_SIZE), thr,
                              pt_col(page_table // PT_SPLIT), pt_col(page_table % PT_SPLIT), topk_s)
    idx = (phys.reshape(s, n_tok, 1, topk_s) * KV_DSA + jnp.arange(KV_DSA, dtype=I32).reshape(1, 1, KV_DSA, 1)).reshape(-1)
    kv_rows = lambda c: c[0].reshape(n_pool * PAGE_SIZE * KV_DSA, HEAD_DIM)
    kg = _sc_gather_rows(kv_rows(cache_dsa_k), idx)
    vg = _sc_gather_rows(kv_rows(cache_dsa_v), idx)
    sgd = _smp_attn_g(rel_bias, q_rows(sdq16), q_rows(sdg), kg, vg, posv.reshape(s, n_tok, topk_s),
                      new_page(sdk), new_page(sdv), keys_new, thr, stab, n_past)
    y_s = _out(xs, sgf.reshape(rows_s, W_HALF), sgd.reshape(rows_s, W_HALF), p_sample[0].reshape(rows_s, PLE_DIM),
               wo, wpg, wple, gple, _pick_tile(rows_s, 256))

    def kv5(a, bb, tt):
        return a.reshape(1, bb, tt, KV_FOX, HEAD_DIM)

    def outs(bb, tt, fk_, fv_, logft_, dk_, dv_, small_):
        return (kv5(fk_, bb, tt), kv5(fv_, bb, tt), logft_.T.reshape(1, bb, tt, H_FOX),
                kv5(dk_, bb, tt), kv5(dv_, bb, tt), small_[:, SM_IK:SM_IK + D_IDX].reshape(1, bb, tt, D_IDX))

    return ((y_p.reshape(b, t, D_MODEL), y_s.reshape(s, n_tok, D_MODEL))
            + outs(b, t, fk, fv, logft, dk, dv, small)
            + outs(s, n_tok, sfk, sfv, slogft, sdk, sdv, ssmall))
```
